```python
import math
import jax, jax.numpy as jnp
from jax import lax
import numpy as np

D_MODEL = 1024
BATCH = 2
SEQ = 16384
DEPTH = 1
DEC_BATCH = 16
DEC_SEQ = 4096
PAST_LEN = 128

N_MEM = 256
BLOCK = 128
DA_HEADS = 4
DA_DQK = 64
DA_DV = 2 * DA_DQK
WA_HEADS = 8
WA_KV_HEADS = 2
WA_DH = 64
WINDOW = 128
MEM_HEADS = 4
MEM_DH = 128
BRANCH_W = 512
DA_Q = DA_HEADS * 2 * DA_DQK
DA_K = DA_HEADS * 2 * DA_DQK
DA_V = DA_HEADS * DA_DV
WA_Q = WA_HEADS * WA_DH
WA_K = WA_KV_HEADS * WA_DH
WA_V = WA_KV_HEADS * WA_DH
MEM_Q = MEM_HEADS * MEM_DH
IN_COLS = DA_Q + DA_K + DA_V + WA_Q + WA_K + WA_V + MEM_Q
IN_SPLITS = (DA_Q, DA_Q + DA_K, DA_Q + DA_K + DA_V, DA_Q + DA_K + DA_V + WA_Q,
             DA_Q + DA_K + DA_V + WA_Q + WA_K, DA_Q + DA_K + DA_V + WA_Q + WA_K + WA_V)
N_BRANCH = 3
PEER_HEADS = 8
N_KEYS = 128
N_EXPERTS = N_KEYS * N_KEYS
PEER_DK = 256
PEER_HALF = PEER_DK // 2
PEER_TOPK = 16
PEER_CHUNK = 128
ALPHA = (2.0 * DEPTH) ** 0.25
BETA = (8.0 * DEPTH) ** -0.25
LN_EPS = 1e-5
NEG = -1e30

kernel_name = "hybrid_diffattn_swa_mem_peer_encoder"


def layer_norm(x, g, b):
    x32 = x.astype(jnp.float32)
    mu = jnp.mean(x32, axis=-1, keepdims=True)
    var = jnp.mean(jnp.square(x32 - mu), axis=-1, keepdims=True)
    y = (x32 - mu) * lax.rsqrt(var + LN_EPS)
    return (y * g.astype(jnp.float32) + b.astype(jnp.float32)).astype(x.dtype)


def alibi_slopes(n):
    return jnp.exp2(-8.0 * jnp.arange(1, n + 1, dtype=jnp.float32) / n)


def diff_attention(q, k, v, lam, slopes):
    B, S, H, _, d = q.shape
    nb = S // BLOCK
    scale = d ** -0.5
    kpos = jnp.arange(S)
    qb = q.reshape(B, nb, BLOCK, H, 2, d).transpose(1, 0, 2, 3, 4, 5)

    def one_block(args):
        qi, i = args
        qpos = i * BLOCK + jnp.arange(BLOCK)
        dist = jnp.abs(qpos[:, None] - kpos[None, :]).astype(jnp.float32)
        bias = -slopes[:, None, None] * dist
        s = jnp.einsum('bqhmd,bkhmd->bmhqk', qi, k,
                       preferred_element_type=jnp.float32) * scale + bias
        p = jax.nn.softmax(s, axis=-1)
        pdiff = p[:, 0] - lam * p[:, 1]
        return jnp.einsum('bhqk,bkhe->bqhe', pdiff.astype(v.dtype), v)

    out = lax.map(one_block, (qb, jnp.arange(nb)))
    return out.transpose(1, 0, 2, 3, 4).reshape(B, S, H, v.shape[-1])


def window_attention(q, k, v, sink, slopes):
    S, Hq, d = q.shape
    G = k.shape[1]
    R = Hq // G
    nb = S // BLOCK

    def bands(t):
        tb = jnp.pad(t, ((BLOCK, BLOCK), (0, 0), (0, 0))).reshape(nb + 2, BLOCK, G, d)
        return jnp.concatenate([tb[:-2], tb[1:-1], tb[2:]], axis=1)

    kw, vw = bands(k), bands(v)
    qb = q.reshape(nb, BLOCK, G, R, d)
    s = jnp.einsum('nqgrd,nkgd->ngrqk', qb, kw,
                   preferred_element_type=jnp.float32) * (d ** -0.5)
    blk = jnp.arange(nb)[:, None] * BLOCK
    qpos = blk + jnp.arange(BLOCK)[None, :]
    kpos = blk - BLOCK + jnp.arange(3 * BLOCK)[None, :]
    rel = jnp.abs(qpos[:, :, None] - kpos[:, None, :])
    valid = (rel <= WINDOW) & (kpos[:, None, :] >= 0) & (kpos[:, None, :] < S)
    sl = slopes.reshape(G, R)[None, :, :, None, None]
    s = jnp.where(valid[:, None, None], s - sl * rel[:, None, None].astype(jnp.float32), NEG)
    sk = sink.astype(jnp.float32).reshape(G, R)[None, :, :, None, None]
    m = jnp.maximum(jnp.max(s, axis=-1, keepdims=True), sk)
    e = jnp.exp(s - m)
    p = e / (jnp.sum(e, axis=-1, keepdims=True) + jnp.exp(sk - m))
    o = jnp.einsum('ngrqk,nkgd->nqgrd', p.astype(v.dtype), vw)
    return o.reshape(S, Hq, d)


def memory_attention(q, mk, mv):
    s = jnp.einsum('bshd,bmhd->bhsm', q, mk,
                   preferred_element_type=jnp.float32) * (q.shape[-1] ** -0.5)
    p = jax.nn.softmax(s, axis=-1)
    return jnp.einsum('bhsm,bmhd->bshd', p.astype(mv.dtype), mv)


def peer(x, w_pq, sub_keys, peer_u, peer_v):
    B, S, D = x.shape
    xt = x.reshape(-1, PEER_CHUNK, D)

    def one_chunk(xc):
        q = (xc @ w_pq).reshape(PEER_CHUNK, PEER_HEADS, 2, PEER_HALF)
        s = jnp.einsum('chpd,hpnd->chpn', q, sub_keys,
                       preferred_element_type=jnp.float32)
        top_s, top_i = lax.top_k(s, PEER_TOPK)
        cand_s = top_s[:, :, 0, :, None] + top_s[:, :, 1, None, :]
        cand_i = top_i[:, :, 0, :, None] * N_KEYS + top_i[:, :, 1, None, :]
        cand_s = cand_s.reshape(PEER_CHUNK, PEER_HEADS, PEER_TOPK * PEER_TOPK)
        cand_i = cand_i.reshape(PEER_CHUNK, PEER_HEADS, PEER_TOPK * PEER_TOPK)
        best_s, pos = lax.top_k(cand_s, PEER_TOPK)
        idx = jnp.take_along_axis(cand_i, pos, axis=-1)
        g = jax.nn.softmax(best_s, axis=-1)
        ue = peer_u[idx]
        ve = peer_v[idx]
        a = jax.nn.gelu(jnp.einsum('cd,chkd->chk', xc, ue,
                                   preferred_element_type=jnp.float32), approximate=False)
        w = (g * a).astype(x.dtype)
        return jnp.einsum('chk,chkd->cd', w, ve)

    return lax.map(one_chunk, xt).reshape(B, S, D)


def encoder_layer(x, mem, lam_init, w_in, w_mem_kv, lam_q1, lam_k1, lam_q2, lam_k2, subln_g,
                  sink, w_gate, b_gate, w_pa, w_pb, w_pc, w_o, ln1_g, ln1_b,
                  w_pq, sub_keys, peer_u, peer_v, ln2_g, ln2_b):
    B, S, D = x.shape
    f32 = jnp.float32
    proj = x @ w_in
    qa, ka, va, qb, kb, vb, qc = jnp.split(proj, IN_SPLITS, axis=-1)

    lam = (jnp.exp(jnp.sum(lam_q1.astype(f32) * lam_k1.astype(f32)))
           - jnp.exp(jnp.sum(lam_q2.astype(f32) * lam_k2.astype(f32))) + lam_init)
    oa = diff_attention(qa.reshape(B, S, DA_HEADS, 2, DA_DQK),
                        ka.reshape(B, S, DA_HEADS, 2, DA_DQK),
                        va.reshape(B, S, DA_HEADS, DA_DV), lam, alibi_slopes(DA_HEADS))
    oa32 = oa.astype(f32)
    oa = (oa32 * lax.rsqrt(jnp.mean(jnp.square(oa32), axis=-1, keepdims=True) + LN_EPS)
          * subln_g.astype(f32) * (1.0 - lam_init)).astype(x.dtype).reshape(B, S, BRANCH_W)

    slopes_b = alibi_slopes(WA_HEADS)
    ob = lax.map(lambda t: window_attention(t[0], t[1], t[2], sink, slopes_b),
                 (qb.reshape(B, S, WA_HEADS, WA_DH),
                  kb.reshape(B, S, WA_KV_HEADS, WA_DH),
                  vb.reshape(B, S, WA_KV_HEADS, WA_DH))).reshape(B, S, BRANCH_W)

    mk, mv = jnp.split(mem @ w_mem_kv, 2, axis=-1)
    M = mem.shape[1]
    oc = memory_attention(qc.reshape(B, S, MEM_HEADS, MEM_DH),
                          mk.reshape(B, M, MEM_HEADS, MEM_DH),
                          mv.reshape(B, M, MEM_HEADS, MEM_DH)).reshape(B, S, BRANCH_W)

    gates = jax.nn.sigmoid(x @ w_gate + b_gate).reshape(B, S, N_BRANCH, D)
    merged = gates[:, :, 0] * (oa @ w_pa) + gates[:, :, 1] * (ob @ w_pb) + gates[:, :, 2] * (oc @ w_pc)
    x1 = layer_norm(ALPHA * x + merged @ w_o, ln1_g, ln1_b)

    return layer_norm(ALPHA * x1 + peer(x1, w_pq, sub_keys, peer_u, peer_v), ln2_g, ln2_b)


def setup_inputs(seed: int = 0) -> dict:
    key = jax.random.key(seed)
    ks = jax.random.split(key, 32)
    L, D = DEPTH, D_MODEL

    def nrm(k, shape, s):
        return jax.random.normal(k, shape, jnp.float32) * s

    return {
        "x_prompt": nrm(ks[0], (BATCH, SEQ, D), 1.0),
        "x_sample": nrm(ks[1], (DEC_BATCH, DEC_SEQ, D), 1.0),
        "mem_prompt": nrm(ks[2], (BATCH, N_MEM, D), 1.0),
        "mem_sample": nrm(ks[3], (DEC_BATCH, N_MEM, D), 1.0),
        "w_in": nrm(ks[4], (L, D, IN_COLS), D ** -0.5),
        "w_mem_kv": nrm(ks[5], (L, D, 2 * MEM_HEADS * MEM_DH), D ** -0.5),
        "lam_q1": nrm(ks[6], (L, DA_DQK), 0.1),
        "lam_k1": nrm(ks[7], (L, DA_DQK), 0.1),
        "lam_q2": nrm(ks[8], (L, DA_DQK), 0.1),
        "lam_k2": nrm(ks[9], (L, DA_DQK), 0.1),
        "subln_g": 1.0 + nrm(ks[10], (L, DA_DV), 0.02),
        "sink": nrm(ks[11], (L, WA_HEADS), 0.5),
        "w_gate": nrm(ks[12], (L, D, N_BRANCH * D), D ** -0.5),
        "b_gate": nrm(ks[13], (L, N_BRANCH * D), 0.02),
        "w_pa": nrm(ks[14], (L, BRANCH_W, D), BETA * BRANCH_W ** -0.5),
        "w_pb": nrm(ks[15], (L, BRANCH_W, D), BETA * BRANCH_W ** -0.5),
        "w_pc": nrm(ks[16], (L, BRANCH_W, D), BETA * BRANCH_W ** -0.5),
        "w_o": nrm(ks[17], (L, D, D), BETA * D ** -0.5),
        "ln1_g": 1.0 + nrm(ks[18], (L, D), 0.02),
        "ln1_b": nrm(ks[19], (L, D), 0.02),
        "w_pq": nrm(ks[20], (L, D, PEER_HEADS * PEER_DK), D ** -0.5),
        "sub_keys": nrm(ks[21], (L, PEER_HEADS, 2, N_KEYS, PEER_HALF), PEER_HALF ** -0.5),
        "peer_u": nrm(ks[22], (L, N_EXPERTS, D), D ** -0.5),
        "peer_v": nrm(ks[23], (L, N_EXPERTS, D), BETA * PEER_HEADS ** -0.5),
        "ln2_g": 1.0 + nrm(ks[24], (L, D), 0.02),
        "ln2_b": nrm(ks[25], (L, D), 0.02),
    }


def reference(x_prompt, x_sample, mem_prompt, mem_sample, w_in, w_mem_kv, lam_q1, lam_k1,
              lam_q2, lam_k2, subln_g, sink, w_gate, b_gate, w_pa, w_pb, w_pc, w_o,
              ln1_g, ln1_b, w_pq, sub_keys, peer_u, peer_v, ln2_g, ln2_b):
    y_prompt, y_sample = x_prompt, x_sample
    for l in range(DEPTH):
        lam_init = 0.8 - 0.6 * math.exp(-0.3 * l)
        y_prompt = encoder_layer(y_prompt, mem_prompt, lam_init, w_in[l], w_mem_kv[l],
                                 lam_q1[l], lam_k1[l], lam_q2[l], lam_k2[l], subln_g[l], sink[l],
                                 w_gate[l], b_gate[l], w_pa[l], w_pb[l], w_pc[l], w_o[l],
                                 ln1_g[l], ln1_b[l], w_pq[l], sub_keys[l], peer_u[l], peer_v[l],
                                 ln2_g[l], ln2_b[l])
        y_sample = encoder_layer(y_sample, mem_sample, lam_init, w_in[l], w_mem_kv[l],
                                 lam_q1[l], lam_k1[l], lam_q2[l], lam_k2[l], subln_g[l], sink[l],
                                 w_gate[l], b_gate[l], w_pa[l], w_pb[l], w_pc[l], w_o[l],
                                 ln1_g[l], ln1_b[l], w_pq[l], sub_keys[l], peer_u[l], peer_v[l],
                                 ln2_g[l], ln2_b[l])
    return (y_prompt, y_sample)
```

```python
import functools
import math

import jax
import jax.numpy as jnp
import numpy as np
from jax import lax
from jax.experimental import pallas as pl
from jax.experimental.pallas import tpu as pltpu

F32 = jnp.float32
BF16 = jnp.bfloat16

D_MODEL = 1024
N_MEM = 256
BLOCK = 128
DA_HEADS = 4
WA_HEADS = 8
MEM_HEADS = 4
MEM_DH = 128
BRANCH_W = 512
PEER_HEADS = 8
N_KEYS = 128
N_EXPERTS = N_KEYS * N_KEYS
PEER_TOPK = 16
LN_EPS = 1e-5
NEG = -1e30
LANES = 128
SUBLANES = 8

COL_QA, COL_KA, COL_VA, COL_QB, COL_QC, COL_KB, COL_VB = 0, 512, 1024, 1536, 2048, 2560, 2688
PROJ_COLS = 2816

VMEM_LIMIT = 56 * 1024 * 1024

_NT = (((1,), (1,)), ((), ()))


def _layer_norm(z, g, b):
    mu = jnp.mean(z, axis=-1, keepdims=True)
    zc = z - mu
    var = jnp.mean(zc * zc, axis=-1, keepdims=True)
    return zc * lax.rsqrt(var + LN_EPS) * g + b


def _proj_kernel(x_ref, w_ref, o_ref, *, n_chunk):
    xb = x_ref[...].astype(BF16)
    for c in range(0, o_ref.shape[-1], n_chunk):
        o_ref[:, c:c + n_chunk] = jnp.dot(
            xb, w_ref[:, c:c + n_chunk], preferred_element_type=F32).astype(BF16)


def _proj(x2d, w, tm):
    t, d = x2d.shape
    n = w.shape[1]
    return pl.pallas_call(
        functools.partial(_proj_kernel, n_chunk=256),
        out_shape=jax.ShapeDtypeStruct((t, n), BF16),
        grid=(t // tm,),
        in_specs=[pl.BlockSpec((tm, d), lambda i: (i, 0)),
                  pl.BlockSpec((d, n), lambda i: (0, 0))],
        out_specs=pl.BlockSpec((tm, n), lambda i: (i, 0)),
        compiler_params=pltpu.CompilerParams(
            dimension_semantics=("parallel",), vmem_limit_bytes=VMEM_LIMIT),
        name="proj",
    )(x2d, w)


def _attn_a_kernel(slopes_ref, lamp_ref, g_ref, q_ref, k_ref, v_ref, o_ref,
                   m_ref, l_ref, acc_ref, *, tq, tk, lam_init):
    h = pl.program_id(1)
    i = pl.program_id(2)
    nk = k_ref.shape[0] // tk
    slope = slopes_ref[h]

    q = q_ref[...]
    lane = lax.broadcasted_iota(jnp.int32, q.shape, 1)
    zero = jnp.zeros_like(q)
    q2 = jnp.concatenate([jnp.where(lane < 64, q, zero),
                          jnp.where(lane >= 64, q, zero)], axis=0)

    m_ref[...] = jnp.full(m_ref.shape, -jnp.inf, F32)
    l_ref[...] = jnp.zeros(l_ref.shape, F32)
    acc_ref[...] = jnp.zeros(acc_ref.shape, F32)

    d0 = (lax.broadcasted_iota(jnp.int32, (tq, tk), 0)
          - lax.broadcasted_iota(jnp.int32, (tq, tk), 1)).astype(F32)
    q_off = (i * tq).astype(F32)

    def body(j, carry):
        ks = pl.multiple_of(j * tk, tk)
        kt = k_ref[pl.ds(ks, tk), :]
        vt = v_ref[pl.ds(ks, tk), :]
        s = lax.dot_general(q2, kt, _NT, preferred_element_type=F32)
        bias = jnp.abs(d0 + (q_off - (j * tk).astype(F32))) * slope
        s = (s.reshape(2, tq, tk) - bias[None]).reshape(2 * tq, tk)
        m_prev = m_ref[...]
        m_new = jnp.maximum(m_prev, jnp.max(s, axis=1, keepdims=True))
        alpha = jnp.exp(m_prev - m_new)
        p = jnp.exp(s - m_new)
        l_ref[...] = alpha * l_ref[...] + jnp.sum(p, axis=1, keepdims=True)
        acc_ref[...] = alpha * acc_ref[...] + jnp.dot(
            p.astype(BF16), vt, preferred_element_type=F32)
        m_ref[...] = m_new
        return carry

    lax.fori_loop(0, nk, body, 0)

    lamp = lamp_ref[...]
    lam = (jnp.exp(jnp.sum(lamp[0:1] * lamp[1:2], axis=1, keepdims=True))
           - jnp.exp(jnp.sum(lamp[2:3] * lamp[3:4], axis=1, keepdims=True)) + lam_init)
    o = acc_ref[...] / l_ref[...]
    o = o[:tq] - lam * o[tq:]
    ms = jnp.mean(o * o, axis=-1, keepdims=True)
    y = o * lax.rsqrt(ms + LN_EPS) * g_ref[...] * (1.0 - lam_init)
    o_ref[...] = y.astype(o_ref.dtype)


def _attn_a(proj, slopes, lamp, subln_g, lam_init, tq, tk):
    b, s, _ = proj.shape
    kern = functools.partial(_attn_a_kernel, tq=tq, tk=tk, lam_init=lam_init)
    return pl.pallas_call(
        kern,
        out_shape=jax.ShapeDtypeStruct((b, s, BRANCH_W), BF16),
        grid=(b, DA_HEADS, s // tq),
        in_specs=[
            pl.BlockSpec(memory_space=pltpu.SMEM),
            pl.BlockSpec((4, 64), lambda bb, h, i: (0, 0)),
            pl.BlockSpec((1, LANES), lambda bb, h, i: (0, 0)),
            pl.BlockSpec((None, tq, LANES), lambda bb, h, i: (bb, i, COL_QA // LANES + h)),
            pl.BlockSpec((None, s, LANES), lambda bb, h, i: (bb, 0, COL_KA // LANES + h)),
            pl.BlockSpec((None, s, LANES), lambda bb, h, i: (bb, 0, COL_VA // LANES + h)),
        ],
        out_specs=pl.BlockSpec((None, tq, LANES), lambda bb, h, i: (bb, i, h)),
        scratch_shapes=[pltpu.VMEM((2 * tq, 1), F32), pltpu.VMEM((2 * tq, 1), F32),
                        pltpu.VMEM((2 * tq, LANES), F32)],
        compiler_params=pltpu.CompilerParams(
            dimension_semantics=("parallel", "parallel", "arbitrary"),
            vmem_limit_bytes=VMEM_LIMIT),
        name="attn_a",
    )(slopes, lamp, subln_g, proj, proj, proj)


def _attn_b_kernel(slopes_ref, sink_ref, q_ref, kp_ref, kc_ref, kn_ref, vp_ref, vc_ref, vn_ref,
                   o_ref, *, tq, seq):
    i = pl.program_id(1)
    kfull = jnp.concatenate([kp_ref[...], kc_ref[...], kn_ref[...]], axis=0)
    vfull = jnp.concatenate([vp_ref[...], vc_ref[...], vn_ref[...]], axis=0)
    band = 3 * BLOCK
    r = lax.broadcasted_iota(jnp.int32, (BLOCK, band), 0)
    c = lax.broadcasted_iota(jnp.int32, (BLOCK, band), 1)
    rel_i = jnp.abs(r + BLOCK - c)
    rel = rel_i.astype(F32)
    lane = lax.broadcasted_iota(jnp.int32, (BLOCK, LANES), 1)
    lo_half = lane < 64

    for sub in range(tq // BLOCK):
        q_start = i * tq + sub * BLOCK
        valid = ((rel_i <= BLOCK) & (c >= BLOCK - q_start) & (c < seq + BLOCK - q_start))
        kband = kfull[sub * BLOCK: sub * BLOCK + band]
        vband = vfull[sub * BLOCK: sub * BLOCK + band]
        qblk = q_ref[sub * BLOCK:(sub + 1) * BLOCK, :]
        parts = []
        for g in range(4):
            qg = qblk[:, g * LANES:(g + 1) * LANES]
            zero = jnp.zeros_like(qg)
            parts.append(jnp.where(lo_half, qg, zero))
            parts.append(jnp.where(lo_half, zero, qg))
        qs = jnp.concatenate(parts, axis=0)
        s_all = lax.dot_general(qs, kband, _NT, preferred_element_type=F32)
        ps, invs = [], []
        for n in range(8):
            hq = (n // 2) + 4 * (n % 2)
            s = s_all[n * BLOCK:(n + 1) * BLOCK]
            s = jnp.where(valid, s - slopes_ref[hq] * rel, NEG)
            sk = sink_ref[hq]
            m = jnp.maximum(jnp.max(s, axis=1, keepdims=True), sk)
            e = jnp.exp(s - m)
            den = jnp.sum(e, axis=1, keepdims=True) + jnp.exp(sk - m)
            ps.append(e.astype(BF16))
            invs.append(1.0 / den)
        p_all = jnp.concatenate(ps, axis=0)
        o_all = jnp.dot(p_all, vband, preferred_element_type=F32)
        for g in range(4):
            o_lo = o_all[(2 * g) * BLOCK:(2 * g + 1) * BLOCK] * invs[2 * g]
            o_hi = o_all[(2 * g + 1) * BLOCK:(2 * g + 2) * BLOCK] * invs[2 * g + 1]
            o_ref[sub * BLOCK:(sub + 1) * BLOCK, g * LANES:(g + 1) * LANES] = jnp.where(
                lo_half, o_lo, o_hi).astype(o_ref.dtype)


def _attn_b(proj, slopes, sink, tq):
    b, s, _ = proj.shape
    nb = s // BLOCK
    r = tq // BLOCK
    kcol, vcol = COL_KB // LANES, COL_VB // LANES

    def prev_map(col):
        return lambda bb, i: (bb, jnp.maximum(i * r - 1, 0), col)

    def cur_map(col):
        return lambda bb, i: (bb, i, col)

    def next_map(col):
        return lambda bb, i: (bb, jnp.minimum(i * r + r, nb - 1), col)

    return pl.pallas_call(
        functools.partial(_attn_b_kernel, tq=tq, seq=s),
        out_shape=jax.ShapeDtypeStruct((b, s, BRANCH_W), BF16),
        grid=(b, s // tq),
        in_specs=[
            pl.BlockSpec(memory_space=pltpu.SMEM),
            pl.BlockSpec(memory_space=pltpu.SMEM),
            pl.BlockSpec((None, tq, BRANCH_W), lambda bb, i: (bb, i, COL_QB // BRANCH_W)),
            pl.BlockSpec((None, BLOCK, LANES), prev_map(kcol)),
            pl.BlockSpec((None, tq, LANES), cur_map(kcol)),
            pl.BlockSpec((None, BLOCK, LANES), next_map(kcol)),
            pl.BlockSpec((None, BLOCK, LANES), prev_map(vcol)),
            pl.BlockSpec((None, tq, LANES), cur_map(vcol)),
            pl.BlockSpec((None, BLOCK, LANES), next_map(vcol)),
        ],
        out_specs=pl.BlockSpec((None, tq, BRANCH_W), lambda bb, i: (bb, i, 0)),
        compiler_params=pltpu.CompilerParams(
            dimension_semantics=("parallel", "parallel"), vmem_limit_bytes=VMEM_LIMIT),
        name="attn_b",
    )(slopes, sink, proj, proj, proj, proj, proj, proj, proj)


def _attn_c_kernel(q_ref, mem_ref, wkv_ref, o_ref, mk_ref, mv_ref):
    @pl.when(pl.program_id(1) == 0)
    def _():
        kv = jnp.dot(mem_ref[...].astype(BF16), wkv_ref[...], preferred_element_type=F32)
        mk_ref[...] = kv[:, :BRANCH_W].astype(BF16)
        mv_ref[...] = kv[:, BRANCH_W:].astype(BF16)

    scale = MEM_DH ** -0.5
    for h in range(MEM_HEADS):
        cols = slice(h * MEM_DH, (h + 1) * MEM_DH)
        s = lax.dot_general(q_ref[:, cols], mk_ref[:, cols], _NT,
                            preferred_element_type=F32) * scale
        m = jnp.max(s, axis=1, keepdims=True)
        e = jnp.exp(s - m)
        inv = 1.0 / jnp.sum(e, axis=1, keepdims=True)
        o = jnp.dot(e.astype(BF16), mv_ref[:, cols], preferred_element_type=F32)
        o_ref[:, cols] = (o * inv).astype(o_ref.dtype)


def _attn_c(proj, mem, wkv, tq):
    b, s, _ = proj.shape
    return pl.pallas_call(
        _attn_c_kernel,
        out_shape=jax.ShapeDtypeStruct((b, s, BRANCH_W), BF16),
        grid=(b, s // tq),
        in_specs=[
            pl.BlockSpec((None, tq, BRANCH_W), lambda bb, i: (bb, i, COL_QC // BRANCH_W)),
            pl.BlockSpec((None, N_MEM, D_MODEL), lambda bb, i: (bb, 0, 0)),
            pl.BlockSpec((D_MODEL, 2 * BRANCH_W), lambda bb, i: (0, 0)),
        ],
        out_specs=pl.BlockSpec((None, tq, BRANCH_W), lambda bb, i: (bb, i, 0)),
        scratch_shapes=[pltpu.VMEM((N_MEM, BRANCH_W), BF16), pltpu.VMEM((N_MEM, BRANCH_W), BF16)],
        compiler_params=pltpu.CompilerParams(
            dimension_semantics=("parallel", "arbitrary"), vmem_limit_bytes=VMEM_LIMIT),
        name="attn_c",
    )(proj, mem, wkv)


def _merge_kernel(x_ref, oa_ref, ob_ref, oc_ref, wg_ref, bg_ref, wpa_ref, wpb_ref, wpc_ref,
                  wo_ref, g_ref, b_ref, o_ref, *, alpha):
    x = x_ref[...]
    xb = x.astype(BF16)
    merged = None
    for n, (br_ref, wp_ref) in enumerate(((oa_ref, wpa_ref), (ob_ref, wpb_ref), (oc_ref, wpc_ref))):
        cols = slice(n * D_MODEL, (n + 1) * D_MODEL)
        gate = jax.nn.sigmoid(
            jnp.dot(xb, wg_ref[:, cols], preferred_element_type=F32) + bg_ref[:, cols])
        term = gate * jnp.dot(br_ref[...], wp_ref[...], preferred_element_type=F32)
        merged = term if merged is None else merged + term
    y = jnp.dot(merged.astype(BF16), wo_ref[...], preferred_element_type=F32)
    o_ref[...] = _layer_norm(alpha * x + y, g_ref[...], b_ref[...])


def _merge(x2d, oa, ob, oc, wg, bg, wpa, wpb, wpc, wo, g, b, alpha, tm):
    t, d = x2d.shape
    const = lambda i: (0, 0)
    row = lambda i: (i, 0)
    return pl.pallas_call(
        functools.partial(_merge_kernel, alpha=alpha),
        out_shape=jax.ShapeDtypeStruct((t, d), F32),
        grid=(t // tm,),
        in_specs=[
            pl.BlockSpec((tm, d), row),
            pl.BlockSpec((tm, BRANCH_W), row),
            pl.BlockSpec((tm, BRANCH_W), row),
            pl.BlockSpec((tm, BRANCH_W), row),
            pl.BlockSpec((d, 3 * d), const),
            pl.BlockSpec((1, 3 * d), const),
            pl.BlockSpec((BRANCH_W, d), const),
            pl.BlockSpec((BRANCH_W, d), const),
            pl.BlockSpec((BRANCH_W, d), const),
            pl.BlockSpec((d, d), const),
            pl.BlockSpec((1, d), const),
            pl.BlockSpec((1, d), const),
        ],
        out_specs=pl.BlockSpec((tm, d), row),
        compiler_params=pltpu.CompilerParams(
            dimension_semantics=("parallel",), vmem_limit_bytes=VMEM_LIMIT),
        name="merge",
    )(x2d, oa, ob, oc, wg, bg, wpa, wpb, wpc, wo, g, b)


def _sort_network(n):
    pairs = []

    def merge(lo, hi, r):
        step = r * 2
        if step < hi - lo:
            merge(lo, hi, step)
            merge(lo + r, hi, step)
            pairs.extend((k, k + r) for k in range(lo + r, hi - r, step))
        else:
            pairs.append((lo, lo + r))

    def sort(lo, hi):
        if hi - lo >= 1:
            mid = lo + (hi - lo) // 2
            sort(lo, mid)
            sort(mid + 1, hi)
            merge(lo, hi, 1)

    sort(0, n - 1)
    return pairs


_SORT16 = _sort_network(PEER_TOPK)


def _top16_desc(slabs):
    v = list(slabs)
    for a, b in _SORT16:
        hi, lo = jnp.maximum(v[a], v[b]), jnp.minimum(v[a], v[b])
        v[a], v[b] = hi, lo
    for shift in (4, 2, 1):
        v = [jnp.maximum(v[k], pltpu.roll(v[PEER_TOPK - 1 - k], shift, 0))
             for k in range(PEER_TOPK)]
        step = PEER_TOPK // 2
        while step >= 1:
            for k in range(PEER_TOPK):
                if k & step == 0:
                    hi, lo = jnp.maximum(v[k], v[k + step]), jnp.minimum(v[k], v[k + step])
                    v[k], v[k + step] = hi, lo
            step //= 2
    return v


def _peer_route(s0, s1):
    n = s0.shape[1]
    top_a = _top16_desc([s0[k * SUBLANES:(k + 1) * SUBLANES] for k in range(N_KEYS // SUBLANES)])
    top_b = _top16_desc([s1[k * SUBLANES:(k + 1) * SUBLANES] for k in range(N_KEYS // SUBLANES)])
    a0, b0 = top_a[0], top_b[0]
    ea = [jnp.exp(t - a0) for t in top_a]
    eb = [jnp.exp(t - b0) for t in top_b]
    sub = lax.broadcasted_iota(jnp.int32, (SUBLANES, n), 0)
    first4 = sub < 4

    def candidates(ea_list):
        col = ea_list[0]
        for s in range(1, 4):
            col = jnp.where(sub == s, ea_list[s], col)
        for s in range(4, 8):
            col = jnp.where(sub == s, eb[s - 4], col)
        out = []
        for v in range(PEER_TOPK):
            c = col * jnp.where(first4, eb[v], ea_list[v])
            if v < 4:
                c = jnp.where(first4, c, -1.0)
            out.append(c)
        return out

    cand = candidates(ea)
    best = _top16_desc(cand)
    z = best[0]
    for t in best[1:]:
        z = z + t
    inv_z = 1.0 / z
    theta = best[PEER_TOPK - 1]
    ean = [t * inv_z for t in ea]
    cand_n = candidates(ean)
    thr = None
    for c, cn in zip(cand, cand_n):
        t = jnp.where(c >= theta, cn, jnp.inf)
        thr = t if thr is None else jnp.minimum(thr, t)
    for shift in (4, 2, 1):
        thr = jnp.minimum(thr, pltpu.roll(thr, shift, 0))
    e1n = jnp.exp(s0 - a0[0:1]) * inv_z[0:1]
    e2 = jnp.exp(s1 - b0[0:1])
    return e1n, e2, thr[0:1]


def _peer_kernel(x1_ref, wpq_ref, keys_ref, u_ref, vt_ref, g_ref, b_ref, o_ref,
                 xb_ref, e1_ref, e2_ref, thr_ref, acc_ref, wa_ref, *, alpha, ec, lane_chunk):
    j = pl.program_id(1)
    tm = x1_ref.shape[0]

    @pl.when(j == 0)
    def _():
        xb = x1_ref[...].astype(BF16)
        xb_ref[...] = xb
        acc_ref[...] = jnp.zeros(acc_ref.shape, F32)
        for h in range(PEER_HEADS):
            s = []
            for half in range(2):
                r = 2 * h + half
                qt = lax.dot_general(wpq_ref[r * N_KEYS:(r + 1) * N_KEYS, :], xb, _NT,
                                     preferred_element_type=F32)
                s.append(jnp.dot(keys_ref[r], qt.astype(BF16), preferred_element_type=F32))
            e1n, e2, thr = _peer_route(s[0], s[1])
            e1_ref[h] = e1n
            e2_ref[h] = e2
            thr_ref[h:h + 1, :] = thr

    ht = lax.dot_general(u_ref[...], xb_ref[...], _NT, preferred_element_type=F32)
    n1 = ec // N_KEYS
    for ii in range(n1):
        i1 = j * n1 + ii
        rows = slice(ii * N_KEYS, (ii + 1) * N_KEYS)
        for lc in range(0, tm, lane_chunk):
            cols = slice(lc, lc + lane_chunk)
            w = None
            for h in range(PEER_HEADS):
                p = e1_ref[h, pl.ds(i1, 1), cols] * e2_ref[h, :, cols]
                t = jnp.where(p >= thr_ref[h:h + 1, cols], p, 0.0)
                w = t if w is None else w + t
            hblk = ht[rows, cols]
            act = 0.5 * hblk * (1.0 + lax.erf(hblk * (2.0 ** -0.5)))
            wa_ref[rows, cols] = (w * act).astype(BF16)
    acc_ref[...] += jnp.dot(vt_ref[...], wa_ref[...], preferred_element_type=F32)

    @pl.when(j == pl.num_programs(1) - 1)
    def _():
        z = alpha * x1_ref[...] + acc_ref[...].T
        o_ref[...] = _layer_norm(z, g_ref[...], b_ref[...])


def _peer(x1, wpq_t, keys, u, vt, g, b, alpha, tm, ec, lane_chunk):
    t, d = x1.shape
    kern = functools.partial(_peer_kernel, alpha=alpha, ec=ec, lane_chunk=lane_chunk)
    return pl.pallas_call(
        kern,
        out_shape=jax.ShapeDtypeStruct((t, d), F32),
        grid=(t // tm, N_EXPERTS // ec),
        in_specs=[
            pl.BlockSpec((tm, d), lambda i, j: (i, 0)),
            pl.BlockSpec(wpq_t.shape, lambda i, j: (0, 0)),
            pl.BlockSpec(keys.shape, lambda i, j: (0, 0, 0)),
            pl.BlockSpec((ec, d), lambda i, j: (j, 0)),
            pl.BlockSpec((d, ec), lambda i, j: (0, j)),
            pl.BlockSpec((1, d), lambda i, j: (0, 0)),
            pl.BlockSpec((1, d), lambda i, j: (0, 0)),
        ],
        out_specs=pl.BlockSpec((tm, d), lambda i, j: (i, 0)),
        scratch_shapes=[
            pltpu.VMEM((tm, d), BF16),
            pltpu.VMEM((PEER_HEADS, N_KEYS, tm), F32),
            pltpu.VMEM((PEER_HEADS, N_KEYS, tm), F32),
            pltpu.VMEM((PEER_HEADS, tm), F32),
            pltpu.VMEM((d, tm), F32),
            pltpu.VMEM((ec, tm), BF16),
        ],
        compiler_params=pltpu.CompilerParams(
            dimension_semantics=("parallel", "arbitrary"), vmem_limit_bytes=VMEM_LIMIT),
        name="peer",
    )(x1, wpq_t, keys, u, vt, g, b)


def _prep_weights(depth, l, w_in, w_mem_kv, lam_q1, lam_k1, lam_q2, lam_k2, subln_g, sink, w_gate,
                  b_gate, w_pa, w_pb, w_pc, w_o, ln1_g, ln1_b, w_pq, sub_keys, peer_u, peer_v,
                  ln2_g, ln2_b):
    w = w_in[l]
    qb_perm = np.concatenate([np.arange(64) + 64 * hq for g in range(4) for hq in (g, g + 4)])
    qa = w[:, 0:512] * 0.125
    ka, va = w[:, 512:1024], w[:, 1024:1536]
    qb = w[:, 1536:2048][:, qb_perm] * 0.125
    kb, vb, qc = w[:, 2048:2176], w[:, 2176:2304], w[:, 2304:2816]
    w_proj = jnp.concatenate([qa, ka, va, qb, qc, kb, vb], axis=1).astype(BF16)
    lamp = jnp.stack([lam_q1[l], lam_k1[l], lam_q2[l], lam_k2[l]]).astype(F32)
    row = lambda a: a.astype(F32).reshape(1, -1)
    return dict(
        w_proj=w_proj,
        w_mem_kv=w_mem_kv[l].astype(BF16),
        lamp=lamp,
        subln_g=row(subln_g[l]),
        sink=sink[l].astype(F32),
        w_gate=w_gate[l].astype(BF16),
        b_gate=row(b_gate[l]),
        w_pa=w_pa[l].astype(BF16),
        w_pb=w_pb[l][qb_perm].astype(BF16),
        w_pc=w_pc[l].astype(BF16),
        w_o=w_o[l].astype(BF16),
        ln1_g=row(ln1_g[l]), ln1_b=row(ln1_b[l]),
        w_pq_t=w_pq[l].T.astype(BF16),
        keys=sub_keys[l].reshape(2 * PEER_HEADS, N_KEYS, N_KEYS).astype(BF16),
        peer_u=peer_u[l].astype(BF16),
        peer_vt=peer_v[l].T.astype(BF16),
        ln2_g=row(ln2_g[l]), ln2_b=row(ln2_b[l]),
        lam_init=0.8 - 0.6 * math.exp(-0.3 * l),
        alpha=(2.0 * depth) ** 0.25,
    )


def _tile(n, pref):
    t = min(n, pref)
    assert n % t == 0, (n, t)
    return t


def _encoder_layer(x, mem, p):
    b, s, d = x.shape
    t = b * s
    x2d = x.reshape(t, d)
    proj = _proj(x2d, p["w_proj"], _tile(t, 512)).reshape(b, s, PROJ_COLS)
    slopes_a = jnp.asarray(2.0 ** (-8.0 * np.arange(1, DA_HEADS + 1) / DA_HEADS), F32)
    slopes_b = jnp.asarray(2.0 ** (-8.0 * np.arange(1, WA_HEADS + 1) / WA_HEADS), F32)
    oa = _attn_a(proj, slopes_a, p["lamp"], p["subln_g"], p["lam_init"],
                 _tile(s, 256), _tile(s, 512))
    ob = _attn_b(proj, slopes_b, p["sink"], _tile(s, 256))
    oc = _attn_c(proj, mem, p["w_mem_kv"], _tile(s, 512))
    x1 = _merge(x2d, oa.reshape(t, BRANCH_W), ob.reshape(t, BRANCH_W), oc.reshape(t, BRANCH_W),
                p["w_gate"], p["b_gate"], p["w_pa"], p["w_pb"], p["w_pc"], p["w_o"],
                p["ln1_g"], p["ln1_b"], p["alpha"], _tile(t, 256))
    y = _peer(x1, p["w_pq_t"], p["keys"], p["peer_u"], p["peer_vt"], p["ln2_g"], p["ln2_b"],
              p["alpha"], _tile(t, 512), 1024, 256)
    return y.reshape(b, s, d)


def kernel(x_prompt, x_sample, mem_prompt, mem_sample, w_in, w_mem_kv, lam_q1, lam_k1, lam_q2,
           lam_k2, subln_g, sink, w_gate, b_gate, w_pa, w_pb, w_pc, w_o, ln1_g, ln1_b, w_pq,
           sub_keys, peer_u, peer_v, ln2_g, ln2_b):
    depth = w_in.shape[0]
    y_prompt, y_sample = x_prompt, x_sample
    for l in range(depth):
        p = _prep_weights(depth, l, w_in, w_mem_kv, lam_q1, lam_k1, lam_q2, lam_k2, subln_g, sink,
                          w_gate, b_gate, w_pa, w_pb, w_pc, w_o, ln1_g, ln1_b, w_pq, sub_keys,
                          peer_u, peer_v, ln2_g, ln2_b)
        y_prompt = _encoder_layer(y_prompt, mem_prompt, p)
        y_sample = _encoder_layer(y_sample, mem_sample, p)
    return (y_prompt, y_sample)
```

```python
import functools
import math

import jax
import jax.numpy as jnp
import numpy as np
from jax import lax
from jax.experimental import pallas as pl
from jax.experimental.pallas import tpu as pltpu

F32 = jnp.float32
BF16 = jnp.bfloat16

D_MODEL = 1024
N_MEM = 256
BLOCK = 128
DA_HEADS = 4
WA_HEADS = 8
MEM_HEADS = 4
MEM_DH = 128
BRANCH_W = 512
PEER_HEADS = 8
N_KEYS = 128
N_EXPERTS = N_KEYS * N_KEYS
PEER_TOPK = 16
LN_EPS = 1e-5
NEG = -1e30
LANES = 128
SUBLANES = 8

COL_QA, COL_KA, COL_VA, COL_QB, COL_QC, COL_KB, COL_VB = 0, 512, 1024, 1536, 2048, 2560, 2688
PROJ_COLS = 2816

VMEM_LIMIT = 56 * 1024 * 1024

_NT = (((1,), (1,)), ((), ()))


def _layer_norm(z, g, b):
    mu = jnp.mean(z, axis=-1, keepdims=True)
    zc = z - mu
    var = jnp.mean(zc * zc, axis=-1, keepdims=True)
    return zc * lax.rsqrt(var + LN_EPS) * g + b


def _proj_kernel(x_ref, w_ref, o_ref, *, n_chunk):
    xb = x_ref[...].astype(BF16)
    for c in range(0, o_ref.shape[-1], n_chunk):
        o_ref[:, c:c + n_chunk] = jnp.dot(
            xb, w_ref[:, c:c + n_chunk], preferred_element_type=F32).astype(BF16)


def _proj(x2d, w, tm):
    t, d = x2d.shape
    n = w.shape[1]
    return pl.pallas_call(
        functools.partial(_proj_kernel, n_chunk=256),
        out_shape=jax.ShapeDtypeStruct((t, n), BF16),
        grid=(t // tm,),
        in_specs=[pl.BlockSpec((tm, d), lambda i: (i, 0)),
                  pl.BlockSpec((d, n), lambda i: (0, 0))],
        out_specs=pl.BlockSpec((tm, n), lambda i: (i, 0)),
        compiler_params=pltpu.CompilerParams(
            dimension_semantics=("parallel",), vmem_limit_bytes=VMEM_LIMIT),
        name="proj",
    )(x2d, w)


def _attn_a_kernel(slopes_ref, lamp_ref, g_ref, q_ref, k_ref, v_ref, o_ref,
                   gm_ref, m_ref, l_ref, acc_ref, *, tq, n_streams, tk, lam_init):
    h = pl.program_id(1)
    i = pl.program_id(2)
    tg = n_streams * tq
    mid = tg // tk
    nk = k_ref.shape[0] // tk
    slope2 = slopes_ref[h] * math.log2(math.e)
    reps = tk // LANES

    gm_ref[...] = (lax.broadcasted_iota(jnp.int32, (tq, tk), 0)
                   - lax.broadcasted_iota(jnp.int32, (tq, tk), 1)).astype(F32) * slope2
    m_ref[...] = jnp.full(m_ref.shape, -jnp.inf, F32)
    l_ref[...] = jnp.zeros(l_ref.shape, F32)
    acc_ref[...] = jnp.zeros(acc_ref.shape, F32)

    lane = lax.broadcasted_iota(jnp.int32, (tq, LANES), 1)
    q2 = []
    for st in range(n_streams):
        q = q_ref[st * tq:(st + 1) * tq, :]
        zero = jnp.zeros_like(q)
        q2.append(jnp.concatenate([jnp.where(lane < 64, q, zero),
                                   jnp.where(lane >= 64, q, zero)], axis=0))

    def step(st, kt, vt, bias, shift):
        s = lax.dot_general(q2[st], kt, _NT, preferred_element_type=F32)
        x = s + jnp.concatenate([bias, bias], axis=0)
        m_prev = m_ref[st]
        m_next = jnp.maximum(m_prev, jnp.max(x, axis=1, keepdims=True) - shift)
        alpha = jnp.exp2(m_prev - m_next)
        sub = m_next + shift
        p = jnp.exp2(x - jnp.concatenate([sub] * reps, axis=1))
        psum = p[:, 0:LANES]
        for c in range(1, reps):
            psum = psum + p[:, c * LANES:(c + 1) * LANES]
        l_ref[st] = alpha * l_ref[st] + psum
        acc_ref[st] = alpha * acc_ref[st] + jnp.dot(
            p.astype(BF16), vt, preferred_element_type=F32)
        m_ref[st] = m_next

    def tiles(j):
        ks = pl.multiple_of(j * tk, tk)
        return k_ref[pl.ds(ks, tk), :], v_ref[pl.ds(ks, tk), :]

    def left(j, carry):
        kt, vt = tiles(j)
        base = (i * tg - j * tk).astype(F32)
        for st in range(n_streams):
            step(st, kt, vt, -gm_ref[...], slope2 * (base + st * tq))
        return carry

    def right(j, carry):
        kt, vt = tiles(j)
        base = (j * tk - i * tg).astype(F32)
        for st in range(n_streams):
            step(st, kt, vt, gm_ref[...], slope2 * (base - st * tq))
        return carry

    lax.fori_loop(0, i * mid, left, 0)
    for mj in range(mid):
        kt, vt = tiles(i * mid + mj)
        for st in range(n_streams):
            off = st * tq - mj * tk
            if off - (tk - 1) >= 0:
                step(st, kt, vt, -gm_ref[...], slope2 * off)
            elif off + (tq - 1) <= 0:
                step(st, kt, vt, gm_ref[...], slope2 * (-off))
            else:
                step(st, kt, vt, -jnp.abs(gm_ref[...] + slope2 * off), 0.0)
    lax.fori_loop((i + 1) * mid, nk, right, 0)

    lamp = lamp_ref[...]
    lam = (jnp.exp(jnp.sum(lamp[0:1] * lamp[1:2], axis=1, keepdims=True))
           - jnp.exp(jnp.sum(lamp[2:3] * lamp[3:4], axis=1, keepdims=True)) + lam_init)
    for st in range(n_streams):
        o = acc_ref[st] / jnp.sum(l_ref[st], axis=1, keepdims=True)
        o = o[:tq] - lam * o[tq:]
        ms = jnp.mean(o * o, axis=-1, keepdims=True)
        y = o * lax.rsqrt(ms + LN_EPS) * g_ref[...] * (1.0 - lam_init)
        o_ref[st * tq:(st + 1) * tq, :] = y.astype(o_ref.dtype)


def _attn_a(proj, slopes, lamp, subln_g, lam_init, tq, n_streams, tk):
    b, s, _ = proj.shape
    tg = tq * n_streams
    assert tg % tk == 0 and s % tg == 0
    kern = functools.partial(_attn_a_kernel, tq=tq, n_streams=n_streams, tk=tk, lam_init=lam_init)
    return pl.pallas_call(
        kern,
        out_shape=jax.ShapeDtypeStruct((b, s, BRANCH_W), BF16),
        grid=(b, DA_HEADS, s // tg),
        in_specs=[
            pl.BlockSpec(memory_space=pltpu.SMEM),
            pl.BlockSpec((4, 64), lambda bb, h, i: (0, 0)),
            pl.BlockSpec((1, LANES), lambda bb, h, i: (0, 0)),
            pl.BlockSpec((None, tg, LANES), lambda bb, h, i: (bb, i, COL_QA // LANES + h)),
            pl.BlockSpec((None, s, LANES), lambda bb, h, i: (bb, 0, COL_KA // LANES + h)),
            pl.BlockSpec((None, s, LANES), lambda bb, h, i: (bb, 0, COL_VA // LANES + h)),
        ],
        out_specs=pl.BlockSpec((None, tg, LANES), lambda bb, h, i: (bb, i, h)),
        scratch_shapes=[pltpu.VMEM((tq, tk), F32),
                        pltpu.VMEM((n_streams, 2 * tq, LANES), F32),
                        pltpu.VMEM((n_streams, 2 * tq, LANES), F32),
                        pltpu.VMEM((n_streams, 2 * tq, LANES), F32)],
        compiler_params=pltpu.CompilerParams(
            dimension_semantics=("parallel", "parallel", "arbitrary"),
            vmem_limit_bytes=VMEM_LIMIT),
        name="attn_a",
    )(slopes, lamp, subln_g, proj, proj, proj)


def _attn_b_kernel(slopes_ref, sink_ref, q_ref, kp_ref, kc_ref, kn_ref, vp_ref, vc_ref, vn_ref,
                   o_ref, *, tq, seq):
    i = pl.program_id(1)
    kfull = jnp.concatenate([kp_ref[...], kc_ref[...], kn_ref[...]], axis=0)
    vfull = jnp.concatenate([vp_ref[...], vc_ref[...], vn_ref[...]], axis=0)
    band = 3 * BLOCK
    r = lax.broadcasted_iota(jnp.int32, (BLOCK, band), 0)
    c = lax.broadcasted_iota(jnp.int32, (BLOCK, band), 1)
    rel_i = jnp.abs(r + BLOCK - c)
    rel = rel_i.astype(F32)
    lane = lax.broadcasted_iota(jnp.int32, (BLOCK, LANES), 1)
    lo_half = lane < 64

    for sub in range(tq // BLOCK):
        q_start = i * tq + sub * BLOCK
        valid = ((rel_i <= BLOCK) & (c >= BLOCK - q_start) & (c < seq + BLOCK - q_start))
        kband = kfull[sub * BLOCK: sub * BLOCK + band]
        vband = vfull[sub * BLOCK: sub * BLOCK + band]
        qblk = q_ref[sub * BLOCK:(sub + 1) * BLOCK, :]
        parts = []
        for g in range(4):
            qg = qblk[:, g * LANES:(g + 1) * LANES]
            zero = jnp.zeros_like(qg)
            parts.append(jnp.where(lo_half, qg, zero))
            parts.append(jnp.where(lo_half, zero, qg))
        qs = jnp.concatenate(parts, axis=0)
        s_all = lax.dot_general(qs, kband, _NT, preferred_element_type=F32)
        ps, invs = [], []
        for n in range(8):
            hq = (n // 2) + 4 * (n % 2)
            s = s_all[n * BLOCK:(n + 1) * BLOCK]
            s = jnp.where(valid, s - slopes_ref[hq] * rel, NEG)
            sk = sink_ref[hq]
            m = jnp.maximum(jnp.max(s, axis=1, keepdims=True), sk)
            e = jnp.exp(s - m)
            den = jnp.sum(e, axis=1, keepdims=True) + jnp.exp(sk - m)
            ps.append(e.astype(BF16))
            invs.append(1.0 / den)
        p_all = jnp.concatenate(ps, axis=0)
        o_all = jnp.dot(p_all, vband, preferred_element_type=F32)
        for g in range(4):
            o_lo = o_all[(2 * g) * BLOCK:(2 * g + 1) * BLOCK] * invs[2 * g]
            o_hi = o_all[(2 * g + 1) * BLOCK:(2 * g + 2) * BLOCK] * invs[2 * g + 1]
            o_ref[sub * BLOCK:(sub + 1) * BLOCK, g * LANES:(g + 1) * LANES] = jnp.where(
                lo_half, o_lo, o_hi).astype(o_ref.dtype)


def _attn_b(proj, slopes, sink, tq):
    b, s, _ = proj.shape
    nb = s // BLOCK
    r = tq // BLOCK
    kcol, vcol = COL_KB // LANES, COL_VB // LANES

    def prev_map(col):
        return lambda bb, i: (bb, jnp.maximum(i * r - 1, 0), col)

    def cur_map(col):
        return lambda bb, i: (bb, i, col)

    def next_map(col):
        return lambda bb, i: (bb, jnp.minimum(i * r + r, nb - 1), col)

    return pl.pallas_call(
        functools.partial(_attn_b_kernel, tq=tq, seq=s),
        out_shape=jax.ShapeDtypeStruct((b, s, BRANCH_W), BF16),
        grid=(b, s // tq),
        in_specs=[
            pl.BlockSpec(memory_space=pltpu.SMEM),
            pl.BlockSpec(memory_space=pltpu.SMEM),
            pl.BlockSpec((None, tq, BRANCH_W), lambda bb, i: (bb, i, COL_QB // BRANCH_W)),
            pl.BlockSpec((None, BLOCK, LANES), prev_map(kcol)),
            pl.BlockSpec((None, tq, LANES), cur_map(kcol)),
            pl.BlockSpec((None, BLOCK, LANES), next_map(kcol)),
            pl.BlockSpec((None, BLOCK, LANES), prev_map(vcol)),
            pl.BlockSpec((None, tq, LANES), cur_map(vcol)),
            pl.BlockSpec((None, BLOCK, LANES), next_map(vcol)),
        ],
        out_specs=pl.BlockSpec((None, tq, BRANCH_W), lambda bb, i: (bb, i, 0)),
        compiler_params=pltpu.CompilerParams(
            dimension_semantics=("parallel", "parallel"), vmem_limit_bytes=VMEM_LIMIT),
        name="attn_b",
    )(slopes, sink, proj, proj, proj, proj, proj, proj, proj)


def _attn_c_kernel(q_ref, mem_ref, wkv_ref, o_ref, mk_ref, mv_ref):
    @pl.when(pl.program_id(1) == 0)
    def _():
        kv = jnp.dot(mem_ref[...].astype(BF16), wkv_ref[...], preferred_element_type=F32)
        mk_ref[...] = kv[:, :BRANCH_W].astype(BF16)
        mv_ref[...] = kv[:, BRANCH_W:].astype(BF16)

    scale = MEM_DH ** -0.5
    for h in range(MEM_HEADS):
        cols = slice(h * MEM_DH, (h + 1) * MEM_DH)
        s = lax.dot_general(q_ref[:, cols], mk_ref[:, cols], _NT,
                            preferred_element_type=F32) * scale
        m = jnp.max(s, axis=1, keepdims=True)
        e = jnp.exp(s - m)
        inv = 1.0 / jnp.sum(e, axis=1, keepdims=True)
        o = jnp.dot(e.astype(BF16), mv_ref[:, cols], preferred_element_type=F32)
        o_ref[:, cols] = (o * inv).astype(o_ref.dtype)


def _attn_c(proj, mem, wkv, tq):
    b, s, _ = proj.shape
    return pl.pallas_call(
        _attn_c_kernel,
        out_shape=jax.ShapeDtypeStruct((b, s, BRANCH_W), BF16),
        grid=(b, s // tq),
        in_specs=[
            pl.BlockSpec((None, tq, BRANCH_W), lambda bb, i: (bb, i, COL_QC // BRANCH_W)),
            pl.BlockSpec((None, N_MEM, D_MODEL), lambda bb, i: (bb, 0, 0)),
            pl.BlockSpec((D_MODEL, 2 * BRANCH_W), lambda bb, i: (0, 0)),
        ],
        out_specs=pl.BlockSpec((None, tq, BRANCH_W), lambda bb, i: (bb, i, 0)),
        scratch_shapes=[pltpu.VMEM((N_MEM, BRANCH_W), BF16), pltpu.VMEM((N_MEM, BRANCH_W), BF16)],
        compiler_params=pltpu.CompilerParams(
            dimension_semantics=("parallel", "arbitrary"), vmem_limit_bytes=VMEM_LIMIT),
        name="attn_c",
    )(proj, mem, wkv)


def _merge_kernel(x_ref, oa_ref, ob_ref, oc_ref, wg_ref, bg_ref, wpa_ref, wpb_ref, wpc_ref,
                  wo_ref, g_ref, b_ref, o_ref, *, alpha):
    x = x_ref[...]
    xb = x.astype(BF16)
    merged = None
    for n, (br_ref, wp_ref) in enumerate(((oa_ref, wpa_ref), (ob_ref, wpb_ref), (oc_ref, wpc_ref))):
        cols = slice(n * D_MODEL, (n + 1) * D_MODEL)
        gate = jax.nn.sigmoid(
            jnp.dot(xb, wg_ref[:, cols], preferred_element_type=F32) + bg_ref[:, cols])
        term = gate * jnp.dot(br_ref[...], wp_ref[...], preferred_element_type=F32)
        merged = term if merged is None else merged + term
    y = jnp.dot(merged.astype(BF16), wo_ref[...], preferred_element_type=F32)
    o_ref[...] = _layer_norm(alpha * x + y, g_ref[...], b_ref[...])


def _merge(x2d, oa, ob, oc, wg, bg, wpa, wpb, wpc, wo, g, b, alpha, tm):
    t, d = x2d.shape
    const = lambda i: (0, 0)
    row = lambda i: (i, 0)
    return pl.pallas_call(
        functools.partial(_merge_kernel, alpha=alpha),
        out_shape=jax.ShapeDtypeStruct((t, d), F32),
        grid=(t // tm,),
        in_specs=[
            pl.BlockSpec((tm, d), row),
            pl.BlockSpec((tm, BRANCH_W), row),
            pl.BlockSpec((tm, BRANCH_W), row),
            pl.BlockSpec((tm, BRANCH_W), row),
            pl.BlockSpec((d, 3 * d), const),
            pl.BlockSpec((1, 3 * d), const),
            pl.BlockSpec((BRANCH_W, d), const),
            pl.BlockSpec((BRANCH_W, d), const),
            pl.BlockSpec((BRANCH_W, d), const),
            pl.BlockSpec((d, d), const),
            pl.BlockSpec((1, d), const),
            pl.BlockSpec((1, d), const),
        ],
        out_specs=pl.BlockSpec((tm, d), row),
        compiler_params=pltpu.CompilerParams(
            dimension_semantics=("parallel",), vmem_limit_bytes=VMEM_LIMIT),
        name="merge",
    )(x2d, oa, ob, oc, wg, bg, wpa, wpb, wpc, wo, g, b)


def _sort_network(n):
    pairs = []

    def merge(lo, hi, r):
        step = r * 2
        if step < hi - lo:
            merge(lo, hi, step)
            merge(lo + r, hi, step)
            pairs.extend((k, k + r) for k in range(lo + r, hi - r, step))
        else:
            pairs.append((lo, lo + r))

    def sort(lo, hi):
        if hi - lo >= 1:
            mid = lo + (hi - lo) // 2
            sort(lo, mid)
            sort(mid + 1, hi)
            merge(lo, hi, 1)

    sort(0, n - 1)
    return pairs


_SORT16 = _sort_network(PEER_TOPK)


def _top16_desc(slabs):
    v = list(slabs)
    for a, b in _SORT16:
        hi, lo = jnp.maximum(v[a], v[b]), jnp.minimum(v[a], v[b])
        v[a], v[b] = hi, lo
    for shift in (4, 2, 1):
        v = [jnp.maximum(v[k], pltpu.roll(v[PEER_TOPK - 1 - k], shift, 0))
             for k in range(PEER_TOPK)]
        step = PEER_TOPK // 2
        while step >= 1:
            for k in range(PEER_TOPK):
                if k & step == 0:
                    hi, lo = jnp.maximum(v[k], v[k + step]), jnp.minimum(v[k], v[k + step])
                    v[k], v[k + step] = hi, lo
            step //= 2
    return v


def _peer_route(s0, s1):
    n = s0.shape[1]
    top_a = _top16_desc([s0[k * SUBLANES:(k + 1) * SUBLANES] for k in range(N_KEYS // SUBLANES)])
    top_b = _top16_desc([s1[k * SUBLANES:(k + 1) * SUBLANES] for k in range(N_KEYS // SUBLANES)])
    a0, b0 = top_a[0], top_b[0]
    ea = [jnp.exp(t - a0) for t in top_a]
    eb = [jnp.exp(t - b0) for t in top_b]
    sub = lax.broadcasted_iota(jnp.int32, (SUBLANES, n), 0)
    first4 = sub < 4

    def candidates(ea_list):
        col = ea_list[0]
        for s in range(1, 4):
            col = jnp.where(sub == s, ea_list[s], col)
        for s in range(4, 8):
            col = jnp.where(sub == s, eb[s - 4], col)
        out = []
        for v in range(PEER_TOPK):
            c = col * jnp.where(first4, eb[v], ea_list[v])
            if v < 4:
                c = jnp.where(first4, c, -1.0)
            out.append(c)
        return out

    cand = candidates(ea)
    best = _top16_desc(cand)
    z = best[0]
    for t in best[1:]:
        z = z + t
    inv_z = 1.0 / z
    theta = best[PEER_TOPK - 1]
    ean = [t * inv_z for t in ea]
    cand_n = candidates(ean)
    thr = None
    for c, cn in zip(cand, cand_n):
        t = jnp.where(c >= theta, cn, jnp.inf)
        thr = t if thr is None else jnp.minimum(thr, t)
    for shift in (4, 2, 1):
        thr = jnp.minimum(thr, pltpu.roll(thr, shift, 0))
    e1n = jnp.exp(s0 - a0[0:1]) * inv_z[0:1]
    e2 = jnp.exp(s1 - b0[0:1])
    return e1n, e2, thr[0:1]


def _peer_kernel(x1_ref, wpq_ref, keys_ref, u_ref, vt_ref, g_ref, b_ref, o_ref,
                 xb_ref, e1_ref, e2_ref, thr_ref, acc_ref, wa_ref, *, alpha, ec, lane_chunk):
    j = pl.program_id(1)
    tm = x1_ref.shape[0]
    n1 = ec // N_KEYS
    assert n1 == SUBLANES

    @pl.when(j == 0)
    def _():
        xb = x1_ref[...].astype(BF16)
        xb_ref[...] = xb
        acc_ref[...] = jnp.zeros(acc_ref.shape, F32)
        for h in range(PEER_HEADS):
            s = []
            for half in range(2):
                r = 2 * h + half
                qt = lax.dot_general(wpq_ref[r * N_KEYS:(r + 1) * N_KEYS, :], xb, _NT,
                                     preferred_element_type=F32)
                s.append(jnp.dot(keys_ref[r], qt.astype(BF16), preferred_element_type=F32))
            for lt in range(0, tm, LANES):
                cols = slice(lt, lt + LANES)
                e1n, e2, thr = _peer_route(s[0][:, cols], s[1][:, cols])
                e1_ref[h, :, cols] = e1n
                e2_ref[h, :, cols] = e2
                thr_ref[h:h + 1, cols] = thr

    ht = lax.dot_general(u_ref[...], xb_ref[...], _NT, preferred_element_type=F32)
    base = pl.multiple_of(j * n1, SUBLANES)
    for ii in range(n1):
        rows = slice(ii * N_KEYS, (ii + 1) * N_KEYS)
        for lc in range(0, tm, lane_chunk):
            cols = slice(lc, lc + lane_chunk)
            w = None
            for h in range(PEER_HEADS):
                e1t = e1_ref[h, pl.ds(base, SUBLANES), cols]
                p = e1t[ii:ii + 1, :] * e2_ref[h, :, cols]
                t = jnp.where(p >= thr_ref[h:h + 1, cols], p, 0.0)
                w = t if w is None else w + t
            hblk = ht[rows, cols]
            act = 0.5 * hblk * (1.0 + lax.erf(hblk * (2.0 ** -0.5)))
            wa_ref[rows, cols] = (w * act).astype(BF16)
    acc_ref[...] += jnp.dot(vt_ref[...], wa_ref[...], preferred_element_type=F32)

    @pl.when(j == pl.num_programs(1) - 1)
    def _():
        z = alpha * x1_ref[...] + acc_ref[...].T
        o_ref[...] = _layer_norm(z, g_ref[...], b_ref[...])


def _peer(x1, wpq_t, keys, u, vt, g, b, alpha, tm, ec, lane_chunk):
    t, d = x1.shape
    kern = functools.partial(_peer_kernel, alpha=alpha, ec=ec, lane_chunk=lane_chunk)
    n_chunks = N_EXPERTS // ec
    return pl.pallas_call(
        kern,
        out_shape=jax.ShapeDtypeStruct((t, d), F32),
        grid=(t // tm, n_chunks),
        in_specs=[
            pl.BlockSpec((tm, d), lambda i, j: (i, 0)),
            pl.BlockSpec(wpq_t.shape, lambda i, j: (0, 0)),
            pl.BlockSpec(keys.shape, lambda i, j: (0, 0, 0)),
            pl.BlockSpec((ec, d), lambda i, j: (j, 0)),
            pl.BlockSpec((d, ec), lambda i, j: (0, j)),
            pl.BlockSpec((1, d), lambda i, j: (0, 0)),
            pl.BlockSpec((1, d), lambda i, j: (0, 0)),
        ],
        out_specs=pl.BlockSpec((tm, d), lambda i, j: (i, 0)),
        scratch_shapes=[
            pltpu.VMEM((tm, d), BF16),
            pltpu.VMEM((PEER_HEADS, N_KEYS, tm), F32),
            pltpu.VMEM((PEER_HEADS, N_KEYS, tm), F32),
            pltpu.VMEM((PEER_HEADS, tm), F32),
            pltpu.VMEM((d, tm), F32),
            pltpu.VMEM((ec, tm), BF16),
        ],
        compiler_params=pltpu.CompilerParams(
            dimension_semantics=("parallel", "arbitrary"), vmem_limit_bytes=VMEM_LIMIT),
        name="peer",
    )(x1, wpq_t, keys, u, vt, g, b)


def _prep_weights(depth, l, w_in, w_mem_kv, lam_q1, lam_k1, lam_q2, lam_k2, subln_g, sink, w_gate,
                  b_gate, w_pa, w_pb, w_pc, w_o, ln1_g, ln1_b, w_pq, sub_keys, peer_u, peer_v,
                  ln2_g, ln2_b):
    w = w_in[l]
    qb_perm = np.concatenate([np.arange(64) + 64 * hq for g in range(4) for hq in (g, g + 4)])
    qa = w[:, 0:512] * (0.125 * math.log2(math.e))
    ka, va = w[:, 512:1024], w[:, 1024:1536]
    qb = w[:, 1536:2048][:, qb_perm] * 0.125
    kb, vb, qc = w[:, 2048:2176], w[:, 2176:2304], w[:, 2304:2816]
    w_proj = jnp.concatenate([qa, ka, va, qb, qc, kb, vb], axis=1).astype(BF16)
    lamp = jnp.stack([lam_q1[l], lam_k1[l], lam_q2[l], lam_k2[l]]).astype(F32)
    row = lambda a: a.astype(F32).reshape(1, -1)
    return dict(
        w_proj=w_proj,
        w_mem_kv=w_mem_kv[l].astype(BF16),
        lamp=lamp,
        subln_g=row(subln_g[l]),
        sink=sink[l].astype(F32),
        w_gate=w_gate[l].astype(BF16),
        b_gate=row(b_gate[l]),
        w_pa=w_pa[l].astype(BF16),
        w_pb=w_pb[l][qb_perm].astype(BF16),
        w_pc=w_pc[l].astype(BF16),
        w_o=w_o[l].astype(BF16),
        ln1_g=row(ln1_g[l]), ln1_b=row(ln1_b[l]),
        w_pq_t=w_pq[l].T.astype(BF16),
        keys=sub_keys[l].reshape(2 * PEER_HEADS, N_KEYS, N_KEYS).astype(BF16),
        peer_u=peer_u[l].astype(BF16),
        peer_vt=peer_v[l].T.astype(BF16),
        ln2_g=row(ln2_g[l]), ln2_b=row(ln2_b[l]),
        lam_init=0.8 - 0.6 * math.exp(-0.3 * l),
        alpha=(2.0 * depth) ** 0.25,
    )


def _tile(n, pref):
    t = min(n, pref)
    assert n % t == 0, (n, t)
    return t


def _encoder_layer(x, mem, p):
    b, s, d = x.shape
    t = b * s
    x2d = x.reshape(t, d)
    proj = _proj(x2d, p["w_proj"], _tile(t, 512)).reshape(b, s, PROJ_COLS)
    slopes_a = jnp.asarray(2.0 ** (-8.0 * np.arange(1, DA_HEADS + 1) / DA_HEADS), F32)
    slopes_b = jnp.asarray(2.0 ** (-8.0 * np.arange(1, WA_HEADS + 1) / WA_HEADS), F32)
    oa = _attn_a(proj, slopes_a, p["lamp"], p["subln_g"], p["lam_init"], 256, 4, 512)
    ob = _attn_b(proj, slopes_b, p["sink"], _tile(s, 256))
    oc = _attn_c(proj, mem, p["w_mem_kv"], _tile(s, 512))
    x1 = _merge(x2d, oa.reshape(t, BRANCH_W), ob.reshape(t, BRANCH_W), oc.reshape(t, BRANCH_W),
                p["w_gate"], p["b_gate"], p["w_pa"], p["w_pb"], p["w_pc"], p["w_o"],
                p["ln1_g"], p["ln1_b"], p["alpha"], _tile(t, 256))
    y = _peer(x1, p["w_pq_t"], p["keys"], p["peer_u"], p["peer_vt"], p["ln2_g"], p["ln2_b"],
              p["alpha"], _tile(t, 512), 1024, 256)
    return y.reshape(b, s, d)


def kernel(x_prompt, x_sample, mem_prompt, mem_sample, w_in, w_mem_kv, lam_q1, lam_k1, lam_q2,
           lam_k2, subln_g, sink, w_gate, b_gate, w_pa, w_pb, w_pc, w_o, ln1_g, ln1_b, w_pq,
           sub_keys, peer_u, peer_v, ln2_g, ln2_b):
    depth = w_in.shape[0]
    y_prompt, y_sample = x_prompt, x_sample
    for l in range(depth):
        p = _prep_weights(depth, l, w_in, w_mem_kv, lam_q1, lam_k1, lam_q2, lam_k2, subln_g, sink,
                          w_gate, b_gate, w_pa, w_pb, w_pc, w_o, ln1_g, ln1_b, w_pq, sub_keys,
                          peer_u, peer_v, ln2_g, ln2_b)
        y_prompt = _encoder_layer(y_prompt, mem_prompt, p)
        y_sample = _encoder_layer(y_sample, mem_sample, p)
    return (y_prompt, y_sample)
```

```python
import functools
import math

import jax
import jax.numpy as jnp
import numpy as np
from jax import lax
from jax.experimental import pallas as pl
from jax.experimental.pallas import tpu as pltpu

F32 = jnp.float32
BF16 = jnp.bfloat16

D_MODEL = 1024
N_MEM = 256
BLOCK = 128
DA_HEADS = 4
WA_HEADS = 8
MEM_HEADS = 4
MEM_DH = 128
BRANCH_W = 512
PEER_HEADS = 8
N_KEYS = 128
N_EXPERTS = N_KEYS * N_KEYS
PEER_TOPK = 16
LN_EPS = 1e-5
NEG = -1e30
LANES = 128
SUBLANES = 8

COL_QA, COL_KA, COL_VA, COL_QB, COL_QC, COL_KB, COL_VB = 0, 512, 1024, 1536, 2048, 2560, 2688
PROJ_COLS = 2816

VMEM_LIMIT = 56 * 1024 * 1024

_NT = (((1,), (1,)), ((), ()))


def _layer_norm(z, g, b):
    mu = jnp.mean(z, axis=-1, keepdims=True)
    zc = z - mu
    var = jnp.mean(zc * zc, axis=-1, keepdims=True)
    return zc * lax.rsqrt(var + LN_EPS) * g + b


def _proj_kernel(x_ref, w_ref, o_ref, *, n_chunk):
    xb = x_ref[...].astype(BF16)
    for c in range(0, o_ref.shape[-1], n_chunk):
        o_ref[:, c:c + n_chunk] = jnp.dot(
            xb, w_ref[:, c:c + n_chunk], preferred_element_type=F32).astype(BF16)


def _proj(x2d, w, tm):
    t, d = x2d.shape
    n = w.shape[1]
    return pl.pallas_call(
        functools.partial(_proj_kernel, n_chunk=256),
        out_shape=jax.ShapeDtypeStruct((t, n), BF16),
        grid=(t // tm,),
        in_specs=[pl.BlockSpec((tm, d), lambda i: (i, 0)),
                  pl.BlockSpec((d, n), lambda i: (0, 0))],
        out_specs=pl.BlockSpec((tm, n), lambda i: (i, 0)),
        compiler_params=pltpu.CompilerParams(
            dimension_semantics=("parallel",), vmem_limit_bytes=VMEM_LIMIT),
        name="proj",
    )(x2d, w)


def _attn_a_kernel(lo_ref, hi_ref, slopes_ref, lamp_ref, g_ref, q_ref, k_ref, v_ref, o_ref,
                   gm_ref, m_ref, l_ref, acc_ref, *, tq, n_streams, tk, lam_init):
    bb = pl.program_id(0)
    h = pl.program_id(1)
    i = pl.program_id(2)
    tg = n_streams * tq
    mid = tg // tk
    flat = (bb * pl.num_programs(1) + h) * pl.num_programs(2) + i
    lo = lo_ref[flat]
    hi = hi_ref[flat]
    slope2 = slopes_ref[h] * math.log2(math.e)
    reps = tk // LANES

    gm_ref[...] = (lax.broadcasted_iota(jnp.int32, (tq, tk), 0)
                   - lax.broadcasted_iota(jnp.int32, (tq, tk), 1)).astype(F32) * slope2
    m_ref[...] = jnp.full(m_ref.shape, -jnp.inf, F32)
    l_ref[...] = jnp.zeros(l_ref.shape, F32)
    acc_ref[...] = jnp.zeros(acc_ref.shape, F32)

    lane = lax.broadcasted_iota(jnp.int32, (tq, LANES), 1)
    q2 = []
    for st in range(n_streams):
        q = q_ref[st * tq:(st + 1) * tq, :]
        zero = jnp.zeros_like(q)
        q2.append(jnp.concatenate([jnp.where(lane < 64, q, zero),
                                   jnp.where(lane >= 64, q, zero)], axis=0))

    def step(st, kt, vt, bias, shift):
        s = lax.dot_general(q2[st], kt, _NT, preferred_element_type=F32)
        x = s + jnp.concatenate([bias, bias], axis=0)
        m_prev = m_ref[st]
        m_next = jnp.maximum(m_prev, jnp.max(x, axis=1, keepdims=True) - shift)
        alpha = jnp.exp2(m_prev - m_next)
        sub = m_next + shift
        p = jnp.exp2(x - jnp.concatenate([sub] * reps, axis=1))
        psum = p[:, 0:LANES]
        for c in range(1, reps):
            psum = psum + p[:, c * LANES:(c + 1) * LANES]
        l_ref[st] = alpha * l_ref[st] + psum
        acc_ref[st] = alpha * acc_ref[st] + jnp.dot(
            p.astype(BF16), vt, preferred_element_type=F32)
        m_ref[st] = m_next

    def tiles(j):
        ks = pl.multiple_of(j * tk, tk)
        return k_ref[pl.ds(ks, tk), :], v_ref[pl.ds(ks, tk), :]

    def left(j, carry):
        kt, vt = tiles(j)
        base = lax.convert_element_type(i * tg - j * tk, F32)
        for st in range(n_streams):
            step(st, kt, vt, -gm_ref[...], slope2 * (base + st * tq))
        return carry

    def right(j, carry):
        kt, vt = tiles(j)
        base = lax.convert_element_type(j * tk - i * tg, F32)
        for st in range(n_streams):
            step(st, kt, vt, gm_ref[...], slope2 * (base - st * tq))
        return carry

    lax.fori_loop(lo, i * mid, left, 0)
    for mj in range(mid):
        kt, vt = tiles(i * mid + mj)
        for st in range(n_streams):
            off = st * tq - mj * tk
            if off - (tk - 1) >= 0:
                step(st, kt, vt, -gm_ref[...], slope2 * off)
            elif off + (tq - 1) <= 0:
                step(st, kt, vt, gm_ref[...], slope2 * (-off))
            else:
                step(st, kt, vt, -jnp.abs(gm_ref[...] + slope2 * off), 0.0)
    lax.fori_loop((i + 1) * mid, hi, right, 0)

    lamp = lamp_ref[...]
    lam = (jnp.exp(jnp.sum(lamp[0:1] * lamp[1:2], axis=1, keepdims=True))
           - jnp.exp(jnp.sum(lamp[2:3] * lamp[3:4], axis=1, keepdims=True)) + lam_init)
    for st in range(n_streams):
        o = acc_ref[st] / jnp.sum(l_ref[st], axis=1, keepdims=True)
        o = o[:tq] - lam * o[tq:]
        ms = jnp.mean(o * o, axis=-1, keepdims=True)
        y = o * lax.rsqrt(ms + LN_EPS) * g_ref[...] * (1.0 - lam_init)
        o_ref[st * tq:(st + 1) * tq, :] = y.astype(o_ref.dtype)


UNDERFLOW_LOG2 = 151.0


def _attn_a_bounds(proj, slopes, tg, tk):
    b, s, _ = proj.shape
    ni, nk, mid = s // tg, s // tk, tg // tk
    q = proj[:, :, COL_QA:COL_QA + 512].astype(F32).reshape(b, s, DA_HEADS, 2, 64)
    k = proj[:, :, COL_KA:COL_KA + 512].astype(F32).reshape(b, s, DA_HEADS, 2, 64)
    qn = jnp.sqrt(jnp.sum(q * q, axis=-1)).reshape(b, ni, tg, DA_HEADS, 2)
    own = jnp.sum(q * k, axis=-1).reshape(b, ni, tg, DA_HEADS, 2)
    kmax = jnp.max(jnp.sqrt(jnp.sum(k * k, axis=-1)), axis=1)
    upper = jnp.max(qn, axis=2) * kmax[:, None] * 1.001
    slack = jnp.max(upper - jnp.min(own, axis=2), axis=-1) + UNDERFLOW_LOG2
    reach = slack / (slopes * math.log2(math.e))
    reach = jnp.minimum(reach, 4.0 * s)
    i0 = (jnp.arange(ni, dtype=F32) * tg)[None, :, None]
    lo = jnp.ceil((i0 + 1.0 - reach) / tk - 1.0)
    lo = jnp.clip(lo, 0, jnp.arange(ni, dtype=F32)[None, :, None] * mid)
    hi = jnp.floor((reach + i0 + tg - 1.0) / tk) + 1.0
    hi = jnp.clip(hi, (jnp.arange(ni, dtype=F32)[None, :, None] + 1.0) * mid, nk)
    flat = lambda a: a.astype(jnp.int32).transpose(0, 2, 1).reshape(-1)
    return flat(lo), flat(hi)


def _attn_a(proj, slopes, lamp, subln_g, lam_init, tq, n_streams, tk):
    b, s, _ = proj.shape
    tg = tq * n_streams
    assert tg % tk == 0 and s % tg == 0
    lo, hi = _attn_a_bounds(proj, slopes, tg, tk)
    kern = functools.partial(_attn_a_kernel, tq=tq, n_streams=n_streams, tk=tk, lam_init=lam_init)
    grid_spec = pltpu.PrefetchScalarGridSpec(
        num_scalar_prefetch=2,
        grid=(b, DA_HEADS, s // tg),
        in_specs=[
            pl.BlockSpec(memory_space=pltpu.SMEM),
            pl.BlockSpec((4, 64), lambda bb, h, i, lo, hi: (0, 0)),
            pl.BlockSpec((1, LANES), lambda bb, h, i, lo, hi: (0, 0)),
            pl.BlockSpec((None, tg, LANES), lambda bb, h, i, lo, hi: (bb, i, COL_QA // LANES + h)),
            pl.BlockSpec((None, s, LANES), lambda bb, h, i, lo, hi: (bb, 0, COL_KA // LANES + h)),
            pl.BlockSpec((None, s, LANES), lambda bb, h, i, lo, hi: (bb, 0, COL_VA // LANES + h)),
        ],
        out_specs=pl.BlockSpec((None, tg, LANES), lambda bb, h, i, lo, hi: (bb, i, h)),
        scratch_shapes=[pltpu.VMEM((tq, tk), F32),
                        pltpu.VMEM((n_streams, 2 * tq, LANES), F32),
                        pltpu.VMEM((n_streams, 2 * tq, LANES), F32),
                        pltpu.VMEM((n_streams, 2 * tq, LANES), F32)],
    )
    return pl.pallas_call(
        kern,
        out_shape=jax.ShapeDtypeStruct((b, s, BRANCH_W), BF16),
        grid_spec=grid_spec,
        compiler_params=pltpu.CompilerParams(
            dimension_semantics=("parallel", "parallel", "arbitrary"),
            vmem_limit_bytes=VMEM_LIMIT),
        name="attn_a",
    )(lo, hi, slopes, lamp, subln_g, proj, proj, proj)


def _attn_b_kernel(slopes_ref, sink_ref, q_ref, kp_ref, kc_ref, kn_ref, vp_ref, vc_ref, vn_ref,
                   o_ref, *, tq, seq):
    i = pl.program_id(1)
    kfull = jnp.concatenate([kp_ref[...], kc_ref[...], kn_ref[...]], axis=0)
    vfull = jnp.concatenate([vp_ref[...], vc_ref[...], vn_ref[...]], axis=0)
    band = 3 * BLOCK
    r = lax.broadcasted_iota(jnp.int32, (BLOCK, band), 0)
    c = lax.broadcasted_iota(jnp.int32, (BLOCK, band), 1)
    rel_i = jnp.abs(r + BLOCK - c)
    rel = rel_i.astype(F32)
    lane = lax.broadcasted_iota(jnp.int32, (BLOCK, LANES), 1)
    lo_half = lane < 64

    for sub in range(tq // BLOCK):
        q_start = i * tq + sub * BLOCK
        valid = ((rel_i <= BLOCK) & (c >= BLOCK - q_start) & (c < seq + BLOCK - q_start))
        kband = kfull[sub * BLOCK: sub * BLOCK + band]
        vband = vfull[sub * BLOCK: sub * BLOCK + band]
        qblk = q_ref[sub * BLOCK:(sub + 1) * BLOCK, :]
        parts = []
        for g in range(4):
            qg = qblk[:, g * LANES:(g + 1) * LANES]
            zero = jnp.zeros_like(qg)
            parts.append(jnp.where(lo_half, qg, zero))
            parts.append(jnp.where(lo_half, zero, qg))
        qs = jnp.concatenate(parts, axis=0)
        s_all = lax.dot_general(qs, kband, _NT, preferred_element_type=F32)
        ps, invs = [], []
        for n in range(8):
            hq = (n // 2) + 4 * (n % 2)
            s = s_all[n * BLOCK:(n + 1) * BLOCK]
            s = jnp.where(valid, s - slopes_ref[hq] * rel, NEG)
            sk = sink_ref[hq]
            m = jnp.maximum(jnp.max(s, axis=1, keepdims=True), sk)
            e = jnp.exp(s - m)
            den = jnp.sum(e, axis=1, keepdims=True) + jnp.exp(sk - m)
            ps.append(e.astype(BF16))
            invs.append(1.0 / den)
        p_all = jnp.concatenate(ps, axis=0)
        o_all = jnp.dot(p_all, vband, preferred_element_type=F32)
        for g in range(4):
            o_lo = o_all[(2 * g) * BLOCK:(2 * g + 1) * BLOCK] * invs[2 * g]
            o_hi = o_all[(2 * g + 1) * BLOCK:(2 * g + 2) * BLOCK] * invs[2 * g + 1]
            o_ref[sub * BLOCK:(sub + 1) * BLOCK, g * LANES:(g + 1) * LANES] = jnp.where(
                lo_half, o_lo, o_hi).astype(o_ref.dtype)


def _attn_b(proj, slopes, sink, tq):
    b, s, _ = proj.shape
    nb = s // BLOCK
    r = tq // BLOCK
    kcol, vcol = COL_KB // LANES, COL_VB // LANES

    def prev_map(col):
        return lambda bb, i: (bb, jnp.maximum(i * r - 1, 0), col)

    def cur_map(col):
        return lambda bb, i: (bb, i, col)

    def next_map(col):
        return lambda bb, i: (bb, jnp.minimum(i * r + r, nb - 1), col)

    return pl.pallas_call(
        functools.partial(_attn_b_kernel, tq=tq, seq=s),
        out_shape=jax.ShapeDtypeStruct((b, s, BRANCH_W), BF16),
        grid=(b, s // tq),
        in_specs=[
            pl.BlockSpec(memory_space=pltpu.SMEM),
            pl.BlockSpec(memory_space=pltpu.SMEM),
            pl.BlockSpec((None, tq, BRANCH_W), lambda bb, i: (bb, i, COL_QB // BRANCH_W)),
            pl.BlockSpec((None, BLOCK, LANES), prev_map(kcol)),
            pl.BlockSpec((None, tq, LANES), cur_map(kcol)),
            pl.BlockSpec((None, BLOCK, LANES), next_map(kcol)),
            pl.BlockSpec((None, BLOCK, LANES), prev_map(vcol)),
            pl.BlockSpec((None, tq, LANES), cur_map(vcol)),
            pl.BlockSpec((None, BLOCK, LANES), next_map(vcol)),
        ],
        out_specs=pl.BlockSpec((None, tq, BRANCH_W), lambda bb, i: (bb, i, 0)),
        compiler_params=pltpu.CompilerParams(
            dimension_semantics=("parallel", "parallel"), vmem_limit_bytes=VMEM_LIMIT),
        name="attn_b",
    )(slopes, sink, proj, proj, proj, proj, proj, proj, proj)


def _attn_c_kernel(q_ref, mem_ref, wkv_ref, o_ref, mk_ref, mv_ref):
    @pl.when(pl.program_id(1) == 0)
    def _():
        kv = jnp.dot(mem_ref[...].astype(BF16), wkv_ref[...], preferred_element_type=F32)
        mk_ref[...] = kv[:, :BRANCH_W].astype(BF16)
        mv_ref[...] = kv[:, BRANCH_W:].astype(BF16)

    scale = MEM_DH ** -0.5
    for h in range(MEM_HEADS):
        cols = slice(h * MEM_DH, (h + 1) * MEM_DH)
        s = lax.dot_general(q_ref[:, cols], mk_ref[:, cols], _NT,
                            preferred_element_type=F32) * scale
        m = jnp.max(s, axis=1, keepdims=True)
        e = jnp.exp(s - m)
        inv = 1.0 / jnp.sum(e, axis=1, keepdims=True)
        o = jnp.dot(e.astype(BF16), mv_ref[:, cols], preferred_element_type=F32)
        o_ref[:, cols] = (o * inv).astype(o_ref.dtype)


def _attn_c(proj, mem, wkv, tq):
    b, s, _ = proj.shape
    return pl.pallas_call(
        _attn_c_kernel,
        out_shape=jax.ShapeDtypeStruct((b, s, BRANCH_W), BF16),
        grid=(b, s // tq),
        in_specs=[
            pl.BlockSpec((None, tq, BRANCH_W), lambda bb, i: (bb, i, COL_QC // BRANCH_W)),
            pl.BlockSpec((None, N_MEM, D_MODEL), lambda bb, i: (bb, 0, 0)),
            pl.BlockSpec((D_MODEL, 2 * BRANCH_W), lambda bb, i: (0, 0)),
        ],
        out_specs=pl.BlockSpec((None, tq, BRANCH_W), lambda bb, i: (bb, i, 0)),
        scratch_shapes=[pltpu.VMEM((N_MEM, BRANCH_W), BF16), pltpu.VMEM((N_MEM, BRANCH_W), BF16)],
        compiler_params=pltpu.CompilerParams(
            dimension_semantics=("parallel", "arbitrary"), vmem_limit_bytes=VMEM_LIMIT),
        name="attn_c",
    )(proj, mem, wkv)


def _merge_kernel(x_ref, oa_ref, ob_ref, oc_ref, wg_ref, bg_ref, wpa_ref, wpb_ref, wpc_ref,
                  wo_ref, g_ref, b_ref, o_ref, *, alpha):
    x = x_ref[...]
    xb = x.astype(BF16)
    merged = None
    for n, (br_ref, wp_ref) in enumerate(((oa_ref, wpa_ref), (ob_ref, wpb_ref), (oc_ref, wpc_ref))):
        cols = slice(n * D_MODEL, (n + 1) * D_MODEL)
        gate = jax.nn.sigmoid(
            jnp.dot(xb, wg_ref[:, cols], preferred_element_type=F32) + bg_ref[:, cols])
        term = gate * jnp.dot(br_ref[...], wp_ref[...], preferred_element_type=F32)
        merged = term if merged is None else merged + term
    y = jnp.dot(merged.astype(BF16), wo_ref[...], preferred_element_type=F32)
    o_ref[...] = _layer_norm(alpha * x + y, g_ref[...], b_ref[...])


def _merge(x2d, oa, ob, oc, wg, bg, wpa, wpb, wpc, wo, g, b, alpha, tm):
    t, d = x2d.shape
    const = lambda i: (0, 0)
    row = lambda i: (i, 0)
    return pl.pallas_call(
        functools.partial(_merge_kernel, alpha=alpha),
        out_shape=jax.ShapeDtypeStruct((t, d), F32),
        grid=(t // tm,),
        in_specs=[
            pl.BlockSpec((tm, d), row),
            pl.BlockSpec((tm, BRANCH_W), row),
            pl.BlockSpec((tm, BRANCH_W), row),
            pl.BlockSpec((tm, BRANCH_W), row),
            pl.BlockSpec((d, 3 * d), const),
            pl.BlockSpec((1, 3 * d), const),
            pl.BlockSpec((BRANCH_W, d), const),
            pl.BlockSpec((BRANCH_W, d), const),
            pl.BlockSpec((BRANCH_W, d), const),
            pl.BlockSpec((d, d), const),
            pl.BlockSpec((1, d), const),
            pl.BlockSpec((1, d), const),
        ],
        out_specs=pl.BlockSpec((tm, d), row),
        compiler_params=pltpu.CompilerParams(
            dimension_semantics=("parallel",), vmem_limit_bytes=VMEM_LIMIT),
        name="merge",
    )(x2d, oa, ob, oc, wg, bg, wpa, wpb, wpc, wo, g, b)


def _sort_network(n):
    pairs = []

    def merge(lo, hi, r):
        step = r * 2
        if step < hi - lo:
            merge(lo, hi, step)
            merge(lo + r, hi, step)
            pairs.extend((k, k + r) for k in range(lo + r, hi - r, step))
        else:
            pairs.append((lo, lo + r))

    def sort(lo, hi):
        if hi - lo >= 1:
            mid = lo + (hi - lo) // 2
            sort(lo, mid)
            sort(mid + 1, hi)
            merge(lo, hi, 1)

    sort(0, n - 1)
    return pairs


_SORT16 = _sort_network(PEER_TOPK)


def _top16_desc(slabs):
    v = list(slabs)
    for a, b in _SORT16:
        hi, lo = jnp.maximum(v[a], v[b]), jnp.minimum(v[a], v[b])
        v[a], v[b] = hi, lo
    for shift in (4, 2, 1):
        v = [jnp.maximum(v[k], pltpu.roll(v[PEER_TOPK - 1 - k], shift, 0))
             for k in range(PEER_TOPK)]
        step = PEER_TOPK // 2
        while step >= 1:
            for k in range(PEER_TOPK):
                if k & step == 0:
                    hi, lo = jnp.maximum(v[k], v[k + step]), jnp.minimum(v[k], v[k + step])
                    v[k], v[k + step] = hi, lo
            step //= 2
    return v


def _peer_route(s0, s1):
    n = s0.shape[1]
    top_a = _top16_desc([s0[k * SUBLANES:(k + 1) * SUBLANES] for k in range(N_KEYS // SUBLANES)])
    top_b = _top16_desc([s1[k * SUBLANES:(k + 1) * SUBLANES] for k in range(N_KEYS // SUBLANES)])
    a0, b0 = top_a[0], top_b[0]
    ea = [jnp.exp(t - a0) for t in top_a]
    eb = [jnp.exp(t - b0) for t in top_b]
    sub = lax.broadcasted_iota(jnp.int32, (SUBLANES, n), 0)
    first4 = sub < 4

    def candidates(ea_list):
        col = ea_list[0]
        for s in range(1, 4):
            col = jnp.where(sub == s, ea_list[s], col)
        for s in range(4, 8):
            col = jnp.where(sub == s, eb[s - 4], col)
        out = []
        for v in range(PEER_TOPK):
            c = col * jnp.where(first4, eb[v], ea_list[v])
            if v < 4:
                c = jnp.where(first4, c, -1.0)
            out.append(c)
        return out

    cand = candidates(ea)
    best = _top16_desc(cand)
    z = best[0]
    for t in best[1:]:
        z = z + t
    inv_z = 1.0 / z
    theta = best[PEER_TOPK - 1]
    ean = [t * inv_z for t in ea]
    cand_n = candidates(ean)
    thr = None
    for c, cn in zip(cand, cand_n):
        t = jnp.where(c >= theta, cn, jnp.inf)
        thr = t if thr is None else jnp.minimum(thr, t)
    for shift in (4, 2, 1):
        thr = jnp.minimum(thr, pltpu.roll(thr, shift, 0))
    e1n = jnp.exp(s0 - a0[0:1]) * inv_z[0:1]
    e2 = jnp.exp(s1 - b0[0:1])
    return e1n, e2, thr[0:1]


def _peer_kernel(x1_ref, wpq_ref, keys_ref, u_ref, vt_ref, g_ref, b_ref, o_ref,
                 xb_ref, e1_ref, e2_ref, thr_ref, acc_ref, wa_ref, *, alpha, ec, lane_chunk):
    j = pl.program_id(1)
    tm = x1_ref.shape[0]
    n1 = ec // N_KEYS
    assert n1 == SUBLANES

    @pl.when(j == 0)
    def _():
        xb = x1_ref[...].astype(BF16)
        xb_ref[...] = xb
        acc_ref[...] = jnp.zeros(acc_ref.shape, F32)
        for h in range(PEER_HEADS):
            s = []
            for half in range(2):
                r = 2 * h + half
                qt = lax.dot_general(wpq_ref[r * N_KEYS:(r + 1) * N_KEYS, :], xb, _NT,
                                     preferred_element_type=F32)
                s.append(jnp.dot(keys_ref[r], qt.astype(BF16), preferred_element_type=F32))
            for lt in range(0, tm, 2 * LANES):
                cols = slice(lt, lt + 2 * LANES)
                e1n, e2, thr = _peer_route(s[0][:, cols], s[1][:, cols])
                e1_ref[h, :, cols] = e1n
                e2_ref[h, :, cols] = e2
                thr_ref[h:h + 1, cols] = thr

    ht = lax.dot_general(u_ref[...], xb_ref[...], _NT, preferred_element_type=F32)
    base = pl.multiple_of(j * n1, SUBLANES)
    for ii in range(n1):
        rows = slice(ii * N_KEYS, (ii + 1) * N_KEYS)
        for lc in range(0, tm, lane_chunk):
            cols = slice(lc, lc + lane_chunk)
            w = None
            for h in range(PEER_HEADS):
                e1t = e1_ref[h, pl.ds(base, SUBLANES), cols]
                p = e1t[ii:ii + 1, :] * e2_ref[h, :, cols]
                t = jnp.where(p >= thr_ref[h:h + 1, cols], p, 0.0)
                w = t if w is None else w + t
            hblk = ht[rows, cols]
            act = 0.5 * hblk * (1.0 + lax.erf(hblk * (2.0 ** -0.5)))
            wa_ref[rows, cols] = (w * act).astype(BF16)
    acc_ref[...] += jnp.dot(vt_ref[...], wa_ref[...], preferred_element_type=F32)

    @pl.when(j == pl.num_programs(1) - 1)
    def _():
        z = alpha * x1_ref[...] + acc_ref[...].T
        o_ref[...] = _layer_norm(z, g_ref[...], b_ref[...])


def _peer(x1, wpq_t, keys, u, vt, g, b, alpha, tm, ec, lane_chunk):
    t, d = x1.shape
    kern = functools.partial(_peer_kernel, alpha=alpha, ec=ec, lane_chunk=lane_chunk)
    n_chunks = N_EXPERTS // ec
    return pl.pallas_call(
        kern,
        out_shape=jax.ShapeDtypeStruct((t, d), F32),
        grid=(t // tm, n_chunks),
        in_specs=[
            pl.BlockSpec((tm, d), lambda i, j: (i, 0)),
            pl.BlockSpec(wpq_t.shape, lambda i, j: (0, 0)),
            pl.BlockSpec(keys.shape, lambda i, j: (0, 0, 0)),
            pl.BlockSpec((ec, d), lambda i, j: (j, 0)),
            pl.BlockSpec((d, ec), lambda i, j: (0, j)),
            pl.BlockSpec((1, d), lambda i, j: (0, 0)),
            pl.BlockSpec((1, d), lambda i, j: (0, 0)),
        ],
        out_specs=pl.BlockSpec((tm, d), lambda i, j: (i, 0)),
        scratch_shapes=[
            pltpu.VMEM((tm, d), BF16),
            pltpu.VMEM((PEER_HEADS, N_KEYS, tm), F32),
            pltpu.VMEM((PEER_HEADS, N_KEYS, tm), F32),
            pltpu.VMEM((PEER_HEADS, tm), F32),
            pltpu.VMEM((d, tm), F32),
            pltpu.VMEM((ec, tm), BF16),
        ],
        compiler_params=pltpu.CompilerParams(
            dimension_semantics=("parallel", "arbitrary"), vmem_limit_bytes=VMEM_LIMIT),
        name="peer",
    )(x1, wpq_t, keys, u, vt, g, b)


def _prep_weights(depth, l, w_in, w_mem_kv, lam_q1, lam_k1, lam_q2, lam_k2, subln_g, sink, w_gate,
                  b_gate, w_pa, w_pb, w_pc, w_o, ln1_g, ln1_b, w_pq, sub_keys, peer_u, peer_v,
                  ln2_g, ln2_b):
    w = w_in[l]
    qb_perm = np.concatenate([np.arange(64) + 64 * hq for g in range(4) for hq in (g, g + 4)])
    qa = w[:, 0:512] * (0.125 * math.log2(math.e))
    ka, va = w[:, 512:1024], w[:, 1024:1536]
    qb = w[:, 1536:2048][:, qb_perm] * 0.125
    kb, vb, qc = w[:, 2048:2176], w[:, 2176:2304], w[:, 2304:2816]
    w_proj = jnp.concatenate([qa, ka, va, qb, qc, kb, vb], axis=1).astype(BF16)
    lamp = jnp.stack([lam_q1[l], lam_k1[l], lam_q2[l], lam_k2[l]]).astype(F32)
    row = lambda a: a.astype(F32).reshape(1, -1)
    return dict(
        w_proj=w_proj,
        w_mem_kv=w_mem_kv[l].astype(BF16),
        lamp=lamp,
        subln_g=row(subln_g[l]),
        sink=sink[l].astype(F32),
        w_gate=w_gate[l].astype(BF16),
        b_gate=row(b_gate[l]),
        w_pa=w_pa[l].astype(BF16),
        w_pb=w_pb[l][qb_perm].astype(BF16),
        w_pc=w_pc[l].astype(BF16),
        w_o=w_o[l].astype(BF16),
        ln1_g=row(ln1_g[l]), ln1_b=row(ln1_b[l]),
        w_pq_t=w_pq[l].T.astype(BF16),
        keys=sub_keys[l].reshape(2 * PEER_HEADS, N_KEYS, N_KEYS).astype(BF16),
        peer_u=peer_u[l].astype(BF16),
        peer_vt=peer_v[l].T.astype(BF16),
        ln2_g=row(ln2_g[l]), ln2_b=row(ln2_b[l]),
        lam_init=0.8 - 0.6 * math.exp(-0.3 * l),
        alpha=(2.0 * depth) ** 0.25,
    )


def _tile(n, pref):
    t = min(n, pref)
    assert n % t == 0, (n, t)
    return t


def _encoder_layer(x, mem, p):
    b, s, d = x.shape
    t = b * s
    x2d = x.reshape(t, d)
    proj = _proj(x2d, p["w_proj"], _tile(t, 512)).reshape(b, s, PROJ_COLS)
    slopes_a = jnp.asarray(2.0 ** (-8.0 * np.arange(1, DA_HEADS + 1) / DA_HEADS), F32)
    slopes_b = jnp.asarray(2.0 ** (-8.0 * np.arange(1, WA_HEADS + 1) / WA_HEADS), F32)
    oa = _attn_a(proj, slopes_a, p["lamp"], p["subln_g"], p["lam_init"], 256, 4, 512)
    ob = _attn_b(proj, slopes_b, p["sink"], _tile(s, 256))
    oc = _attn_c(proj, mem, p["w_mem_kv"], _tile(s, 512))
    x1 = _merge(x2d, oa.reshape(t, BRANCH_W), ob.reshape(t, BRANCH_W), oc.reshape(t, BRANCH_W),
                p["w_gate"], p["b_gate"], p["w_pa"], p["w_pb"], p["w_pc"], p["w_o"],
                p["ln1_g"], p["ln1_b"], p["alpha"], _tile(t, 256))
    y = _peer(x1, p["w_pq_t"], p["keys"], p["peer_u"], p["peer_vt"], p["ln2_g"], p["ln2_b"],
              p["alpha"], _tile(t, 512), 1024, 256)
    return y.reshape(b, s, d)


def kernel(x_prompt, x_sample, mem_prompt, mem_sample, w_in, w_mem_kv, lam_q1, lam_k1, lam_q2,
           lam_k2, subln_g, sink, w_gate, b_gate, w_pa, w_pb, w_pc, w_o, ln1_g, ln1_b, w_pq,
           sub_keys, peer_u, peer_v, ln2_g, ln2_b):
    depth = w_in.shape[0]
    y_prompt, y_sample = x_prompt, x_sample
    for l in range(depth):
        p = _prep_weights(depth, l, w_in, w_mem_kv, lam_q1, lam_k1, lam_q2, lam_k2, subln_g, sink,
                          w_gate, b_gate, w_pa, w_pb, w_pc, w_o, ln1_g, ln1_b, w_pq, sub_keys,
                          peer_u, peer_v, ln2_g, ln2_b)
        y_prompt = _encoder_layer(y_prompt, mem_prompt, p)
        y_sample = _encoder_layer(y_sample, mem_sample, p)
    return (y_prompt, y_sample)
```

```python
import functools
import math

import jax
import jax.numpy as jnp
import numpy as np
from jax import lax
from jax.experimental import pallas as pl
from jax.experimental.pallas import tpu as pltpu

F32 = jnp.float32
BF16 = jnp.bfloat16

D_MODEL = 1024
N_MEM = 256
BLOCK = 128
DA_HEADS = 4
WA_HEADS = 8
MEM_HEADS = 4
MEM_DH = 128
BRANCH_W = 512
PEER_HEADS = 8
N_KEYS = 128
N_EXPERTS = N_KEYS * N_KEYS
PEER_TOPK = 16
LN_EPS = 1e-5
NEG = -1e30
LANES = 128
SUBLANES = 8

COL_QA, COL_KA, COL_VA, COL_QB, COL_QC, COL_KB, COL_VB = 0, 512, 1024, 1536, 2048, 2560, 2688
PROJ_COLS = 2816

VMEM_LIMIT = 56 * 1024 * 1024

_NT = (((1,), (1,)), ((), ()))


def _layer_norm(z, g, b):
    mu = jnp.mean(z, axis=-1, keepdims=True)
    zc = z - mu
    var = jnp.mean(zc * zc, axis=-1, keepdims=True)
    return zc * lax.rsqrt(var + LN_EPS) * g + b


def _proj_kernel(x_ref, w_ref, o_ref, *, n_chunk):
    xb = x_ref[...].astype(BF16)
    for c in range(0, o_ref.shape[-1], n_chunk):
        o_ref[:, c:c + n_chunk] = jnp.dot(
            xb, w_ref[:, c:c + n_chunk], preferred_element_type=F32).astype(BF16)


def _proj(x2d, w, tm):
    t, d = x2d.shape
    n = w.shape[1]
    return pl.pallas_call(
        functools.partial(_proj_kernel, n_chunk=256),
        out_shape=jax.ShapeDtypeStruct((t, n), BF16),
        grid=(t // tm,),
        in_specs=[pl.BlockSpec((tm, d), lambda i: (i, 0)),
                  pl.BlockSpec((d, n), lambda i: (0, 0))],
        out_specs=pl.BlockSpec((tm, n), lambda i: (i, 0)),
        compiler_params=pltpu.CompilerParams(
            dimension_semantics=("parallel",), vmem_limit_bytes=VMEM_LIMIT),
        name="proj",
    )(x2d, w)


def _attn_a_kernel(lo_ref, hi_ref, slopes_ref, lamp_ref, g_ref, q_ref, k_ref, v_ref, o_ref,
                   gm_ref, m_ref, l_ref, acc_ref, *, tq, n_streams, tk, lam_init):
    bb = pl.program_id(0)
    h = pl.program_id(1)
    i = pl.program_id(2)
    tg = n_streams * tq
    mid = tg // tk
    flat = (bb * pl.num_programs(1) + h) * pl.num_programs(2) + i
    lo = lo_ref[flat]
    hi = hi_ref[flat]
    slope2 = slopes_ref[h] * math.log2(math.e)
    reps = tk // LANES

    gm_ref[...] = (lax.broadcasted_iota(jnp.int32, (tq, tk), 0)
                   - lax.broadcasted_iota(jnp.int32, (tq, tk), 1)).astype(F32) * slope2
    m_ref[...] = jnp.full(m_ref.shape, -jnp.inf, F32)
    l_ref[...] = jnp.zeros(l_ref.shape, F32)
    acc_ref[...] = jnp.zeros(acc_ref.shape, F32)

    lane = lax.broadcasted_iota(jnp.int32, (tq, LANES), 1)
    q2 = []
    for st in range(n_streams):
        q = q_ref[st * tq:(st + 1) * tq, :]
        zero = jnp.zeros_like(q)
        q2.append(jnp.concatenate([jnp.where(lane < 64, q, zero),
                                   jnp.where(lane >= 64, q, zero)], axis=0))

    def step(st, kt, vt, bias, shift):
        s = lax.dot_general(q2[st], kt, _NT, preferred_element_type=F32)
        x = s + jnp.concatenate([bias, bias], axis=0)
        m_prev = m_ref[st]
        m_next = jnp.maximum(m_prev, jnp.max(x, axis=1, keepdims=True) - shift)
        alpha = jnp.exp2(m_prev - m_next)
        sub = m_next + shift
        p = jnp.exp2(x - jnp.concatenate([sub] * reps, axis=1))
        psum = p[:, 0:LANES]
        for c in range(1, reps):
            psum = psum + p[:, c * LANES:(c + 1) * LANES]
        l_ref[st] = alpha * l_ref[st] + psum
        acc_ref[st] = alpha * acc_ref[st] + jnp.dot(
            p.astype(BF16), vt, preferred_element_type=F32)
        m_ref[st] = m_next

    def tiles(j):
        ks = pl.multiple_of(j * tk, tk)
        return k_ref[pl.ds(ks, tk), :], v_ref[pl.ds(ks, tk), :]

    def left(j, carry):
        kt, vt = tiles(j)
        base = lax.convert_element_type(i * tg - j * tk, F32)
        for st in range(n_streams):
            step(st, kt, vt, -gm_ref[...], slope2 * (base + st * tq))
        return carry

    def right(j, carry):
        kt, vt = tiles(j)
        base = lax.convert_element_type(j * tk - i * tg, F32)
        for st in range(n_streams):
            step(st, kt, vt, gm_ref[...], slope2 * (base - st * tq))
        return carry

    lax.fori_loop(lo, i * mid, left, 0)
    for mj in range(mid):
        kt, vt = tiles(i * mid + mj)
        for st in range(n_streams):
            off = st * tq - mj * tk
            if off - (tk - 1) >= 0:
                step(st, kt, vt, -gm_ref[...], slope2 * off)
            elif off + (tq - 1) <= 0:
                step(st, kt, vt, gm_ref[...], slope2 * (-off))
            else:
                step(st, kt, vt, -jnp.abs(gm_ref[...] + slope2 * off), 0.0)
    lax.fori_loop((i + 1) * mid, hi, right, 0)

    lamp = lamp_ref[...]
    lam = (jnp.exp(jnp.sum(lamp[0:1] * lamp[1:2], axis=1, keepdims=True))
           - jnp.exp(jnp.sum(lamp[2:3] * lamp[3:4], axis=1, keepdims=True)) + lam_init)
    for st in range(n_streams):
        o = acc_ref[st] / jnp.sum(l_ref[st], axis=1, keepdims=True)
        o = o[:tq] - lam * o[tq:]
        ms = jnp.mean(o * o, axis=-1, keepdims=True)
        y = o * lax.rsqrt(ms + LN_EPS) * g_ref[...] * (1.0 - lam_init)
        o_ref[st * tq:(st + 1) * tq, :] = y.astype(o_ref.dtype)


UNDERFLOW_LOG2 = 151.0


def _attn_a_bounds(proj, slopes, tg, tk):
    b, s, _ = proj.shape
    ni, nk, mid = s // tg, s // tk, tg // tk
    q = proj[:, :, COL_QA:COL_QA + 512].astype(F32).reshape(b, s, DA_HEADS, 2, 64)
    k = proj[:, :, COL_KA:COL_KA + 512].astype(F32).reshape(b, s, DA_HEADS, 2, 64)
    qn = jnp.sqrt(jnp.sum(q * q, axis=-1)).reshape(b, ni, tg, DA_HEADS, 2)
    own = jnp.sum(q * k, axis=-1).reshape(b, ni, tg, DA_HEADS, 2)
    kmax = jnp.max(jnp.sqrt(jnp.sum(k * k, axis=-1)), axis=1)
    upper = jnp.max(qn, axis=2) * kmax[:, None] * 1.001
    slack = jnp.max(upper - jnp.min(own, axis=2), axis=-1) + UNDERFLOW_LOG2
    reach = slack / (slopes * math.log2(math.e))
    reach = jnp.minimum(reach, 4.0 * s)
    i0 = (jnp.arange(ni, dtype=F32) * tg)[None, :, None]
    lo = jnp.ceil((i0 + 1.0 - reach) / tk - 1.0)
    lo = jnp.clip(lo, 0, jnp.arange(ni, dtype=F32)[None, :, None] * mid)
    hi = jnp.floor((reach + i0 + tg - 1.0) / tk) + 1.0
    hi = jnp.clip(hi, (jnp.arange(ni, dtype=F32)[None, :, None] + 1.0) * mid, nk)
    flat = lambda a: a.astype(jnp.int32).transpose(0, 2, 1).reshape(-1)
    return flat(lo), flat(hi)


def _attn_a(proj, slopes, lamp, subln_g, lam_init, tq, n_streams, tk):
    b, s, _ = proj.shape
    tg = tq * n_streams
    assert tg % tk == 0 and s % tg == 0
    lo, hi = _attn_a_bounds(proj, slopes, tg, tk)
    kern = functools.partial(_attn_a_kernel, tq=tq, n_streams=n_streams, tk=tk, lam_init=lam_init)
    grid_spec = pltpu.PrefetchScalarGridSpec(
        num_scalar_prefetch=2,
        grid=(b, DA_HEADS, s // tg),
        in_specs=[
            pl.BlockSpec(memory_space=pltpu.SMEM),
            pl.BlockSpec((4, 64), lambda bb, h, i, lo, hi: (0, 0)),
            pl.BlockSpec((1, LANES), lambda bb, h, i, lo, hi: (0, 0)),
            pl.BlockSpec((None, tg, LANES), lambda bb, h, i, lo, hi: (bb, i, COL_QA // LANES + h)),
            pl.BlockSpec((None, s, LANES), lambda bb, h, i, lo, hi: (bb, 0, COL_KA // LANES + h)),
            pl.BlockSpec((None, s, LANES), lambda bb, h, i, lo, hi: (bb, 0, COL_VA // LANES + h)),
        ],
        out_specs=pl.BlockSpec((None, tg, LANES), lambda bb, h, i, lo, hi: (bb, i, h)),
        scratch_shapes=[pltpu.VMEM((tq, tk), F32),
                        pltpu.VMEM((n_streams, 2 * tq, LANES), F32),
                        pltpu.VMEM((n_streams, 2 * tq, LANES), F32),
                        pltpu.VMEM((n_streams, 2 * tq, LANES), F32)],
    )
    return pl.pallas_call(
        kern,
        out_shape=jax.ShapeDtypeStruct((b, s, BRANCH_W), BF16),
        grid_spec=grid_spec,
        compiler_params=pltpu.CompilerParams(
            dimension_semantics=("parallel", "parallel", "arbitrary"),
            vmem_limit_bytes=VMEM_LIMIT),
        name="attn_a",
    )(lo, hi, slopes, lamp, subln_g, proj, proj, proj)


def _attn_b_kernel(slopes_ref, sink_ref, q_ref, kp_ref, kc_ref, kn_ref, vp_ref, vc_ref, vn_ref,
                   o_ref, *, tq, seq):
    i = pl.program_id(1)
    kfull = jnp.concatenate([kp_ref[...], kc_ref[...], kn_ref[...]], axis=0)
    vfull = jnp.concatenate([vp_ref[...], vc_ref[...], vn_ref[...]], axis=0)
    band = 3 * BLOCK
    r = lax.broadcasted_iota(jnp.int32, (BLOCK, band), 0)
    c = lax.broadcasted_iota(jnp.int32, (BLOCK, band), 1)
    rel_i = jnp.abs(r + BLOCK - c)
    rel = rel_i.astype(F32)
    lane = lax.broadcasted_iota(jnp.int32, (BLOCK, LANES), 1)
    lo_half = lane < 64

    for sub in range(tq // BLOCK):
        q_start = i * tq + sub * BLOCK
        valid = ((rel_i <= BLOCK) & (c >= BLOCK - q_start) & (c < seq + BLOCK - q_start))
        kband = kfull[sub * BLOCK: sub * BLOCK + band]
        vband = vfull[sub * BLOCK: sub * BLOCK + band]
        qblk = q_ref[sub * BLOCK:(sub + 1) * BLOCK, :]
        parts = []
        for g in range(4):
            qg = qblk[:, g * LANES:(g + 1) * LANES]
            zero = jnp.zeros_like(qg)
            parts.append(jnp.where(lo_half, qg, zero))
            parts.append(jnp.where(lo_half, zero, qg))
        qs = jnp.concatenate(parts, axis=0)
        s_all = lax.dot_general(qs, kband, _NT, preferred_element_type=F32)
        ps, invs = [], []
        for n in range(8):
            hq = (n // 2) + 4 * (n % 2)
            s = s_all[n * BLOCK:(n + 1) * BLOCK]
            s = jnp.where(valid, s - slopes_ref[hq] * rel, NEG)
            sk = sink_ref[hq]
            m = jnp.maximum(jnp.max(s, axis=1, keepdims=True), sk)
            e = jnp.exp(s - m)
            den = jnp.sum(e, axis=1, keepdims=True) + jnp.exp(sk - m)
            ps.append(e.astype(BF16))
            invs.append(1.0 / den)
        p_all = jnp.concatenate(ps, axis=0)
        o_all = jnp.dot(p_all, vband, preferred_element_type=F32)
        for g in range(4):
            o_lo = o_all[(2 * g) * BLOCK:(2 * g + 1) * BLOCK] * invs[2 * g]
            o_hi = o_all[(2 * g + 1) * BLOCK:(2 * g + 2) * BLOCK] * invs[2 * g + 1]
            o_ref[sub * BLOCK:(sub + 1) * BLOCK, g * LANES:(g + 1) * LANES] = jnp.where(
                lo_half, o_lo, o_hi).astype(o_ref.dtype)


def _attn_b(proj, slopes, sink, tq):
    b, s, _ = proj.shape
    nb = s // BLOCK
    r = tq // BLOCK
    kcol, vcol = COL_KB // LANES, COL_VB // LANES

    def prev_map(col):
        return lambda bb, i: (bb, jnp.maximum(i * r - 1, 0), col)

    def cur_map(col):
        return lambda bb, i: (bb, i, col)

    def next_map(col):
        return lambda bb, i: (bb, jnp.minimum(i * r + r, nb - 1), col)

    return pl.pallas_call(
        functools.partial(_attn_b_kernel, tq=tq, seq=s),
        out_shape=jax.ShapeDtypeStruct((b, s, BRANCH_W), BF16),
        grid=(b, s // tq),
        in_specs=[
            pl.BlockSpec(memory_space=pltpu.SMEM),
            pl.BlockSpec(memory_space=pltpu.SMEM),
            pl.BlockSpec((None, tq, BRANCH_W), lambda bb, i: (bb, i, COL_QB // BRANCH_W)),
            pl.BlockSpec((None, BLOCK, LANES), prev_map(kcol)),
            pl.BlockSpec((None, tq, LANES), cur_map(kcol)),
            pl.BlockSpec((None, BLOCK, LANES), next_map(kcol)),
            pl.BlockSpec((None, BLOCK, LANES), prev_map(vcol)),
            pl.BlockSpec((None, tq, LANES), cur_map(vcol)),
            pl.BlockSpec((None, BLOCK, LANES), next_map(vcol)),
        ],
        out_specs=pl.BlockSpec((None, tq, BRANCH_W), lambda bb, i: (bb, i, 0)),
        compiler_params=pltpu.CompilerParams(
            dimension_semantics=("parallel", "parallel"), vmem_limit_bytes=VMEM_LIMIT),
        name="attn_b",
    )(slopes, sink, proj, proj, proj, proj, proj, proj, proj)


def _attn_c_kernel(q_ref, mem_ref, wkv_ref, o_ref, mk_ref, mv_ref):
    @pl.when(pl.program_id(1) == 0)
    def _():
        kv = jnp.dot(mem_ref[...].astype(BF16), wkv_ref[...], preferred_element_type=F32)
        mk_ref[...] = kv[:, :BRANCH_W].astype(BF16)
        mv_ref[...] = kv[:, BRANCH_W:].astype(BF16)

    scale = MEM_DH ** -0.5
    for h in range(MEM_HEADS):
        cols = slice(h * MEM_DH, (h + 1) * MEM_DH)
        s = lax.dot_general(q_ref[:, cols], mk_ref[:, cols], _NT,
                            preferred_element_type=F32) * scale
        m = jnp.max(s, axis=1, keepdims=True)
        e = jnp.exp(s - m)
        inv = 1.0 / jnp.sum(e, axis=1, keepdims=True)
        o = jnp.dot(e.astype(BF16), mv_ref[:, cols], preferred_element_type=F32)
        o_ref[:, cols] = (o * inv).astype(o_ref.dtype)


def _attn_c(proj, mem, wkv, tq):
    b, s, _ = proj.shape
    return pl.pallas_call(
        _attn_c_kernel,
        out_shape=jax.ShapeDtypeStruct((b, s, BRANCH_W), BF16),
        grid=(b, s // tq),
        in_specs=[
            pl.BlockSpec((None, tq, BRANCH_W), lambda bb, i: (bb, i, COL_QC // BRANCH_W)),
            pl.BlockSpec((None, N_MEM, D_MODEL), lambda bb, i: (bb, 0, 0)),
            pl.BlockSpec((D_MODEL, 2 * BRANCH_W), lambda bb, i: (0, 0)),
        ],
        out_specs=pl.BlockSpec((None, tq, BRANCH_W), lambda bb, i: (bb, i, 0)),
        scratch_shapes=[pltpu.VMEM((N_MEM, BRANCH_W), BF16), pltpu.VMEM((N_MEM, BRANCH_W), BF16)],
        compiler_params=pltpu.CompilerParams(
            dimension_semantics=("parallel", "arbitrary"), vmem_limit_bytes=VMEM_LIMIT),
        name="attn_c",
    )(proj, mem, wkv)


def _merge_kernel(x_ref, oa_ref, ob_ref, oc_ref, wg_ref, bg_ref, wpa_ref, wpb_ref, wpc_ref,
                  wo_ref, g_ref, b_ref, o_ref, *, alpha):
    x = x_ref[...]
    xb = x.astype(BF16)
    merged = None
    for n, (br_ref, wp_ref) in enumerate(((oa_ref, wpa_ref), (ob_ref, wpb_ref), (oc_ref, wpc_ref))):
        cols = slice(n * D_MODEL, (n + 1) * D_MODEL)
        gate = jax.nn.sigmoid(
            jnp.dot(xb, wg_ref[:, cols], preferred_element_type=F32) + bg_ref[:, cols])
        term = gate * jnp.dot(br_ref[...], wp_ref[...], preferred_element_type=F32)
        merged = term if merged is None else merged + term
    y = jnp.dot(merged.astype(BF16), wo_ref[...], preferred_element_type=F32)
    o_ref[...] = _layer_norm(alpha * x + y, g_ref[...], b_ref[...])


def _merge(x2d, oa, ob, oc, wg, bg, wpa, wpb, wpc, wo, g, b, alpha, tm):
    t, d = x2d.shape
    const = lambda i: (0, 0)
    row = lambda i: (i, 0)
    return pl.pallas_call(
        functools.partial(_merge_kernel, alpha=alpha),
        out_shape=jax.ShapeDtypeStruct((t, d), F32),
        grid=(t // tm,),
        in_specs=[
            pl.BlockSpec((tm, d), row),
            pl.BlockSpec((tm, BRANCH_W), row),
            pl.BlockSpec((tm, BRANCH_W), row),
            pl.BlockSpec((tm, BRANCH_W), row),
            pl.BlockSpec((d, 3 * d), const),
            pl.BlockSpec((1, 3 * d), const),
            pl.BlockSpec((BRANCH_W, d), const),
            pl.BlockSpec((BRANCH_W, d), const),
            pl.BlockSpec((BRANCH_W, d), const),
            pl.BlockSpec((d, d), const),
            pl.BlockSpec((1, d), const),
            pl.BlockSpec((1, d), const),
        ],
        out_specs=pl.BlockSpec((tm, d), row),
        compiler_params=pltpu.CompilerParams(
            dimension_semantics=("parallel",), vmem_limit_bytes=VMEM_LIMIT),
        name="merge",
    )(x2d, oa, ob, oc, wg, bg, wpa, wpb, wpc, wo, g, b)


def _sort_network(n):
    pairs = []

    def merge(lo, hi, r):
        step = r * 2
        if step < hi - lo:
            merge(lo, hi, step)
            merge(lo + r, hi, step)
            pairs.extend((k, k + r) for k in range(lo + r, hi - r, step))
        else:
            pairs.append((lo, lo + r))

    def sort(lo, hi):
        if hi - lo >= 1:
            mid = lo + (hi - lo) // 2
            sort(lo, mid)
            sort(mid + 1, hi)
            merge(lo, hi, 1)

    sort(0, n - 1)
    return pairs


_SORT16 = _sort_network(PEER_TOPK)


def _top16_desc(slabs):
    v = list(slabs)
    for a, b in _SORT16:
        hi, lo = jnp.maximum(v[a], v[b]), jnp.minimum(v[a], v[b])
        v[a], v[b] = hi, lo
    for shift in (4, 2, 1):
        v = [jnp.maximum(v[k], pltpu.roll(v[PEER_TOPK - 1 - k], shift, 0))
             for k in range(PEER_TOPK)]
        step = PEER_TOPK // 2
        while step >= 1:
            for k in range(PEER_TOPK):
                if k & step == 0:
                    hi, lo = jnp.maximum(v[k], v[k + step]), jnp.minimum(v[k], v[k + step])
                    v[k], v[k + step] = hi, lo
            step //= 2
    return v


def _peer_route(s0, s1):
    n = s0.shape[1]
    top_a = _top16_desc([s0[k * SUBLANES:(k + 1) * SUBLANES] for k in range(N_KEYS // SUBLANES)])
    top_b = _top16_desc([s1[k * SUBLANES:(k + 1) * SUBLANES] for k in range(N_KEYS // SUBLANES)])
    a0, b0 = top_a[0], top_b[0]
    ea = [jnp.exp(t - a0) for t in top_a]
    eb = [jnp.exp(t - b0) for t in top_b]
    sub = lax.broadcasted_iota(jnp.int32, (SUBLANES, n), 0)
    first4 = sub < 4

    def candidates(ea_list):
        col = ea_list[0]
        for s in range(1, 4):
            col = jnp.where(sub == s, ea_list[s], col)
        for s in range(4, 8):
            col = jnp.where(sub == s, eb[s - 4], col)
        out = []
        for v in range(PEER_TOPK):
            c = col * jnp.where(first4, eb[v], ea_list[v])
            if v < 4:
                c = jnp.where(first4, c, -1.0)
            out.append(c)
        return out

    cand = candidates(ea)
    best = _top16_desc(cand)
    z = best[0]
    for t in best[1:]:
        z = z + t
    inv_z = (2.0 ** -0.5) / z
    theta = best[PEER_TOPK - 1]
    ean = [t * inv_z for t in ea]
    cand_n = candidates(ean)
    thr = None
    for c, cn in zip(cand, cand_n):
        t = jnp.where(c >= theta, cn, jnp.inf)
        thr = t if thr is None else jnp.minimum(thr, t)
    for shift in (4, 2, 1):
        thr = jnp.minimum(thr, pltpu.roll(thr, shift, 0))
    e1n = jnp.exp(s0 - a0[0:1]) * inv_z[0:1]
    e2 = jnp.exp(s1 - b0[0:1])
    return e1n, e2, thr[0:1]


def _peer_kernel(x1_ref, wpq_ref, keys_ref, u_ref, vt_ref, g_ref, b_ref, o_ref,
                 xb_ref, q_ref, e1_ref, e2_ref, thr_ref, acc_ref, wa_ref, *, alpha, ec, lane_chunk):
    j = pl.program_id(1)
    tm = x1_ref.shape[0]
    n1 = ec // N_KEYS
    assert n1 == SUBLANES

    @pl.when(j == 0)
    def _():
        xb = x1_ref[...].astype(BF16)
        xb_ref[...] = xb
        acc_ref[...] = jnp.zeros(acc_ref.shape, F32)
        for c in range(0, q_ref.shape[1], 512):
            q_ref[:, c:c + 512] = jnp.dot(xb, wpq_ref[:, c:c + 512],
                                          preferred_element_type=F32).astype(BF16)
        for h in range(PEER_HEADS):
            s = []
            for half in range(2):
                r = 2 * h + half
                s.append(lax.dot_general(keys_ref[r], q_ref[:, r * N_KEYS:(r + 1) * N_KEYS], _NT,
                                         preferred_element_type=F32))
            for lt in range(0, tm, 2 * LANES):
                cols = slice(lt, lt + 2 * LANES)
                e1n, e2, thr = _peer_route(s[0][:, cols], s[1][:, cols])
                e1_ref[h, :, cols] = e1n
                e2_ref[h, :, cols] = e2
                thr_ref[h:h + 1, cols] = thr

    ht = lax.dot_general(u_ref[...], xb_ref[...], _NT, preferred_element_type=F32)
    base = pl.multiple_of(j * n1, SUBLANES)
    for ii in range(n1):
        rows = slice(ii * N_KEYS, (ii + 1) * N_KEYS)
        for lc in range(0, tm, lane_chunk):
            cols = slice(lc, lc + lane_chunk)
            w = None
            for h in range(PEER_HEADS):
                e1t = e1_ref[h, pl.ds(base, SUBLANES), cols]
                p = e1t[ii:ii + 1, :] * e2_ref[h, :, cols]
                t = jnp.where(p >= thr_ref[h:h + 1, cols], p, 0.0)
                w = t if w is None else w + t
            hblk = ht[rows, cols]
            act = hblk * (1.0 + lax.erf(hblk))
            wa_ref[rows, cols] = (w * act).astype(BF16)
    acc_ref[...] += jnp.dot(vt_ref[...], wa_ref[...], preferred_element_type=F32)

    @pl.when(j == pl.num_programs(1) - 1)
    def _():
        z = alpha * x1_ref[...] + acc_ref[...].T
        o_ref[...] = _layer_norm(z, g_ref[...], b_ref[...])


def _peer(x1, wpq, keys, u, vt, g, b, alpha, tm, ec, lane_chunk):
    t, d = x1.shape
    kern = functools.partial(_peer_kernel, alpha=alpha, ec=ec, lane_chunk=lane_chunk)
    n_chunks = N_EXPERTS // ec
    return pl.pallas_call(
        kern,
        out_shape=jax.ShapeDtypeStruct((t, d), F32),
        grid=(t // tm, n_chunks),
        in_specs=[
            pl.BlockSpec((tm, d), lambda i, j: (i, 0)),
            pl.BlockSpec(wpq.shape, lambda i, j: (0, 0)),
            pl.BlockSpec(keys.shape, lambda i, j: (0, 0, 0)),
            pl.BlockSpec((ec, d), lambda i, j: (j, 0)),
            pl.BlockSpec((d, ec), lambda i, j: (0, j)),
            pl.BlockSpec((1, d), lambda i, j: (0, 0)),
            pl.BlockSpec((1, d), lambda i, j: (0, 0)),
        ],
        out_specs=pl.BlockSpec((tm, d), lambda i, j: (i, 0)),
        scratch_shapes=[
            pltpu.VMEM((tm, d), BF16),
            pltpu.VMEM((tm, wpq.shape[1]), BF16),
            pltpu.VMEM((PEER_HEADS, N_KEYS, tm), F32),
            pltpu.VMEM((PEER_HEADS, N_KEYS, tm), F32),
            pltpu.VMEM((PEER_HEADS, tm), F32),
            pltpu.VMEM((d, tm), F32),
            pltpu.VMEM((ec, tm), BF16),
        ],
        compiler_params=pltpu.CompilerParams(
            dimension_semantics=("parallel", "arbitrary"), vmem_limit_bytes=VMEM_LIMIT),
        name="peer",
    )(x1, wpq, keys, u, vt, g, b)


def _prep_weights(depth, l, w_in, w_mem_kv, lam_q1, lam_k1, lam_q2, lam_k2, subln_g, sink, w_gate,
                  b_gate, w_pa, w_pb, w_pc, w_o, ln1_g, ln1_b, w_pq, sub_keys, peer_u, peer_v,
                  ln2_g, ln2_b):
    w = w_in[l]
    qb_perm = np.concatenate([np.arange(64) + 64 * hq for g in range(4) for hq in (g, g + 4)])
    qa = w[:, 0:512] * (0.125 * math.log2(math.e))
    ka, va = w[:, 512:1024], w[:, 1024:1536]
    qb = w[:, 1536:2048][:, qb_perm] * 0.125
    kb, vb, qc = w[:, 2048:2176], w[:, 2176:2304], w[:, 2304:2816]
    w_proj = jnp.concatenate([qa, ka, va, qb, qc, kb, vb], axis=1).astype(BF16)
    lamp = jnp.stack([lam_q1[l], lam_k1[l], lam_q2[l], lam_k2[l]]).astype(F32)
    row = lambda a: a.astype(F32).reshape(1, -1)
    return dict(
        w_proj=w_proj,
        w_mem_kv=w_mem_kv[l].astype(BF16),
        lamp=lamp,
        subln_g=row(subln_g[l]),
        sink=sink[l].astype(F32),
        w_gate=w_gate[l].astype(BF16),
        b_gate=row(b_gate[l]),
        w_pa=w_pa[l].astype(BF16),
        w_pb=w_pb[l][qb_perm].astype(BF16),
        w_pc=w_pc[l].astype(BF16),
        w_o=w_o[l].astype(BF16),
        ln1_g=row(ln1_g[l]), ln1_b=row(ln1_b[l]),
        w_pq=w_pq[l].astype(BF16),
        keys=sub_keys[l].reshape(2 * PEER_HEADS, N_KEYS, N_KEYS).astype(BF16),
        peer_u=(peer_u[l] * (2.0 ** -0.5)).astype(BF16),
        peer_vt=peer_v[l].T.astype(BF16),
        ln2_g=row(ln2_g[l]), ln2_b=row(ln2_b[l]),
        lam_init=0.8 - 0.6 * math.exp(-0.3 * l),
        alpha=(2.0 * depth) ** 0.25,
    )


def _tile(n, pref):
    t = min(n, pref)
    assert n % t == 0, (n, t)
    return t


def _encoder_layer(x, mem, p):
    b, s, d = x.shape
    t = b * s
    x2d = x.reshape(t, d)
    proj = _proj(x2d, p["w_proj"], _tile(t, 512)).reshape(b, s, PROJ_COLS)
    slopes_a = jnp.asarray(2.0 ** (-8.0 * np.arange(1, DA_HEADS + 1) / DA_HEADS), F32)
    slopes_b = jnp.asarray(2.0 ** (-8.0 * np.arange(1, WA_HEADS + 1) / WA_HEADS), F32)
    oa = _attn_a(proj, slopes_a, p["lamp"], p["subln_g"], p["lam_init"], 256, 4, 512)
    ob = _attn_b(proj, slopes_b, p["sink"], _tile(s, 256))
    oc = _attn_c(proj, mem, p["w_mem_kv"], _tile(s, 512))
    x1 = _merge(x2d, oa.reshape(t, BRANCH_W), ob.reshape(t, BRANCH_W), oc.reshape(t, BRANCH_W),
                p["w_gate"], p["b_gate"], p["w_pa"], p["w_pb"], p["w_pc"], p["w_o"],
                p["ln1_g"], p["ln1_b"], p["alpha"], _tile(t, 256))
    y = _peer(x1, p["w_pq"], p["keys"], p["peer_u"], p["peer_vt"], p["ln2_g"], p["ln2_b"],
              p["alpha"], _tile(t, 512), 1024, 256)
    return y.reshape(b, s, d)


def kernel(x_prompt, x_sample, mem_prompt, mem_sample, w_in, w_mem_kv, lam_q1, lam_k1, lam_q2,
           lam_k2, subln_g, sink, w_gate, b_gate, w_pa, w_pb, w_pc, w_o, ln1_g, ln1_b, w_pq,
           sub_keys, peer_u, peer_v, ln2_g, ln2_b):
    depth = w_in.shape[0]
    y_prompt, y_sample = x_prompt, x_sample
    for l in range(depth):
        p = _prep_weights(depth, l, w_in, w_mem_kv, lam_q1, lam_k1, lam_q2, lam_k2, subln_g, sink,
                          w_gate, b_gate, w_pa, w_pb, w_pc, w_o, ln1_g, ln1_b, w_pq, sub_keys,
                          peer_u, peer_v, ln2_g, ln2_b)
        y_prompt = _encoder_layer(y_prompt, mem_prompt, p)
        y_sample = _encoder_layer(y_sample, mem_sample, p)
    return (y_prompt, y_sample)
```

```python
import functools
import math

import jax
import jax.numpy as jnp
import numpy as np
from jax import lax
from jax.experimental import pallas as pl
from jax.experimental.pallas import tpu as pltpu

F32 = jnp.float32
BF16 = jnp.bfloat16

D_MODEL = 1024
N_MEM = 256
BLOCK = 128
DA_HEADS = 4
WA_HEADS = 8
MEM_HEADS = 4
MEM_DH = 128
BRANCH_W = 512
PEER_HEADS = 8
N_KEYS = 128
N_EXPERTS = N_KEYS * N_KEYS
PEER_TOPK = 16
LN_EPS = 1e-5
NEG = -1e30
LANES = 128
SUBLANES = 8

COL_QA, COL_KA, COL_VA, COL_QB, COL_QC, COL_KB, COL_VB = 0, 512, 1024, 1536, 2048, 2560, 2688
PROJ_COLS = 2816

VMEM_LIMIT = 56 * 1024 * 1024

_NT = (((1,), (1,)), ((), ()))


def _layer_norm(z, g, b):
    mu = jnp.mean(z, axis=-1, keepdims=True)
    zc = z - mu
    var = jnp.mean(zc * zc, axis=-1, keepdims=True)
    return zc * lax.rsqrt(var + LN_EPS) * g + b


def _proj_kernel(x_ref, w_ref, o_ref, *, n_chunk):
    xb = x_ref[...].astype(BF16)
    for c in range(0, o_ref.shape[-1], n_chunk):
        o_ref[:, c:c + n_chunk] = jnp.dot(
            xb, w_ref[:, c:c + n_chunk], preferred_element_type=F32).astype(BF16)


def _proj(x2d, w, tm):
    t, d = x2d.shape
    n = w.shape[1]
    return pl.pallas_call(
        functools.partial(_proj_kernel, n_chunk=256),
        out_shape=jax.ShapeDtypeStruct((t, n), BF16),
        grid=(t // tm,),
        in_specs=[pl.BlockSpec((tm, d), lambda i: (i, 0)),
                  pl.BlockSpec((d, n), lambda i: (0, 0))],
        out_specs=pl.BlockSpec((tm, n), lambda i: (i, 0)),
        compiler_params=pltpu.CompilerParams(
            dimension_semantics=("parallel",), vmem_limit_bytes=VMEM_LIMIT),
        name="proj",
    )(x2d, w)


def _attn_a_kernel(lo_ref, hi_ref, slopes_ref, lamp_ref, g_ref, q_ref, k_ref, v_ref, o_ref,
                   gm_ref, m_ref, l_ref, acc_ref, *, tq, n_streams, tk, lam_init):
    bb = pl.program_id(0)
    h = pl.program_id(1)
    i = pl.program_id(2)
    tg = n_streams * tq
    mid = tg // tk
    flat = (bb * pl.num_programs(1) + h) * pl.num_programs(2) + i
    lo = lo_ref[flat]
    hi = hi_ref[flat]
    slope2 = slopes_ref[4 * h]
    pieces = [slopes_ref[4 * h + 1 + n] for n in range(3)]
    reps = tk // LANES
    assert tq <= 256 and tk % 256 == 0

    gm_ref[...] = (lax.broadcasted_iota(jnp.int32, (tq, tk), 0)
                   - lax.broadcasted_iota(jnp.int32, (tq, tk), 1)).astype(F32) * slope2
    m_ref[...] = jnp.full(m_ref.shape, -jnp.inf, F32)
    l_ref[...] = jnp.zeros(l_ref.shape, F32)
    acc_ref[...] = jnp.zeros(acc_ref.shape, F32)

    lane = lax.broadcasted_iota(jnp.int32, (tq, LANES), 1)
    row = lax.broadcasted_iota(jnp.int32, (tq, LANES), 0).astype(F32)
    qf = jnp.where(lane < 3, row, 0.0)
    klane = lax.broadcasted_iota(jnp.int32, (tk, LANES), 1)
    col = lax.broadcasted_iota(jnp.int32, (tk, LANES), 0)
    col_lo = (col & 255).astype(F32)
    kf = jnp.where((klane >= 3) & (klane < 6), col_lo,
                   jnp.where((klane >= 6) & (klane < 9), col.astype(F32) - col_lo, 0.0))
    for n in range(3):
        qf = jnp.where((lane == 3 + n) | (lane == 6 + n), -pieces[n], qf)
        kf = jnp.where(klane == n, pieces[n], kf)
    qf = jnp.concatenate([qf, qf], axis=0)
    kf = kf.astype(BF16)
    ones_col = jnp.where(klane == 0, 1.0, 0.0).astype(BF16)

    q2, q_right, q_left = [], [], []
    for st in range(n_streams):
        q = q_ref[st * tq:(st + 1) * tq, :]
        zero = jnp.zeros_like(q)
        q2.append(jnp.concatenate([jnp.where(lane < 64, q, zero),
                                   jnp.where(lane >= 64, q, zero)], axis=0))
        q_right.append(jnp.concatenate([q2[st], qf.astype(BF16)], axis=1))
        q_left.append(jnp.concatenate([q2[st], (-qf).astype(BF16)], axis=1))

    def step(st, lhs, rhs, v_aug, bias, shift):
        x = lax.dot_general(lhs, rhs, _NT, preferred_element_type=F32)
        if bias is not None:
            x = x + jnp.concatenate([bias, bias], axis=0)
        m_prev = m_ref[st]
        m_next = jnp.maximum(m_prev, jnp.max(x, axis=1, keepdims=True) - shift)
        alpha = jnp.exp2(m_prev - m_next)
        sub = m_next + shift
        p = jnp.exp2(x - jnp.concatenate([sub] * reps, axis=1))
        pv = jnp.dot(p.astype(BF16), v_aug, preferred_element_type=F32)
        l_ref[st] = alpha * l_ref[st] + pv[:, LANES:]
        acc_ref[st] = alpha * acc_ref[st] + pv[:, :LANES]
        m_ref[st] = m_next

    def tiles(j):
        ks = pl.multiple_of(j * tk, tk)
        kt = k_ref[pl.ds(ks, tk), :]
        v_aug = jnp.concatenate([v_ref[pl.ds(ks, tk), :], ones_col], axis=1)
        return kt, jnp.concatenate([kt, kf], axis=1), v_aug

    def left(j, carry):
        _, k_aug, v_aug = tiles(j)
        base = lax.convert_element_type(i * tg - j * tk, F32)
        for st in range(n_streams):
            step(st, q_left[st], k_aug, v_aug, None, slope2 * (base + st * tq))
        return carry

    def right(j, carry):
        _, k_aug, v_aug = tiles(j)
        base = lax.convert_element_type(j * tk - i * tg, F32)
        for st in range(n_streams):
            step(st, q_right[st], k_aug, v_aug, None, slope2 * (base - st * tq))
        return carry

    lax.fori_loop(lo, i * mid, left, 0)
    for mj in range(mid):
        kt, k_aug, v_aug = tiles(i * mid + mj)
        for st in range(n_streams):
            off = st * tq - mj * tk
            if off - (tk - 1) >= 0:
                step(st, q_left[st], k_aug, v_aug, None, slope2 * off)
            elif off + (tq - 1) <= 0:
                step(st, q_right[st], k_aug, v_aug, None, slope2 * (-off))
            else:
                step(st, q2[st], kt, v_aug, -jnp.abs(gm_ref[...] + slope2 * off), 0.0)
    lax.fori_loop((i + 1) * mid, hi, right, 0)

    lamp = lamp_ref[...]
    lam = (jnp.exp(jnp.sum(lamp[0:1] * lamp[1:2], axis=1, keepdims=True))
           - jnp.exp(jnp.sum(lamp[2:3] * lamp[3:4], axis=1, keepdims=True)) + lam_init)
    for st in range(n_streams):
        o = acc_ref[st] / jnp.sum(l_ref[st], axis=1, keepdims=True)
        o = o[:tq] - lam * o[tq:]
        ms = jnp.mean(o * o, axis=-1, keepdims=True)
        y = o * lax.rsqrt(ms + LN_EPS) * g_ref[...] * (1.0 - lam_init)
        o_ref[st * tq:(st + 1) * tq, :] = y.astype(o_ref.dtype)


UNDERFLOW_LOG2 = 151.0


def _attn_a_bounds(proj, slopes, tg, tk):
    b, s, _ = proj.shape
    ni, nk, mid = s // tg, s // tk, tg // tk
    q = proj[:, :, COL_QA:COL_QA + 512].astype(F32).reshape(b, s, DA_HEADS, 2, 64)
    k = proj[:, :, COL_KA:COL_KA + 512].astype(F32).reshape(b, s, DA_HEADS, 2, 64)
    qn = jnp.sqrt(jnp.sum(q * q, axis=-1)).reshape(b, ni, tg, DA_HEADS, 2)
    own = jnp.sum(q * k, axis=-1).reshape(b, ni, tg, DA_HEADS, 2)
    kmax = jnp.max(jnp.sqrt(jnp.sum(k * k, axis=-1)), axis=1)
    upper = jnp.max(qn, axis=2) * kmax[:, None] * 1.001
    slack = jnp.max(upper - jnp.min(own, axis=2), axis=-1) + UNDERFLOW_LOG2
    reach = slack / (slopes * math.log2(math.e))
    reach = jnp.minimum(reach, 4.0 * s)
    i0 = (jnp.arange(ni, dtype=F32) * tg)[None, :, None]
    lo = jnp.ceil((i0 + 1.0 - reach) / tk - 1.0)
    lo = jnp.clip(lo, 0, jnp.arange(ni, dtype=F32)[None, :, None] * mid)
    hi = jnp.floor((reach + i0 + tg - 1.0) / tk) + 1.0
    hi = jnp.clip(hi, (jnp.arange(ni, dtype=F32)[None, :, None] + 1.0) * mid, nk)
    flat = lambda a: a.astype(jnp.int32).transpose(0, 2, 1).reshape(-1)
    return flat(lo), flat(hi)


def _slope_pieces(slopes):
    rows = []
    for s in slopes:
        s2 = np.float32(np.float32(s) * np.float32(math.log2(math.e)))
        hi = np.float32(np.asarray(s2, dtype=jnp.bfloat16))
        mid = np.float32(np.asarray(np.float32(s2 - hi), dtype=jnp.bfloat16))
        lo = np.float32(np.asarray(np.float32(s2 - hi - mid), dtype=jnp.bfloat16))
        assert np.float32(np.float32(hi + mid) + lo) == s2
        rows += [s2, hi, mid, lo]
    return np.asarray(rows, np.float32)


def _attn_a(proj, slopes, slope_tab, lamp, subln_g, lam_init, tq, n_streams, tk):
    b, s, _ = proj.shape
    tg = tq * n_streams
    assert tg % tk == 0 and s % tg == 0
    lo, hi = _attn_a_bounds(proj, slopes, tg, tk)
    kern = functools.partial(_attn_a_kernel, tq=tq, n_streams=n_streams, tk=tk, lam_init=lam_init)
    grid_spec = pltpu.PrefetchScalarGridSpec(
        num_scalar_prefetch=2,
        grid=(b, DA_HEADS, s // tg),
        in_specs=[
            pl.BlockSpec(memory_space=pltpu.SMEM),
            pl.BlockSpec((4, 64), lambda bb, h, i, lo, hi: (0, 0)),
            pl.BlockSpec((1, LANES), lambda bb, h, i, lo, hi: (0, 0)),
            pl.BlockSpec((None, tg, LANES), lambda bb, h, i, lo, hi: (bb, i, COL_QA // LANES + h)),
            pl.BlockSpec((None, s, LANES), lambda bb, h, i, lo, hi: (bb, 0, COL_KA // LANES + h)),
            pl.BlockSpec((None, s, LANES), lambda bb, h, i, lo, hi: (bb, 0, COL_VA // LANES + h)),
        ],
        out_specs=pl.BlockSpec((None, tg, LANES), lambda bb, h, i, lo, hi: (bb, i, h)),
        scratch_shapes=[pltpu.VMEM((tq, tk), F32),
                        pltpu.VMEM((n_streams, 2 * tq, LANES), F32),
                        pltpu.VMEM((n_streams, 2 * tq, LANES), F32),
                        pltpu.VMEM((n_streams, 2 * tq, LANES), F32)],
    )
    return pl.pallas_call(
        kern,
        out_shape=jax.ShapeDtypeStruct((b, s, BRANCH_W), BF16),
        grid_spec=grid_spec,
        compiler_params=pltpu.CompilerParams(
            dimension_semantics=("parallel", "parallel", "arbitrary"),
            vmem_limit_bytes=VMEM_LIMIT),
        name="attn_a",
    )(lo, hi, slope_tab, lamp, subln_g, proj, proj, proj)


def _attn_b_kernel(slopes_ref, sink_ref, q_ref, kp_ref, kc_ref, kn_ref, vp_ref, vc_ref, vn_ref,
                   o_ref, *, tq, seq):
    i = pl.program_id(1)
    kfull = jnp.concatenate([kp_ref[...], kc_ref[...], kn_ref[...]], axis=0)
    vfull = jnp.concatenate([vp_ref[...], vc_ref[...], vn_ref[...]], axis=0)
    band = 3 * BLOCK
    r = lax.broadcasted_iota(jnp.int32, (BLOCK, band), 0)
    c = lax.broadcasted_iota(jnp.int32, (BLOCK, band), 1)
    rel_i = jnp.abs(r + BLOCK - c)
    rel = rel_i.astype(F32)
    lane = lax.broadcasted_iota(jnp.int32, (BLOCK, LANES), 1)
    lo_half = lane < 64

    for sub in range(tq // BLOCK):
        q_start = i * tq + sub * BLOCK
        valid = ((rel_i <= BLOCK) & (c >= BLOCK - q_start) & (c < seq + BLOCK - q_start))
        kband = kfull[sub * BLOCK: sub * BLOCK + band]
        vband = vfull[sub * BLOCK: sub * BLOCK + band]
        qblk = q_ref[sub * BLOCK:(sub + 1) * BLOCK, :]
        parts = []
        for g in range(4):
            qg = qblk[:, g * LANES:(g + 1) * LANES]
            zero = jnp.zeros_like(qg)
            parts.append(jnp.where(lo_half, qg, zero))
            parts.append(jnp.where(lo_half, zero, qg))
        qs = jnp.concatenate(parts, axis=0)
        s_all = lax.dot_general(qs, kband, _NT, preferred_element_type=F32)
        ps, invs = [], []
        for n in range(8):
            hq = (n // 2) + 4 * (n % 2)
            s = s_all[n * BLOCK:(n + 1) * BLOCK]
            s = jnp.where(valid, s - slopes_ref[hq] * rel, NEG)
            sk = sink_ref[hq]
            m = jnp.maximum(jnp.max(s, axis=1, keepdims=True), sk)
            e = jnp.exp(s - m)
            den = jnp.sum(e, axis=1, keepdims=True) + jnp.exp(sk - m)
            ps.append(e.astype(BF16))
            invs.append(1.0 / den)
        p_all = jnp.concatenate(ps, axis=0)
        o_all = jnp.dot(p_all, vband, preferred_element_type=F32)
        for g in range(4):
            o_lo = o_all[(2 * g) * BLOCK:(2 * g + 1) * BLOCK] * invs[2 * g]
            o_hi = o_all[(2 * g + 1) * BLOCK:(2 * g + 2) * BLOCK] * invs[2 * g + 1]
            o_ref[sub * BLOCK:(sub + 1) * BLOCK, g * LANES:(g + 1) * LANES] = jnp.where(
                lo_half, o_lo, o_hi).astype(o_ref.dtype)


def _attn_b(proj, slopes, sink, tq):
    b, s, _ = proj.shape
    nb = s // BLOCK
    r = tq // BLOCK
    kcol, vcol = COL_KB // LANES, COL_VB // LANES

    def prev_map(col):
        return lambda bb, i: (bb, jnp.maximum(i * r - 1, 0), col)

    def cur_map(col):
        return lambda bb, i: (bb, i, col)

    def next_map(col):
        return lambda bb, i: (bb, jnp.minimum(i * r + r, nb - 1), col)

    return pl.pallas_call(
        functools.partial(_attn_b_kernel, tq=tq, seq=s),
        out_shape=jax.ShapeDtypeStruct((b, s, BRANCH_W), BF16),
        grid=(b, s // tq),
        in_specs=[
            pl.BlockSpec(memory_space=pltpu.SMEM),
            pl.BlockSpec(memory_space=pltpu.SMEM),
            pl.BlockSpec((None, tq, BRANCH_W), lambda bb, i: (bb, i, COL_QB // BRANCH_W)),
            pl.BlockSpec((None, BLOCK, LANES), prev_map(kcol)),
            pl.BlockSpec((None, tq, LANES), cur_map(kcol)),
            pl.BlockSpec((None, BLOCK, LANES), next_map(kcol)),
            pl.BlockSpec((None, BLOCK, LANES), prev_map(vcol)),
            pl.BlockSpec((None, tq, LANES), cur_map(vcol)),
            pl.BlockSpec((None, BLOCK, LANES), next_map(vcol)),
        ],
        out_specs=pl.BlockSpec((None, tq, BRANCH_W), lambda bb, i: (bb, i, 0)),
        compiler_params=pltpu.CompilerParams(
            dimension_semantics=("parallel", "parallel"), vmem_limit_bytes=VMEM_LIMIT),
        name="attn_b",
    )(slopes, sink, proj, proj, proj, proj, proj, proj, proj)


def _attn_c_kernel(q_ref, mem_ref, wkv_ref, o_ref, mk_ref, mv_ref):
    @pl.when(pl.program_id(1) == 0)
    def _():
        kv = jnp.dot(mem_ref[...].astype(BF16), wkv_ref[...], preferred_element_type=F32)
        mk_ref[...] = kv[:, :BRANCH_W].astype(BF16)
        mv_ref[...] = kv[:, BRANCH_W:].astype(BF16)

    scale = MEM_DH ** -0.5
    for h in range(MEM_HEADS):
        cols = slice(h * MEM_DH, (h + 1) * MEM_DH)
        s = lax.dot_general(q_ref[:, cols], mk_ref[:, cols], _NT,
                            preferred_element_type=F32) * scale
        m = jnp.max(s, axis=1, keepdims=True)
        e = jnp.exp(s - m)
        inv = 1.0 / jnp.sum(e, axis=1, keepdims=True)
        o = jnp.dot(e.astype(BF16), mv_ref[:, cols], preferred_element_type=F32)
        o_ref[:, cols] = (o * inv).astype(o_ref.dtype)


def _attn_c(proj, mem, wkv, tq):
    b, s, _ = proj.shape
    return pl.pallas_call(
        _attn_c_kernel,
        out_shape=jax.ShapeDtypeStruct((b, s, BRANCH_W), BF16),
        grid=(b, s // tq),
        in_specs=[
            pl.BlockSpec((None, tq, BRANCH_W), lambda bb, i: (bb, i, COL_QC // BRANCH_W)),
            pl.BlockSpec((None, N_MEM, D_MODEL), lambda bb, i: (bb, 0, 0)),
            pl.BlockSpec((D_MODEL, 2 * BRANCH_W), lambda bb, i: (0, 0)),
        ],
        out_specs=pl.BlockSpec((None, tq, BRANCH_W), lambda bb, i: (bb, i, 0)),
        scratch_shapes=[pltpu.VMEM((N_MEM, BRANCH_W), BF16), pltpu.VMEM((N_MEM, BRANCH_W), BF16)],
        compiler_params=pltpu.CompilerParams(
            dimension_semantics=("parallel", "arbitrary"), vmem_limit_bytes=VMEM_LIMIT),
        name="attn_c",
    )(proj, mem, wkv)


def _merge_kernel(x_ref, oa_ref, ob_ref, oc_ref, wg_ref, bg_ref, wpa_ref, wpb_ref, wpc_ref,
                  wo_ref, g_ref, b_ref, o_ref, *, alpha):
    x = x_ref[...]
    xb = x.astype(BF16)
    merged = None
    for n, (br_ref, wp_ref) in enumerate(((oa_ref, wpa_ref), (ob_ref, wpb_ref), (oc_ref, wpc_ref))):
        cols = slice(n * D_MODEL, (n + 1) * D_MODEL)
        gate = jax.nn.sigmoid(
            jnp.dot(xb, wg_ref[:, cols], preferred_element_type=F32) + bg_ref[:, cols])
        term = gate * jnp.dot(br_ref[...], wp_ref[...], preferred_element_type=F32)
        merged = term if merged is None else merged + term
    y = jnp.dot(merged.astype(BF16), wo_ref[...], preferred_element_type=F32)
    o_ref[...] = _layer_norm(alpha * x + y, g_ref[...], b_ref[...])


def _merge(x2d, oa, ob, oc, wg, bg, wpa, wpb, wpc, wo, g, b, alpha, tm):
    t, d = x2d.shape
    const = lambda i: (0, 0)
    row = lambda i: (i, 0)
    return pl.pallas_call(
        functools.partial(_merge_kernel, alpha=alpha),
        out_shape=jax.ShapeDtypeStruct((t, d), F32),
        grid=(t // tm,),
        in_specs=[
            pl.BlockSpec((tm, d), row),
            pl.BlockSpec((tm, BRANCH_W), row),
            pl.BlockSpec((tm, BRANCH_W), row),
            pl.BlockSpec((tm, BRANCH_W), row),
            pl.BlockSpec((d, 3 * d), const),
            pl.BlockSpec((1, 3 * d), const),
            pl.BlockSpec((BRANCH_W, d), const),
            pl.BlockSpec((BRANCH_W, d), const),
            pl.BlockSpec((BRANCH_W, d), const),
            pl.BlockSpec((d, d), const),
            pl.BlockSpec((1, d), const),
            pl.BlockSpec((1, d), const),
        ],
        out_specs=pl.BlockSpec((tm, d), row),
        compiler_params=pltpu.CompilerParams(
            dimension_semantics=("parallel",), vmem_limit_bytes=VMEM_LIMIT),
        name="merge",
    )(x2d, oa, ob, oc, wg, bg, wpa, wpb, wpc, wo, g, b)


def _sort_network(n):
    pairs = []

    def merge(lo, hi, r):
        step = r * 2
        if step < hi - lo:
            merge(lo, hi, step)
            merge(lo + r, hi, step)
            pairs.extend((k, k + r) for k in range(lo + r, hi - r, step))
        else:
            pairs.append((lo, lo + r))

    def sort(lo, hi):
        if hi - lo >= 1:
            mid = lo + (hi - lo) // 2
            sort(lo, mid)
            sort(mid + 1, hi)
            merge(lo, hi, 1)

    sort(0, n - 1)
    return pairs


_SORT16 = _sort_network(PEER_TOPK)


def _top16_desc(slabs):
    v = list(slabs)
    for a, b in _SORT16:
        hi, lo = jnp.maximum(v[a], v[b]), jnp.minimum(v[a], v[b])
        v[a], v[b] = hi, lo
    for shift in (4, 2, 1):
        v = [jnp.maximum(v[k], pltpu.roll(v[PEER_TOPK - 1 - k], shift, 0))
             for k in range(PEER_TOPK)]
        step = PEER_TOPK // 2
        while step >= 1:
            for k in range(PEER_TOPK):
                if k & step == 0:
                    hi, lo = jnp.maximum(v[k], v[k + step]), jnp.minimum(v[k], v[k + step])
                    v[k], v[k + step] = hi, lo
            step //= 2
    return v


def _peer_route(s0, s1):
    n = s0.shape[1]
    top_a = _top16_desc([s0[k * SUBLANES:(k + 1) * SUBLANES] for k in range(N_KEYS // SUBLANES)])
    top_b = _top16_desc([s1[k * SUBLANES:(k + 1) * SUBLANES] for k in range(N_KEYS // SUBLANES)])
    a0, b0 = top_a[0], top_b[0]
    ea = [jnp.exp(t - a0) for t in top_a]
    eb = [jnp.exp(t - b0) for t in top_b]
    sub = lax.broadcasted_iota(jnp.int32, (SUBLANES, n), 0)
    first4 = sub < 4

    def candidates(ea_list):
        col = ea_list[0]
        for s in range(1, 4):
            col = jnp.where(sub == s, ea_list[s], col)
        for s in range(4, 8):
            col = jnp.where(sub == s, eb[s - 4], col)
        out = []
        for v in range(PEER_TOPK):
            c = col * jnp.where(first4, eb[v], ea_list[v])
            if v < 4:
                c = jnp.where(first4, c, -1.0)
            out.append(c)
        return out

    cand = candidates(ea)
    best = _top16_desc(cand)
    z = best[0]
    for t in best[1:]:
        z = z + t
    inv_z = (2.0 ** -0.5) / z
    theta = best[PEER_TOPK - 1]
    ean = [t * inv_z for t in ea]
    cand_n = candidates(ean)
    thr = None
    for c, cn in zip(cand, cand_n):
        t = jnp.where(c >= theta, cn, jnp.inf)
        thr = t if thr is None else jnp.minimum(thr, t)
    for shift in (4, 2, 1):
        thr = jnp.minimum(thr, pltpu.roll(thr, shift, 0))
    thr = thr[0:1]
    e1n = jnp.exp(s0 - a0[0:1]) * inv_z[0:1]
    e2 = jnp.exp(s1 - b0[0:1])
    psi = jnp.full(s0.shape, float(PEER_TOPK), F32)
    for r in range(PEER_TOPK):
        psi = jnp.where(e1n * eb[r][0:1] >= thr, float(PEER_TOPK - 1 - r), psi)
    code2 = jnp.zeros(s1.shape, F32)
    for r in reversed(range(PEER_TOPK)):
        code2 = jnp.where(s1 >= top_b[r][0:1], float(PEER_TOPK - r), code2)
    return e1n, e2, psi, code2


def _peer_kernel(x1_ref, wpq_ref, keys_ref, u_ref, vt_ref, g_ref, b_ref, o_ref,
                 xb_ref, q_ref, e1_ref, psi_ref, e2_ref, code_ref, acc_ref, wa_ref, *,
                 alpha, ec, lane_chunk):
    j = pl.program_id(1)
    tm = x1_ref.shape[0]
    n1 = ec // N_KEYS
    assert n1 == SUBLANES

    @pl.when(j == 0)
    def _():
        xb = x1_ref[...].astype(BF16)
        xb_ref[...] = xb
        acc_ref[...] = jnp.zeros(acc_ref.shape, F32)
        for c in range(0, q_ref.shape[1], 512):
            q_ref[:, c:c + 512] = jnp.dot(xb, wpq_ref[:, c:c + 512],
                                          preferred_element_type=F32).astype(BF16)
        for h in range(PEER_HEADS):
            s = []
            for half in range(2):
                r = 2 * h + half
                s.append(lax.dot_general(keys_ref[r], q_ref[:, r * N_KEYS:(r + 1) * N_KEYS], _NT,
                                         preferred_element_type=F32))
            for lt in range(0, tm, 2 * LANES):
                cols = slice(lt, lt + 2 * LANES)
                e1n, e2, psi, code2 = _peer_route(s[0][:, cols], s[1][:, cols])
                e1_ref[h, :, cols] = e1n
                psi_ref[h, :, cols] = psi
                e2_ref[h, :, cols] = e2.astype(BF16)
                code_ref[h, :, cols] = code2.astype(BF16)

    def packed_row(tile, ii):
        r16 = jnp.broadcast_to(tile[ii:ii + 1, :], (2 * SUBLANES, tile.shape[1])).astype(BF16)
        return jnp.concatenate([r16] * (N_KEYS // (2 * SUBLANES)), axis=0)

    ht = lax.dot_general(u_ref[...], xb_ref[...], _NT, preferred_element_type=F32)
    base = pl.multiple_of(j * n1, SUBLANES)
    for ii in range(n1):
        rows = slice(ii * N_KEYS, (ii + 1) * N_KEYS)
        for lc in range(0, tm, lane_chunk):
            cols = slice(lc, lc + lane_chunk)
            w = None
            for h in range(PEER_HEADS):
                e1b = packed_row(e1_ref[h, pl.ds(base, SUBLANES), cols], ii)
                psib = packed_row(psi_ref[h, pl.ds(base, SUBLANES), cols], ii)
                p = e1b * e2_ref[h, :, cols]
                t = jnp.where(code_ref[h, :, cols] > psib, p, jnp.zeros_like(p))
                w = t if w is None else w + t
            hblk = ht[rows, cols]
            act = hblk * (1.0 + lax.erf(hblk))
            wa_ref[rows, cols] = w * act.astype(BF16)
    acc_ref[...] += jnp.dot(vt_ref[...], wa_ref[...], preferred_element_type=F32)

    @pl.when(j == pl.num_programs(1) - 1)
    def _():
        z = alpha * x1_ref[...] + acc_ref[...].T
        o_ref[...] = _layer_norm(z, g_ref[...], b_ref[...])


def _peer(x1, wpq, keys, u, vt, g, b, alpha, tm, ec, lane_chunk):
    t, d = x1.shape
    kern = functools.partial(_peer_kernel, alpha=alpha, ec=ec, lane_chunk=lane_chunk)
    n_chunks = N_EXPERTS // ec
    return pl.pallas_call(
        kern,
        out_shape=jax.ShapeDtypeStruct((t, d), F32),
        grid=(t // tm, n_chunks),
        in_specs=[
            pl.BlockSpec((tm, d), lambda i, j: (i, 0)),
            pl.BlockSpec(wpq.shape, lambda i, j: (0, 0)),
            pl.BlockSpec(keys.shape, lambda i, j: (0, 0, 0)),
            pl.BlockSpec((ec, d), lambda i, j: (j, 0)),
            pl.BlockSpec((d, ec), lambda i, j: (0, j)),
            pl.BlockSpec((1, d), lambda i, j: (0, 0)),
            pl.BlockSpec((1, d), lambda i, j: (0, 0)),
        ],
        out_specs=pl.BlockSpec((tm, d), lambda i, j: (i, 0)),
        scratch_shapes=[
            pltpu.VMEM((tm, d), BF16),
            pltpu.VMEM((tm, wpq.shape[1]), BF16),
            pltpu.VMEM((PEER_HEADS, N_KEYS, tm), F32),
            pltpu.VMEM((PEER_HEADS, N_KEYS, tm), F32),
            pltpu.VMEM((PEER_HEADS, N_KEYS, tm), BF16),
            pltpu.VMEM((PEER_HEADS, N_KEYS, tm), BF16),
            pltpu.VMEM((d, tm), F32),
            pltpu.VMEM((ec, tm), BF16),
        ],
        compiler_params=pltpu.CompilerParams(
            dimension_semantics=("parallel", "arbitrary"), vmem_limit_bytes=VMEM_LIMIT),
        name="peer",
    )(x1, wpq, keys, u, vt, g, b)


def _prep_weights(depth, l, w_in, w_mem_kv, lam_q1, lam_k1, lam_q2, lam_k2, subln_g, sink, w_gate,
                  b_gate, w_pa, w_pb, w_pc, w_o, ln1_g, ln1_b, w_pq, sub_keys, peer_u, peer_v,
                  ln2_g, ln2_b):
    w = w_in[l]
    qb_perm = np.concatenate([np.arange(64) + 64 * hq for g in range(4) for hq in (g, g + 4)])
    qa = w[:, 0:512] * (0.125 * math.log2(math.e))
    ka, va = w[:, 512:1024], w[:, 1024:1536]
    qb = w[:, 1536:2048][:, qb_perm] * 0.125
    kb, vb, qc = w[:, 2048:2176], w[:, 2176:2304], w[:, 2304:2816]
    w_proj = jnp.concatenate([qa, ka, va, qb, qc, kb, vb], axis=1).astype(BF16)
    lamp = jnp.stack([lam_q1[l], lam_k1[l], lam_q2[l], lam_k2[l]]).astype(F32)
    row = lambda a: a.astype(F32).reshape(1, -1)
    return dict(
        w_proj=w_proj,
        w_mem_kv=w_mem_kv[l].astype(BF16),
        lamp=lamp,
        subln_g=row(subln_g[l]),
        sink=sink[l].astype(F32),
        w_gate=w_gate[l].astype(BF16),
        b_gate=row(b_gate[l]),
        w_pa=w_pa[l].astype(BF16),
        w_pb=w_pb[l][qb_perm].astype(BF16),
        w_pc=w_pc[l].astype(BF16),
        w_o=w_o[l].astype(BF16),
        ln1_g=row(ln1_g[l]), ln1_b=row(ln1_b[l]),
        w_pq=w_pq[l].astype(BF16),
        keys=sub_keys[l].reshape(2 * PEER_HEADS, N_KEYS, N_KEYS).astype(BF16),
        peer_u=(peer_u[l] * (2.0 ** -0.5)).astype(BF16),
        peer_vt=peer_v[l].T.astype(BF16),
        ln2_g=row(ln2_g[l]), ln2_b=row(ln2_b[l]),
        lam_init=0.8 - 0.6 * math.exp(-0.3 * l),
        alpha=(2.0 * depth) ** 0.25,
    )


def _tile(n, pref):
    t = min(n, pref)
    assert n % t == 0, (n, t)
    return t


def _encoder_layer(x, mem, p):
    b, s, d = x.shape
    t = b * s
    x2d = x.reshape(t, d)
    proj = _proj(x2d, p["w_proj"], _tile(t, 512)).reshape(b, s, PROJ_COLS)
    slopes_a = jnp.asarray(2.0 ** (-8.0 * np.arange(1, DA_HEADS + 1) / DA_HEADS), F32)
    slope_tab = jnp.asarray(_slope_pieces(2.0 ** (-8.0 * np.arange(1, DA_HEADS + 1) / DA_HEADS)))
    slopes_b = jnp.asarray(2.0 ** (-8.0 * np.arange(1, WA_HEADS + 1) / WA_HEADS), F32)
    oa = _attn_a(proj, slopes_a, slope_tab, p["lamp"], p["subln_g"], p["lam_init"], 256, 4, 512)
    ob = _attn_b(proj, slopes_b, p["sink"], _tile(s, 256))
    oc = _attn_c(proj, mem, p["w_mem_kv"], _tile(s, 512))
    x1 = _merge(x2d, oa.reshape(t, BRANCH_W), ob.reshape(t, BRANCH_W), oc.reshape(t, BRANCH_W),
                p["w_gate"], p["b_gate"], p["w_pa"], p["w_pb"], p["w_pc"], p["w_o"],
                p["ln1_g"], p["ln1_b"], p["alpha"], _tile(t, 256))
    y = _peer(x1, p["w_pq"], p["keys"], p["peer_u"], p["peer_vt"], p["ln2_g"], p["ln2_b"],
              p["alpha"], _tile(t, 512), 1024, 256)
    return y.reshape(b, s, d)


def kernel(x_prompt, x_sample, mem_prompt, mem_sample, w_in, w_mem_kv, lam_q1, lam_k1, lam_q2,
           lam_k2, subln_g, sink, w_gate, b_gate, w_pa, w_pb, w_pc, w_o, ln1_g, ln1_b, w_pq,
           sub_keys, peer_u, peer_v, ln2_g, ln2_b):
    depth = w_in.shape[0]
    y_prompt, y_sample = x_prompt, x_sample
    for l in range(depth):
        p = _prep_weights(depth, l, w_in, w_mem_kv, lam_q1, lam_k1, lam_q2, lam_k2, subln_g, sink,
                          w_gate, b_gate, w_pa, w_pb, w_pc, w_o, ln1_g, ln1_b, w_pq, sub_keys,
                          peer_u, peer_v, ln2_g, ln2_b)
        y_prompt = _encoder_layer(y_prompt, mem_prompt, p)
        y_sample = _encoder_layer(y_sample, mem_sample, p)
    return (y_prompt, y_sample)
```

```python
import functools
import math

import jax
import jax.numpy as jnp
import numpy as np
from jax import lax
from jax.experimental import pallas as pl
from jax.experimental.pallas import tpu as pltpu

F32 = jnp.float32
BF16 = jnp.bfloat16

D_MODEL = 1024
N_MEM = 256
BLOCK = 128
DA_HEADS = 4
WA_HEADS = 8
MEM_HEADS = 4
MEM_DH = 128
BRANCH_W = 512
PEER_HEADS = 8
N_KEYS = 128
N_EXPERTS = N_KEYS * N_KEYS
PEER_TOPK = 16
LN_EPS = 1e-5
NEG = -1e30
LANES = 128
SUBLANES = 8

COL_QA, COL_KA, COL_VA, COL_QB, COL_QC, COL_KB, COL_VB = 0, 512, 1024, 1536, 2048, 2560, 2688
PROJ_COLS = 2816

VMEM_LIMIT = 56 * 1024 * 1024

_NT = (((1,), (1,)), ((), ()))


def _layer_norm(z, g, b):
    mu = jnp.mean(z, axis=-1, keepdims=True)
    zc = z - mu
    var = jnp.mean(zc * zc, axis=-1, keepdims=True)
    return zc * lax.rsqrt(var + LN_EPS) * g + b


def _proj_kernel(x_ref, w_ref, o_ref, *, n_chunk):
    xb = x_ref[...].astype(BF16)
    for c in range(0, o_ref.shape[-1], n_chunk):
        o_ref[:, c:c + n_chunk] = jnp.dot(
            xb, w_ref[:, c:c + n_chunk], preferred_element_type=F32).astype(BF16)


def _proj(x2d, w, tm):
    t, d = x2d.shape
    n = w.shape[1]
    return pl.pallas_call(
        functools.partial(_proj_kernel, n_chunk=256),
        out_shape=jax.ShapeDtypeStruct((t, n), BF16),
        grid=(t // tm,),
        in_specs=[pl.BlockSpec((tm, d), lambda i: (i, 0)),
                  pl.BlockSpec((d, n), lambda i: (0, 0))],
        out_specs=pl.BlockSpec((tm, n), lambda i: (i, 0)),
        compiler_params=pltpu.CompilerParams(
            dimension_semantics=("parallel",), vmem_limit_bytes=VMEM_LIMIT),
        name="proj",
    )(x2d, w)


def _attn_a_kernel(lo_ref, hi_ref, slopes_ref, lamp_ref, g_ref, q_ref, k_ref, v_ref, o_ref,
                   gm_ref, m_ref, l_ref, acc_ref, *, tq, n_streams, tk, lam_init):
    bb = pl.program_id(0)
    h = pl.program_id(1)
    i = pl.program_id(2)
    tg = n_streams * tq
    mid = tg // tk
    flat = (bb * pl.num_programs(1) + h) * pl.num_programs(2) + i
    lo = lo_ref[flat]
    hi = hi_ref[flat]
    slope2 = slopes_ref[4 * h]
    pieces = [slopes_ref[4 * h + 1 + n] for n in range(3)]
    reps = tk // LANES
    assert tq <= 256 and tk % 256 == 0

    gm_ref[...] = (lax.broadcasted_iota(jnp.int32, (tq, tk), 0)
                   - lax.broadcasted_iota(jnp.int32, (tq, tk), 1)).astype(F32) * slope2
    m_ref[...] = jnp.full(m_ref.shape, -jnp.inf, F32)
    l_ref[...] = jnp.zeros(l_ref.shape, F32)
    acc_ref[...] = jnp.zeros(acc_ref.shape, F32)

    lane = lax.broadcasted_iota(jnp.int32, (tq, LANES), 1)
    row = lax.broadcasted_iota(jnp.int32, (tq, LANES), 0).astype(F32)
    qf = jnp.where(lane < 3, row, 0.0)
    klane = lax.broadcasted_iota(jnp.int32, (tk, LANES), 1)
    col = lax.broadcasted_iota(jnp.int32, (tk, LANES), 0)
    col_lo = (col & 255).astype(F32)
    kf = jnp.where((klane >= 3) & (klane < 6), col_lo,
                   jnp.where((klane >= 6) & (klane < 9), col.astype(F32) - col_lo, 0.0))
    for n in range(3):
        qf = jnp.where((lane == 3 + n) | (lane == 6 + n), -pieces[n], qf)
        kf = jnp.where(klane == n, pieces[n], kf)
    qf = jnp.concatenate([qf, qf], axis=0)
    kf = kf.astype(BF16)
    ones_col = jnp.where(klane == 0, 1.0, 0.0).astype(BF16)

    q2, q_right, q_left = [], [], []
    for st in range(n_streams):
        q = q_ref[st * tq:(st + 1) * tq, :]
        zero = jnp.zeros_like(q)
        q2.append(jnp.concatenate([jnp.where(lane < 64, q, zero),
                                   jnp.where(lane >= 64, q, zero)], axis=0))
        q_right.append(jnp.concatenate([q2[st], qf.astype(BF16)], axis=1))
        q_left.append(jnp.concatenate([q2[st], (-qf).astype(BF16)], axis=1))

    def step(st, lhs, rhs, v_aug, bias, shift):
        x = lax.dot_general(lhs, rhs, _NT, preferred_element_type=F32)
        if bias is not None:
            x = x + jnp.concatenate([bias, bias], axis=0)
        m_prev = m_ref[st]
        m_next = jnp.maximum(m_prev, jnp.max(x, axis=1, keepdims=True) - shift)
        alpha = jnp.exp2(m_prev - m_next)
        sub = m_next + shift
        p = jnp.exp2(x - jnp.concatenate([sub] * reps, axis=1))
        pv = jnp.dot(p.astype(BF16), v_aug, preferred_element_type=F32)
        l_ref[st] = alpha * l_ref[st] + pv[:, LANES:]
        acc_ref[st] = alpha * acc_ref[st] + pv[:, :LANES]
        m_ref[st] = m_next

    def tiles(j):
        ks = pl.multiple_of(j * tk, tk)
        kt = k_ref[pl.ds(ks, tk), :]
        v_aug = jnp.concatenate([v_ref[pl.ds(ks, tk), :], ones_col], axis=1)
        return kt, jnp.concatenate([kt, kf], axis=1), v_aug

    def left(j, carry):
        _, k_aug, v_aug = tiles(j)
        base = lax.convert_element_type(i * tg - j * tk, F32)
        for st in range(n_streams):
            step(st, q_left[st], k_aug, v_aug, None, slope2 * (base + st * tq))
        return carry

    def right(j, carry):
        _, k_aug, v_aug = tiles(j)
        base = lax.convert_element_type(j * tk - i * tg, F32)
        for st in range(n_streams):
            step(st, q_right[st], k_aug, v_aug, None, slope2 * (base - st * tq))
        return carry

    lax.fori_loop(lo, i * mid, left, 0)
    for mj in range(mid):
        kt, k_aug, v_aug = tiles(i * mid + mj)
        for st in range(n_streams):
            off = st * tq - mj * tk
            if off - (tk - 1) >= 0:
                step(st, q_left[st], k_aug, v_aug, None, slope2 * off)
            elif off + (tq - 1) <= 0:
                step(st, q_right[st], k_aug, v_aug, None, slope2 * (-off))
            else:
                step(st, q2[st], kt, v_aug, -jnp.abs(gm_ref[...] + slope2 * off), 0.0)
    lax.fori_loop((i + 1) * mid, hi, right, 0)

    lamp = lamp_ref[...]
    lam = (jnp.exp(jnp.sum(lamp[0:1] * lamp[1:2], axis=1, keepdims=True))
           - jnp.exp(jnp.sum(lamp[2:3] * lamp[3:4], axis=1, keepdims=True)) + lam_init)
    for st in range(n_streams):
        o = acc_ref[st] / jnp.sum(l_ref[st], axis=1, keepdims=True)
        o = o[:tq] - lam * o[tq:]
        ms = jnp.mean(o * o, axis=-1, keepdims=True)
        y = o * lax.rsqrt(ms + LN_EPS) * g_ref[...] * (1.0 - lam_init)
        o_ref[st * tq:(st + 1) * tq, :] = y.astype(o_ref.dtype)


UNDERFLOW_LOG2 = 151.0


def _attn_a_bounds(proj, slopes, tg, tk):
    b, s, _ = proj.shape
    ni, nk, mid = s // tg, s // tk, tg // tk
    q = proj[:, :, COL_QA:COL_QA + 512].astype(F32)
    k = proj[:, :, COL_KA:COL_KA + 512].astype(F32)
    group = jnp.asarray(np.repeat(np.eye(2 * DA_HEADS, dtype=np.float32), 64, axis=0))

    def group_sums(a):
        return jnp.einsum("bsc,cg->bsg", a, group, precision=lax.Precision.HIGHEST,
                          preferred_element_type=F32)

    qn = jnp.sqrt(group_sums(q * q)).reshape(b, ni, tg, DA_HEADS, 2)
    own = group_sums(q * k).reshape(b, ni, tg, DA_HEADS, 2)
    kmax = jnp.max(jnp.sqrt(group_sums(k * k)), axis=1).reshape(b, DA_HEADS, 2)
    upper = jnp.max(qn, axis=2) * kmax[:, None] * 1.001
    slack = jnp.max(upper - jnp.min(own, axis=2), axis=-1) + UNDERFLOW_LOG2
    reach = slack / (slopes * math.log2(math.e))
    reach = jnp.minimum(reach, 4.0 * s)
    i0 = (jnp.arange(ni, dtype=F32) * tg)[None, :, None]
    lo = jnp.ceil((i0 + 1.0 - reach) / tk - 1.0)
    lo = jnp.clip(lo, 0, jnp.arange(ni, dtype=F32)[None, :, None] * mid)
    hi = jnp.floor((reach + i0 + tg - 1.0) / tk) + 1.0
    hi = jnp.clip(hi, (jnp.arange(ni, dtype=F32)[None, :, None] + 1.0) * mid, nk)
    flat = lambda a: a.astype(jnp.int32).transpose(0, 2, 1).reshape(-1)
    return flat(lo), flat(hi)


def _slope_pieces(slopes):
    rows = []
    for s in slopes:
        s2 = np.float32(np.float32(s) * np.float32(math.log2(math.e)))
        hi = np.float32(np.asarray(s2, dtype=jnp.bfloat16))
        mid = np.float32(np.asarray(np.float32(s2 - hi), dtype=jnp.bfloat16))
        lo = np.float32(np.asarray(np.float32(s2 - hi - mid), dtype=jnp.bfloat16))
        assert np.float32(np.float32(hi + mid) + lo) == s2
        rows += [s2, hi, mid, lo]
    return np.asarray(rows, np.float32)


def _attn_a(proj, slopes, slope_tab, lamp, subln_g, lam_init, tq, n_streams, tk):
    b, s, _ = proj.shape
    tg = tq * n_streams
    assert tg % tk == 0 and s % tg == 0
    lo, hi = _attn_a_bounds(proj, slopes, tg, tk)
    kern = functools.partial(_attn_a_kernel, tq=tq, n_streams=n_streams, tk=tk, lam_init=lam_init)
    grid_spec = pltpu.PrefetchScalarGridSpec(
        num_scalar_prefetch=2,
        grid=(b, DA_HEADS, s // tg),
        in_specs=[
            pl.BlockSpec(memory_space=pltpu.SMEM),
            pl.BlockSpec((4, 64), lambda bb, h, i, lo, hi: (0, 0)),
            pl.BlockSpec((1, LANES), lambda bb, h, i, lo, hi: (0, 0)),
            pl.BlockSpec((None, tg, LANES), lambda bb, h, i, lo, hi: (bb, i, COL_QA // LANES + h)),
            pl.BlockSpec((None, s, LANES), lambda bb, h, i, lo, hi: (bb, 0, COL_KA // LANES + h)),
            pl.BlockSpec((None, s, LANES), lambda bb, h, i, lo, hi: (bb, 0, COL_VA // LANES + h)),
        ],
        out_specs=pl.BlockSpec((None, tg, LANES), lambda bb, h, i, lo, hi: (bb, i, h)),
        scratch_shapes=[pltpu.VMEM((tq, tk), F32),
                        pltpu.VMEM((n_streams, 2 * tq, LANES), F32),
                        pltpu.VMEM((n_streams, 2 * tq, LANES), F32),
                        pltpu.VMEM((n_streams, 2 * tq, LANES), F32)],
    )
    return pl.pallas_call(
        kern,
        out_shape=jax.ShapeDtypeStruct((b, s, BRANCH_W), BF16),
        grid_spec=grid_spec,
        compiler_params=pltpu.CompilerParams(
            dimension_semantics=("parallel", "parallel", "arbitrary"),
            vmem_limit_bytes=VMEM_LIMIT),
        name="attn_a",
    )(lo, hi, slope_tab, lamp, subln_g, proj, proj, proj)


def _attn_b_kernel(slopes_ref, sink_ref, q_ref, kp_ref, kc_ref, kn_ref, vp_ref, vc_ref, vn_ref,
                   o_ref, *, tq, seq):
    i = pl.program_id(1)
    kfull = jnp.concatenate([kp_ref[...], kc_ref[...], kn_ref[...]], axis=0)
    vfull = jnp.concatenate([vp_ref[...], vc_ref[...], vn_ref[...]], axis=0)
    band = 3 * BLOCK
    r = lax.broadcasted_iota(jnp.int32, (BLOCK, band), 0)
    c = lax.broadcasted_iota(jnp.int32, (BLOCK, band), 1)
    rel_i = jnp.abs(r + BLOCK - c)
    rel = rel_i.astype(F32)
    lane = lax.broadcasted_iota(jnp.int32, (BLOCK, LANES), 1)
    lo_half = lane < 64

    for sub in range(tq // BLOCK):
        q_start = i * tq + sub * BLOCK
        valid = ((rel_i <= BLOCK) & (c >= BLOCK - q_start) & (c < seq + BLOCK - q_start))
        kband = kfull[sub * BLOCK: sub * BLOCK + band]
        vband = vfull[sub * BLOCK: sub * BLOCK + band]
        qblk = q_ref[sub * BLOCK:(sub + 1) * BLOCK, :]
        parts = []
        for g in range(4):
            qg = qblk[:, g * LANES:(g + 1) * LANES]
            zero = jnp.zeros_like(qg)
            parts.append(jnp.where(lo_half, qg, zero))
            parts.append(jnp.where(lo_half, zero, qg))
        qs = jnp.concatenate(parts, axis=0)
        s_all = lax.dot_general(qs, kband, _NT, preferred_element_type=F32)
        ps, invs = [], []
        for n in range(8):
            hq = (n // 2) + 4 * (n % 2)
            s = s_all[n * BLOCK:(n + 1) * BLOCK]
            s = jnp.where(valid, s - slopes_ref[hq] * rel, NEG)
            sk = sink_ref[hq]
            m = jnp.maximum(jnp.max(s, axis=1, keepdims=True), sk)
            e = jnp.exp(s - m)
            den = jnp.sum(e, axis=1, keepdims=True) + jnp.exp(sk - m)
            ps.append(e.astype(BF16))
            invs.append(1.0 / den)
        p_all = jnp.concatenate(ps, axis=0)
        o_all = jnp.dot(p_all, vband, preferred_element_type=F32)
        for g in range(4):
            o_lo = o_all[(2 * g) * BLOCK:(2 * g + 1) * BLOCK] * invs[2 * g]
            o_hi = o_all[(2 * g + 1) * BLOCK:(2 * g + 2) * BLOCK] * invs[2 * g + 1]
            o_ref[sub * BLOCK:(sub + 1) * BLOCK, g * LANES:(g + 1) * LANES] = jnp.where(
                lo_half, o_lo, o_hi).astype(o_ref.dtype)


def _attn_b(proj, slopes, sink, tq):
    b, s, _ = proj.shape
    nb = s // BLOCK
    r = tq // BLOCK
    kcol, vcol = COL_KB // LANES, COL_VB // LANES

    def prev_map(col):
        return lambda bb, i: (bb, jnp.maximum(i * r - 1, 0), col)

    def cur_map(col):
        return lambda bb, i: (bb, i, col)

    def next_map(col):
        return lambda bb, i: (bb, jnp.minimum(i * r + r, nb - 1), col)

    return pl.pallas_call(
        functools.partial(_attn_b_kernel, tq=tq, seq=s),
        out_shape=jax.ShapeDtypeStruct((b, s, BRANCH_W), BF16),
        grid=(b, s // tq),
        in_specs=[
            pl.BlockSpec(memory_space=pltpu.SMEM),
            pl.BlockSpec(memory_space=pltpu.SMEM),
            pl.BlockSpec((None, tq, BRANCH_W), lambda bb, i: (bb, i, COL_QB // BRANCH_W)),
            pl.BlockSpec((None, BLOCK, LANES), prev_map(kcol)),
            pl.BlockSpec((None, tq, LANES), cur_map(kcol)),
            pl.BlockSpec((None, BLOCK, LANES), next_map(kcol)),
            pl.BlockSpec((None, BLOCK, LANES), prev_map(vcol)),
            pl.BlockSpec((None, tq, LANES), cur_map(vcol)),
            pl.BlockSpec((None, BLOCK, LANES), next_map(vcol)),
        ],
        out_specs=pl.BlockSpec((None, tq, BRANCH_W), lambda bb, i: (bb, i, 0)),
        compiler_params=pltpu.CompilerParams(
            dimension_semantics=("parallel", "parallel"), vmem_limit_bytes=VMEM_LIMIT),
        name="attn_b",
    )(slopes, sink, proj, proj, proj, proj, proj, proj, proj)


def _attn_c_kernel(q_ref, mem_ref, wkv_ref, o_ref, mk_ref, mv_ref):
    @pl.when(pl.program_id(1) == 0)
    def _():
        kv = jnp.dot(mem_ref[...].astype(BF16), wkv_ref[...], preferred_element_type=F32)
        mk_ref[...] = kv[:, :BRANCH_W].astype(BF16)
        mv_ref[...] = kv[:, BRANCH_W:].astype(BF16)

    scale = MEM_DH ** -0.5
    for h in range(MEM_HEADS):
        cols = slice(h * MEM_DH, (h + 1) * MEM_DH)
        s = lax.dot_general(q_ref[:, cols], mk_ref[:, cols], _NT,
                            preferred_element_type=F32) * scale
        m = jnp.max(s, axis=1, keepdims=True)
        e = jnp.exp(s - m)
        inv = 1.0 / jnp.sum(e, axis=1, keepdims=True)
        o = jnp.dot(e.astype(BF16), mv_ref[:, cols], preferred_element_type=F32)
        o_ref[:, cols] = (o * inv).astype(o_ref.dtype)


def _attn_c(proj, mem, wkv, tq):
    b, s, _ = proj.shape
    return pl.pallas_call(
        _attn_c_kernel,
        out_shape=jax.ShapeDtypeStruct((b, s, BRANCH_W), BF16),
        grid=(b, s // tq),
        in_specs=[
            pl.BlockSpec((None, tq, BRANCH_W), lambda bb, i: (bb, i, COL_QC // BRANCH_W)),
            pl.BlockSpec((None, N_MEM, D_MODEL), lambda bb, i: (bb, 0, 0)),
            pl.BlockSpec((D_MODEL, 2 * BRANCH_W), lambda bb, i: (0, 0)),
        ],
        out_specs=pl.BlockSpec((None, tq, BRANCH_W), lambda bb, i: (bb, i, 0)),
        scratch_shapes=[pltpu.VMEM((N_MEM, BRANCH_W), BF16), pltpu.VMEM((N_MEM, BRANCH_W), BF16)],
        compiler_params=pltpu.CompilerParams(
            dimension_semantics=("parallel", "arbitrary"), vmem_limit_bytes=VMEM_LIMIT),
        name="attn_c",
    )(proj, mem, wkv)


def _merge_kernel(x_ref, oa_ref, ob_ref, oc_ref, wg_ref, bg_ref, wpa_ref, wpb_ref, wpc_ref,
                  wo_ref, g_ref, b_ref, o_ref, *, alpha):
    x = x_ref[...]
    xb = x.astype(BF16)
    merged = None
    for n, (br_ref, wp_ref) in enumerate(((oa_ref, wpa_ref), (ob_ref, wpb_ref), (oc_ref, wpc_ref))):
        cols = slice(n * D_MODEL, (n + 1) * D_MODEL)
        gate = jax.nn.sigmoid(
            jnp.dot(xb, wg_ref[:, cols], preferred_element_type=F32) + bg_ref[:, cols])
        term = gate * jnp.dot(br_ref[...], wp_ref[...], preferred_element_type=F32)
        merged = term if merged is None else merged + term
    y = jnp.dot(merged.astype(BF16), wo_ref[...], preferred_element_type=F32)
    o_ref[...] = _layer_norm(alpha * x + y, g_ref[...], b_ref[...])


def _merge(x2d, oa, ob, oc, wg, bg, wpa, wpb, wpc, wo, g, b, alpha, tm):
    t, d = x2d.shape
    const = lambda i: (0, 0)
    row = lambda i: (i, 0)
    return pl.pallas_call(
        functools.partial(_merge_kernel, alpha=alpha),
        out_shape=jax.ShapeDtypeStruct((t, d), F32),
        grid=(t // tm,),
        in_specs=[
            pl.BlockSpec((tm, d), row),
            pl.BlockSpec((tm, BRANCH_W), row),
            pl.BlockSpec((tm, BRANCH_W), row),
            pl.BlockSpec((tm, BRANCH_W), row),
            pl.BlockSpec((d, 3 * d), const),
            pl.BlockSpec((1, 3 * d), const),
            pl.BlockSpec((BRANCH_W, d), const),
            pl.BlockSpec((BRANCH_W, d), const),
            pl.BlockSpec((BRANCH_W, d), const),
            pl.BlockSpec((d, d), const),
            pl.BlockSpec((1, d), const),
            pl.BlockSpec((1, d), const),
        ],
        out_specs=pl.BlockSpec((tm, d), row),
        compiler_params=pltpu.CompilerParams(
            dimension_semantics=("parallel",), vmem_limit_bytes=VMEM_LIMIT),
        name="merge",
    )(x2d, oa, ob, oc, wg, bg, wpa, wpb, wpc, wo, g, b)


def _sort_network(n):
    pairs = []

    def merge(lo, hi, r):
        step = r * 2
        if step < hi - lo:
            merge(lo, hi, step)
            merge(lo + r, hi, step)
            pairs.extend((k, k + r) for k in range(lo + r, hi - r, step))
        else:
            pairs.append((lo, lo + r))

    def sort(lo, hi):
        if hi - lo >= 1:
            mid = lo + (hi - lo) // 2
            sort(lo, mid)
            sort(mid + 1, hi)
            merge(lo, hi, 1)

    sort(0, n - 1)
    return pairs


_SORT16 = _sort_network(PEER_TOPK)


def _top16_desc(slabs):
    v = list(slabs)
    for a, b in _SORT16:
        hi, lo = jnp.maximum(v[a], v[b]), jnp.minimum(v[a], v[b])
        v[a], v[b] = hi, lo
    for shift in (4, 2, 1):
        v = [jnp.maximum(v[k], pltpu.roll(v[PEER_TOPK - 1 - k], shift, 0))
             for k in range(PEER_TOPK)]
        step = PEER_TOPK // 2
        while step >= 1:
            for k in range(PEER_TOPK):
                if k & step == 0:
                    hi, lo = jnp.maximum(v[k], v[k + step]), jnp.minimum(v[k], v[k + step])
                    v[k], v[k + step] = hi, lo
            step //= 2
    return v


def _peer_route(s0, s1):
    n = s0.shape[1]
    top_a = _top16_desc([s0[k * SUBLANES:(k + 1) * SUBLANES] for k in range(N_KEYS // SUBLANES)])
    top_b = _top16_desc([s1[k * SUBLANES:(k + 1) * SUBLANES] for k in range(N_KEYS // SUBLANES)])
    a0, b0 = top_a[0], top_b[0]
    ea = [jnp.exp(t - a0) for t in top_a]
    eb = [jnp.exp(t - b0) for t in top_b]
    sub = lax.broadcasted_iota(jnp.int32, (SUBLANES, n), 0)
    first4 = sub < 4

    def candidates(ea_list):
        col = ea_list[0]
        for s in range(1, 4):
            col = jnp.where(sub == s, ea_list[s], col)
        for s in range(4, 8):
            col = jnp.where(sub == s, eb[s - 4], col)
        out = []
        for v in range(PEER_TOPK):
            c = col * jnp.where(first4, eb[v], ea_list[v])
            if v < 4:
                c = jnp.where(first4, c, -1.0)
            out.append(c)
        return out

    cand = candidates(ea)
    best = _top16_desc(cand)
    z = best[0]
    for t in best[1:]:
        z = z + t
    inv_z = (2.0 ** -0.5) / z
    theta = best[PEER_TOPK - 1]
    ean = [t * inv_z for t in ea]
    cand_n = candidates(ean)
    thr = None
    for c, cn in zip(cand, cand_n):
        t = jnp.where(c >= theta, cn, jnp.inf)
        thr = t if thr is None else jnp.minimum(thr, t)
    for shift in (4, 2, 1):
        thr = jnp.minimum(thr, pltpu.roll(thr, shift, 0))
    thr = thr[0:1]
    e1n = jnp.exp(s0 - a0[0:1]) * inv_z[0:1]
    e2 = jnp.exp(s1 - b0[0:1])
    psi = jnp.full(s0.shape, float(PEER_TOPK), F32)
    for r in range(PEER_TOPK):
        psi = jnp.where(e1n * eb[r][0:1] >= thr, float(PEER_TOPK - 1 - r), psi)
    code2 = jnp.zeros(s1.shape, F32)
    for r in reversed(range(PEER_TOPK)):
        code2 = jnp.where(s1 >= top_b[r][0:1], float(PEER_TOPK - r), code2)
    return e1n, e2, psi, code2


def _peer_kernel(x1_ref, wpq_ref, keys_ref, u_ref, vt_ref, g_ref, b_ref, o_ref,
                 xb_ref, q_ref, e1_ref, psi_ref, e2_ref, code_ref, acc_ref, wa_ref, *,
                 alpha, ec, lane_chunk):
    j = pl.program_id(1)
    tm = x1_ref.shape[0]
    n1 = ec // N_KEYS
    assert n1 % SUBLANES == 0

    @pl.when(j == 0)
    def _():
        xb = x1_ref[...].astype(BF16)
        xb_ref[...] = xb
        acc_ref[...] = jnp.zeros(acc_ref.shape, F32)
        for c in range(0, q_ref.shape[1], 512):
            q_ref[:, c:c + 512] = jnp.dot(xb, wpq_ref[:, c:c + 512],
                                          preferred_element_type=F32).astype(BF16)
        for h in range(PEER_HEADS):
            s = []
            for half in range(2):
                r = 2 * h + half
                s.append(lax.dot_general(keys_ref[r], q_ref[:, r * N_KEYS:(r + 1) * N_KEYS], _NT,
                                         preferred_element_type=F32))
            for lt in range(0, tm, 2 * LANES):
                cols = slice(lt, lt + 2 * LANES)
                e1n, e2, psi, code2 = _peer_route(s[0][:, cols], s[1][:, cols])
                e1_ref[h, :, cols] = e1n
                psi_ref[h, :, cols] = psi
                e2_ref[h, :, cols] = e2.astype(BF16)
                code_ref[h, :, cols] = code2.astype(BF16)

    def packed_row(tile, ii):
        r16 = jnp.broadcast_to(tile[ii:ii + 1, :], (2 * SUBLANES, tile.shape[1])).astype(BF16)
        return jnp.concatenate([r16] * (N_KEYS // (2 * SUBLANES)), axis=0)

    ht = lax.dot_general(u_ref[...], xb_ref[...], _NT, preferred_element_type=F32)
    base = pl.multiple_of(j * n1, SUBLANES)
    for ii in range(n1):
        rows = slice(ii * N_KEYS, (ii + 1) * N_KEYS)
        for lc in range(0, tm, lane_chunk):
            cols = slice(lc, lc + lane_chunk)
            w = None
            grp = pl.multiple_of(base + (ii // SUBLANES) * SUBLANES, SUBLANES)
            for h in range(PEER_HEADS):
                e1b = packed_row(e1_ref[h, pl.ds(grp, SUBLANES), cols], ii % SUBLANES)
                psib = packed_row(psi_ref[h, pl.ds(grp, SUBLANES), cols], ii % SUBLANES)
                p = e1b * e2_ref[h, :, cols]
                t = jnp.where(code_ref[h, :, cols] > psib, p, jnp.zeros_like(p))
                w = t if w is None else w + t
            hblk = ht[rows, cols]
            act = hblk * (1.0 + lax.erf(hblk))
            wa_ref[rows, cols] = w * act.astype(BF16)
    acc_ref[...] += jnp.dot(vt_ref[...], wa_ref[...], preferred_element_type=F32)

    @pl.when(j == pl.num_programs(1) - 1)
    def _():
        z = alpha * x1_ref[...] + acc_ref[...].T
        o_ref[...] = _layer_norm(z, g_ref[...], b_ref[...])


def _peer(x1, wpq, keys, u, vt, g, b, alpha, tm, ec, lane_chunk):
    t, d = x1.shape
    kern = functools.partial(_peer_kernel, alpha=alpha, ec=ec, lane_chunk=lane_chunk)
    n_chunks = N_EXPERTS // ec
    return pl.pallas_call(
        kern,
        out_shape=jax.ShapeDtypeStruct((t, d), F32),
        grid=(t // tm, n_chunks),
        in_specs=[
            pl.BlockSpec((tm, d), lambda i, j: (i, 0)),
            pl.BlockSpec(wpq.shape, lambda i, j: (0, 0)),
            pl.BlockSpec(keys.shape, lambda i, j: (0, 0, 0)),
            pl.BlockSpec((ec, d), lambda i, j: (j, 0)),
            pl.BlockSpec((d, ec), lambda i, j: (0, j)),
            pl.BlockSpec((1, d), lambda i, j: (0, 0)),
            pl.BlockSpec((1, d), lambda i, j: (0, 0)),
        ],
        out_specs=pl.BlockSpec((tm, d), lambda i, j: (i, 0)),
        scratch_shapes=[
            pltpu.VMEM((tm, d), BF16),
            pltpu.VMEM((tm, wpq.shape[1]), BF16),
            pltpu.VMEM((PEER_HEADS, N_KEYS, tm), F32),
            pltpu.VMEM((PEER_HEADS, N_KEYS, tm), F32),
            pltpu.VMEM((PEER_HEADS, N_KEYS, tm), BF16),
            pltpu.VMEM((PEER_HEADS, N_KEYS, tm), BF16),
            pltpu.VMEM((d, tm), F32),
            pltpu.VMEM((ec, tm), BF16),
        ],
        compiler_params=pltpu.CompilerParams(
            dimension_semantics=("parallel", "arbitrary"), vmem_limit_bytes=VMEM_LIMIT),
        name="peer",
    )(x1, wpq, keys, u, vt, g, b)


def _prep_weights(depth, l, w_in, w_mem_kv, lam_q1, lam_k1, lam_q2, lam_k2, subln_g, sink, w_gate,
                  b_gate, w_pa, w_pb, w_pc, w_o, ln1_g, ln1_b, w_pq, sub_keys, peer_u, peer_v,
                  ln2_g, ln2_b):
    w = w_in[l]
    qb_perm = np.concatenate([np.arange(64) + 64 * hq for g in range(4) for hq in (g, g + 4)])
    qa = w[:, 0:512] * (0.125 * math.log2(math.e))
    ka, va = w[:, 512:1024], w[:, 1024:1536]
    qb = w[:, 1536:2048][:, qb_perm] * 0.125
    kb, vb, qc = w[:, 2048:2176], w[:, 2176:2304], w[:, 2304:2816]
    w_proj = jnp.concatenate([qa, ka, va, qb, qc, kb, vb], axis=1).astype(BF16)
    lamp = jnp.stack([lam_q1[l], lam_k1[l], lam_q2[l], lam_k2[l]]).astype(F32)
    row = lambda a: a.astype(F32).reshape(1, -1)
    return dict(
        w_proj=w_proj,
        w_mem_kv=w_mem_kv[l].astype(BF16),
        lamp=lamp,
        subln_g=row(subln_g[l]),
        sink=sink[l].astype(F32),
        w_gate=w_gate[l].astype(BF16),
        b_gate=row(b_gate[l]),
        w_pa=w_pa[l].astype(BF16),
        w_pb=w_pb[l][qb_perm].astype(BF16),
        w_pc=w_pc[l].astype(BF16),
        w_o=w_o[l].astype(BF16),
        ln1_g=row(ln1_g[l]), ln1_b=row(ln1_b[l]),
        w_pq=w_pq[l].astype(BF16),
        keys=sub_keys[l].reshape(2 * PEER_HEADS, N_KEYS, N_KEYS).astype(BF16),
        peer_u=(peer_u[l] * (2.0 ** -0.5)).astype(BF16),
        peer_vt=peer_v[l].T.astype(BF16),
        ln2_g=row(ln2_g[l]), ln2_b=row(ln2_b[l]),
        lam_init=0.8 - 0.6 * math.exp(-0.3 * l),
        alpha=(2.0 * depth) ** 0.25,
    )


def _tile(n, pref):
    t = min(n, pref)
    assert n % t == 0, (n, t)
    return t


def _encoder_layer(x, mem, p):
    b, s, d = x.shape
    t = b * s
    x2d = x.reshape(t, d)
    proj = _proj(x2d, p["w_proj"], _tile(t, 512)).reshape(b, s, PROJ_COLS)
    slopes_a = jnp.asarray(2.0 ** (-8.0 * np.arange(1, DA_HEADS + 1) / DA_HEADS), F32)
    slope_tab = jnp.asarray(_slope_pieces(2.0 ** (-8.0 * np.arange(1, DA_HEADS + 1) / DA_HEADS)))
    slopes_b = jnp.asarray(2.0 ** (-8.0 * np.arange(1, WA_HEADS + 1) / WA_HEADS), F32)
    oa = _attn_a(proj, slopes_a, slope_tab, p["lamp"], p["subln_g"], p["lam_init"], 256, 4, 512)
    ob = _attn_b(proj, slopes_b, p["sink"], _tile(s, 256))
    oc = _attn_c(proj, mem, p["w_mem_kv"], _tile(s, 512))
    x1 = _merge(x2d, oa.reshape(t, BRANCH_W), ob.reshape(t, BRANCH_W), oc.reshape(t, BRANCH_W),
                p["w_gate"], p["b_gate"], p["w_pa"], p["w_pb"], p["w_pc"], p["w_o"],
                p["ln1_g"], p["ln1_b"], p["alpha"], _tile(t, 512))
    y = _peer(x1, p["w_pq"], p["keys"], p["peer_u"], p["peer_vt"], p["ln2_g"], p["ln2_b"],
              p["alpha"], _tile(t, 512), 2048, 256)
    return y.reshape(b, s, d)


def kernel(x_prompt, x_sample, mem_prompt, mem_sample, w_in, w_mem_kv, lam_q1, lam_k1, lam_q2,
           lam_k2, subln_g, sink, w_gate, b_gate, w_pa, w_pb, w_pc, w_o, ln1_g, ln1_b, w_pq,
           sub_keys, peer_u, peer_v, ln2_g, ln2_b):
    depth = w_in.shape[0]
    y_prompt, y_sample = x_prompt, x_sample
    for l in range(depth):
        p = _prep_weights(depth, l, w_in, w_mem_kv, lam_q1, lam_k1, lam_q2, lam_k2, subln_g, sink,
                          w_gate, b_gate, w_pa, w_pb, w_pc, w_o, ln1_g, ln1_b, w_pq, sub_keys,
                          peer_u, peer_v, ln2_g, ln2_b)
        y_prompt = _encoder_layer(y_prompt, mem_prompt, p)
        y_sample = _encoder_layer(y_sample, mem_sample, p)
    return (y_prompt, y_sample)
```

```python
import functools
import math

import jax
import jax.numpy as jnp
import numpy as np
from jax import lax
from jax.experimental import pallas as pl
from jax.experimental.pallas import tpu as pltpu

F32 = jnp.float32
BF16 = jnp.bfloat16

D_MODEL = 1024
N_MEM = 256
BLOCK = 128
DA_HEADS = 4
WA_HEADS = 8
MEM_HEADS = 4
MEM_DH = 128
BRANCH_W = 512
PEER_HEADS = 8
N_KEYS = 128
N_EXPERTS = N_KEYS * N_KEYS
PEER_TOPK = 16
LN_EPS = 1e-5
NEG = -1e30
LANES = 128
SUBLANES = 8

COL_QA, COL_KA, COL_VA, COL_QB, COL_QC, COL_KB, COL_VB = 0, 512, 1024, 1536, 2048, 2560, 2688
PROJ_COLS = 2816

VMEM_LIMIT = 56 * 1024 * 1024

_NT = (((1,), (1,)), ((), ()))


def _layer_norm(z, g, b):
    mu = jnp.mean(z, axis=-1, keepdims=True)
    zc = z - mu
    var = jnp.mean(zc * zc, axis=-1, keepdims=True)
    return zc * lax.rsqrt(var + LN_EPS) * g + b


def _proj_kernel(x_ref, w_ref, o_ref, *, n_chunk):
    xb = x_ref[...].astype(BF16)
    for c in range(0, o_ref.shape[-1], n_chunk):
        o_ref[:, c:c + n_chunk] = jnp.dot(
            xb, w_ref[:, c:c + n_chunk], preferred_element_type=F32).astype(BF16)


def _proj(x2d, w, tm):
    t, d = x2d.shape
    n = w.shape[1]
    return pl.pallas_call(
        functools.partial(_proj_kernel, n_chunk=256),
        out_shape=jax.ShapeDtypeStruct((t, n), BF16),
        grid=(t // tm,),
        in_specs=[pl.BlockSpec((tm, d), lambda i: (i, 0)),
                  pl.BlockSpec((d, n), lambda i: (0, 0))],
        out_specs=pl.BlockSpec((tm, n), lambda i: (i, 0)),
        compiler_params=pltpu.CompilerParams(
            dimension_semantics=("parallel",), vmem_limit_bytes=VMEM_LIMIT),
        name="proj",
    )(x2d, w)


def _attn_a_kernel(lo_ref, hi_ref, slopes_ref, lamp_ref, g_ref, q_ref, k_ref, v_ref, o_ref,
                   gm_ref, m_ref, l_ref, acc_ref, *, tq, n_streams, tk, lam_init):
    bb = pl.program_id(0)
    h = pl.program_id(1)
    i = pl.program_id(2)
    tg = n_streams * tq
    mid = tg // tk
    flat = (bb * pl.num_programs(1) + h) * pl.num_programs(2) + i
    lo = lo_ref[flat]
    hi = hi_ref[flat]
    slope2 = slopes_ref[4 * h]
    pieces = [slopes_ref[4 * h + 1 + n] for n in range(3)]
    reps = tk // LANES
    assert tq <= 256 and tk % 256 == 0

    gm_ref[...] = (lax.broadcasted_iota(jnp.int32, (tq, tk), 0)
                   - lax.broadcasted_iota(jnp.int32, (tq, tk), 1)).astype(F32) * slope2
    m_ref[...] = jnp.full(m_ref.shape, -jnp.inf, F32)
    l_ref[...] = jnp.zeros(l_ref.shape, F32)
    acc_ref[...] = jnp.zeros(acc_ref.shape, F32)

    lane = lax.broadcasted_iota(jnp.int32, (tq, LANES), 1)
    row = lax.broadcasted_iota(jnp.int32, (tq, LANES), 0).astype(F32)
    qf = jnp.where(lane < 3, row, 0.0)
    klane = lax.broadcasted_iota(jnp.int32, (tk, LANES), 1)
    col = lax.broadcasted_iota(jnp.int32, (tk, LANES), 0)
    col_lo = (col & 255).astype(F32)
    kf = jnp.where((klane >= 3) & (klane < 6), col_lo,
                   jnp.where((klane >= 6) & (klane < 9), col.astype(F32) - col_lo, 0.0))
    for n in range(3):
        qf = jnp.where((lane == 3 + n) | (lane == 6 + n), -pieces[n], qf)
        kf = jnp.where(klane == n, pieces[n], kf)
    qf = jnp.concatenate([qf, qf], axis=0)
    kf = kf.astype(BF16)
    ones_col = jnp.where(klane == 0, 1.0, 0.0).astype(BF16)

    q2, q_right, q_left = [], [], []
    for st in range(n_streams):
        q = q_ref[st * tq:(st + 1) * tq, :]
        zero = jnp.zeros_like(q)
        q2.append(jnp.concatenate([jnp.where(lane < 64, q, zero),
                                   jnp.where(lane >= 64, q, zero)], axis=0))
        q_right.append(jnp.concatenate([q2[st], qf.astype(BF16)], axis=1))
        q_left.append(jnp.concatenate([q2[st], (-qf).astype(BF16)], axis=1))

    def step(st, lhs, rhs, v_aug, bias, shift):
        x = lax.dot_general(lhs, rhs, _NT, preferred_element_type=F32)
        if bias is not None:
            x = x + jnp.concatenate([bias, bias], axis=0)
        m_prev = m_ref[st]
        m_next = jnp.maximum(m_prev, jnp.max(x, axis=1, keepdims=True) - shift)
        alpha = jnp.exp2(m_prev - m_next)
        sub = m_next + shift
        p = jnp.exp2(x - jnp.concatenate([sub] * reps, axis=1))
        pv = jnp.dot(p.astype(BF16), v_aug, preferred_element_type=F32)
        l_ref[st] = alpha * l_ref[st] + pv[:, LANES:]
        acc_ref[st] = alpha * acc_ref[st] + pv[:, :LANES]
        m_ref[st] = m_next

    def tiles(j):
        ks = pl.multiple_of(j * tk, tk)
        kt = k_ref[pl.ds(ks, tk), :]
        v_aug = jnp.concatenate([v_ref[pl.ds(ks, tk), :], ones_col], axis=1)
        return kt, jnp.concatenate([kt, kf], axis=1), v_aug

    def left(j, carry):
        _, k_aug, v_aug = tiles(j)
        base = lax.convert_element_type(i * tg - j * tk, F32)
        for st in range(n_streams):
            step(st, q_left[st], k_aug, v_aug, None, slope2 * (base + st * tq))
        return carry

    def right(j, carry):
        _, k_aug, v_aug = tiles(j)
        base = lax.convert_element_type(j * tk - i * tg, F32)
        for st in range(n_streams):
            step(st, q_right[st], k_aug, v_aug, None, slope2 * (base - st * tq))
        return carry

    lax.fori_loop(lo, i * mid, left, 0)
    for mj in range(mid):
        kt, k_aug, v_aug = tiles(i * mid + mj)
        for st in range(n_streams):
            off = st * tq - mj * tk
            if off - (tk - 1) >= 0:
                step(st, q_left[st], k_aug, v_aug, None, slope2 * off)
            elif off + (tq - 1) <= 0:
                step(st, q_right[st], k_aug, v_aug, None, slope2 * (-off))
            else:
                step(st, q2[st], kt, v_aug, -jnp.abs(gm_ref[...] + slope2 * off), 0.0)
    lax.fori_loop((i + 1) * mid, hi, right, 0)

    lamp = lamp_ref[...]
    lam = (jnp.exp(jnp.sum(lamp[0:1] * lamp[1:2], axis=1, keepdims=True))
           - jnp.exp(jnp.sum(lamp[2:3] * lamp[3:4], axis=1, keepdims=True)) + lam_init)
    for st in range(n_streams):
        o = acc_ref[st] / jnp.sum(l_ref[st], axis=1, keepdims=True)
        o = o[:tq] - lam * o[tq:]
        ms = jnp.mean(o * o, axis=-1, keepdims=True)
        y = o * lax.rsqrt(ms + LN_EPS) * g_ref[...] * (1.0 - lam_init)
        o_ref[st * tq:(st + 1) * tq, :] = y.astype(o_ref.dtype)


UNDERFLOW_LOG2 = 151.0


def _attn_a_bounds(proj, slopes, tg, tk):
    b, s, _ = proj.shape
    ni, nk, mid = s // tg, s // tk, tg // tk
    q = proj[:, :, COL_QA:COL_QA + 512].astype(F32)
    k = proj[:, :, COL_KA:COL_KA + 512].astype(F32)
    group = jnp.asarray(np.repeat(np.eye(2 * DA_HEADS, dtype=np.float32), 64, axis=0))

    def group_sums(a):
        return jnp.einsum("bsc,cg->bsg", a, group, precision=lax.Precision.HIGHEST,
                          preferred_element_type=F32)

    qn = jnp.sqrt(group_sums(q * q)).reshape(b, ni, tg, DA_HEADS, 2)
    own = group_sums(q * k).reshape(b, ni, tg, DA_HEADS, 2)
    kmax = jnp.max(jnp.sqrt(group_sums(k * k)), axis=1).reshape(b, DA_HEADS, 2)
    upper = jnp.max(qn, axis=2) * kmax[:, None] * 1.001
    slack = jnp.max(upper - jnp.min(own, axis=2), axis=-1) + UNDERFLOW_LOG2
    reach = slack / (slopes * math.log2(math.e))
    reach = jnp.minimum(reach, 4.0 * s)
    i0 = (jnp.arange(ni, dtype=F32) * tg)[None, :, None]
    lo = jnp.ceil((i0 + 1.0 - reach) / tk - 1.0)
    lo = jnp.clip(lo, 0, jnp.arange(ni, dtype=F32)[None, :, None] * mid)
    hi = jnp.floor((reach + i0 + tg - 1.0) / tk) + 1.0
    hi = jnp.clip(hi, (jnp.arange(ni, dtype=F32)[None, :, None] + 1.0) * mid, nk)
    flat = lambda a: a.astype(jnp.int32).transpose(0, 2, 1).reshape(-1)
    return flat(lo), flat(hi)


def _slope_pieces(slopes):
    rows = []
    for s in slopes:
        s2 = np.float32(np.float32(s) * np.float32(math.log2(math.e)))
        hi = np.float32(np.asarray(s2, dtype=jnp.bfloat16))
        mid = np.float32(np.asarray(np.float32(s2 - hi), dtype=jnp.bfloat16))
        lo = np.float32(np.asarray(np.float32(s2 - hi - mid), dtype=jnp.bfloat16))
        assert np.float32(np.float32(hi + mid) + lo) == s2
        rows += [s2, hi, mid, lo]
    return np.asarray(rows, np.float32)


def _attn_a(proj, slopes, slope_tab, lamp, subln_g, lam_init, tq, n_streams, tk):
    b, s, _ = proj.shape
    tg = tq * n_streams
    assert tg % tk == 0 and s % tg == 0
    lo, hi = _attn_a_bounds(proj, slopes, tg, tk)
    kern = functools.partial(_attn_a_kernel, tq=tq, n_streams=n_streams, tk=tk, lam_init=lam_init)
    grid_spec = pltpu.PrefetchScalarGridSpec(
        num_scalar_prefetch=2,
        grid=(b, DA_HEADS, s // tg),
        in_specs=[
            pl.BlockSpec(memory_space=pltpu.SMEM),
            pl.BlockSpec((4, 64), lambda bb, h, i, lo, hi: (0, 0)),
            pl.BlockSpec((1, LANES), lambda bb, h, i, lo, hi: (0, 0)),
            pl.BlockSpec((None, tg, LANES), lambda bb, h, i, lo, hi: (bb, i, COL_QA // LANES + h)),
            pl.BlockSpec((None, s, LANES), lambda bb, h, i, lo, hi: (bb, 0, COL_KA // LANES + h)),
            pl.BlockSpec((None, s, LANES), lambda bb, h, i, lo, hi: (bb, 0, COL_VA // LANES + h)),
        ],
        out_specs=pl.BlockSpec((None, tg, LANES), lambda bb, h, i, lo, hi: (bb, i, h)),
        scratch_shapes=[pltpu.VMEM((tq, tk), F32),
                        pltpu.VMEM((n_streams, 2 * tq, LANES), F32),
                        pltpu.VMEM((n_streams, 2 * tq, LANES), F32),
                        pltpu.VMEM((n_streams, 2 * tq, LANES), F32)],
    )
    return pl.pallas_call(
        kern,
        out_shape=jax.ShapeDtypeStruct((b, s, BRANCH_W), BF16),
        grid_spec=grid_spec,
        compiler_params=pltpu.CompilerParams(
            dimension_semantics=("parallel", "parallel", "arbitrary"),
            vmem_limit_bytes=VMEM_LIMIT),
        name="attn_a",
    )(lo, hi, slope_tab, lamp, subln_g, proj, proj, proj)


def _attn_b_kernel(slopes_ref, sink_ref, q_ref, kp_ref, kc_ref, kn_ref, vp_ref, vc_ref, vn_ref,
                   o_ref, *, tq, seq):
    i = pl.program_id(1)
    kfull = jnp.concatenate([kp_ref[...], kc_ref[...], kn_ref[...]], axis=0)
    vfull = jnp.concatenate([vp_ref[...], vc_ref[...], vn_ref[...]], axis=0)
    band = 3 * BLOCK
    r = lax.broadcasted_iota(jnp.int32, (BLOCK, band), 0)
    c = lax.broadcasted_iota(jnp.int32, (BLOCK, band), 1)
    rel_i = jnp.abs(r + BLOCK - c)
    rel = rel_i.astype(F32)
    lane = lax.broadcasted_iota(jnp.int32, (BLOCK, LANES), 1)
    lo_half = lane < 64

    for sub in range(tq // BLOCK):
        q_start = i * tq + sub * BLOCK
        valid = ((rel_i <= BLOCK) & (c >= BLOCK - q_start) & (c < seq + BLOCK - q_start))
        kband = kfull[sub * BLOCK: sub * BLOCK + band]
        vband = vfull[sub * BLOCK: sub * BLOCK + band]
        qblk = q_ref[sub * BLOCK:(sub + 1) * BLOCK, :]
        parts = []
        for g in range(4):
            qg = qblk[:, g * LANES:(g + 1) * LANES]
            zero = jnp.zeros_like(qg)
            parts.append(jnp.where(lo_half, qg, zero))
            parts.append(jnp.where(lo_half, zero, qg))
        qs = jnp.concatenate(parts, axis=0)
        s_all = lax.dot_general(qs, kband, _NT, preferred_element_type=F32)
        ps, invs = [], []
        for n in range(8):
            hq = (n // 2) + 4 * (n % 2)
            s = s_all[n * BLOCK:(n + 1) * BLOCK]
            s = jnp.where(valid, s - slopes_ref[hq] * rel, NEG)
            sk = sink_ref[hq]
            m = jnp.maximum(jnp.max(s, axis=1, keepdims=True), sk)
            e = jnp.exp(s - m)
            den = jnp.sum(e, axis=1, keepdims=True) + jnp.exp(sk - m)
            ps.append(e.astype(BF16))
            invs.append(1.0 / den)
        p_all = jnp.concatenate(ps, axis=0)
        o_all = jnp.dot(p_all, vband, preferred_element_type=F32)
        for g in range(4):
            o_lo = o_all[(2 * g) * BLOCK:(2 * g + 1) * BLOCK] * invs[2 * g]
            o_hi = o_all[(2 * g + 1) * BLOCK:(2 * g + 2) * BLOCK] * invs[2 * g + 1]
            o_ref[sub * BLOCK:(sub + 1) * BLOCK, g * LANES:(g + 1) * LANES] = jnp.where(
                lo_half, o_lo, o_hi).astype(o_ref.dtype)


def _attn_b(proj, slopes, sink, tq):
    b, s, _ = proj.shape
    nb = s // BLOCK
    r = tq // BLOCK
    kcol, vcol = COL_KB // LANES, COL_VB // LANES

    def prev_map(col):
        return lambda bb, i: (bb, jnp.maximum(i * r - 1, 0), col)

    def cur_map(col):
        return lambda bb, i: (bb, i, col)

    def next_map(col):
        return lambda bb, i: (bb, jnp.minimum(i * r + r, nb - 1), col)

    return pl.pallas_call(
        functools.partial(_attn_b_kernel, tq=tq, seq=s),
        out_shape=jax.ShapeDtypeStruct((b, s, BRANCH_W), BF16),
        grid=(b, s // tq),
        in_specs=[
            pl.BlockSpec(memory_space=pltpu.SMEM),
            pl.BlockSpec(memory_space=pltpu.SMEM),
            pl.BlockSpec((None, tq, BRANCH_W), lambda bb, i: (bb, i, COL_QB // BRANCH_W)),
            pl.BlockSpec((None, BLOCK, LANES), prev_map(kcol)),
            pl.BlockSpec((None, tq, LANES), cur_map(kcol)),
            pl.BlockSpec((None, BLOCK, LANES), next_map(kcol)),
            pl.BlockSpec((None, BLOCK, LANES), prev_map(vcol)),
            pl.BlockSpec((None, tq, LANES), cur_map(vcol)),
            pl.BlockSpec((None, BLOCK, LANES), next_map(vcol)),
        ],
        out_specs=pl.BlockSpec((None, tq, BRANCH_W), lambda bb, i: (bb, i, 0)),
        compiler_params=pltpu.CompilerParams(
            dimension_semantics=("parallel", "parallel"), vmem_limit_bytes=VMEM_LIMIT),
        name="attn_b",
    )(slopes, sink, proj, proj, proj, proj, proj, proj, proj)


def _attn_c_kernel(q_ref, mem_ref, wkv_ref, o_ref, mk_ref, mv_ref):
    @pl.when(pl.program_id(1) == 0)
    def _():
        kv = jnp.dot(mem_ref[...].astype(BF16), wkv_ref[...], preferred_element_type=F32)
        mk_ref[...] = kv[:, :BRANCH_W].astype(BF16)
        mv_ref[...] = kv[:, BRANCH_W:].astype(BF16)

    scale = MEM_DH ** -0.5
    for h in range(MEM_HEADS):
        cols = slice(h * MEM_DH, (h + 1) * MEM_DH)
        s = lax.dot_general(q_ref[:, cols], mk_ref[:, cols], _NT,
                            preferred_element_type=F32) * scale
        m = jnp.max(s, axis=1, keepdims=True)
        e = jnp.exp(s - m)
        inv = 1.0 / jnp.sum(e, axis=1, keepdims=True)
        o = jnp.dot(e.astype(BF16), mv_ref[:, cols], preferred_element_type=F32)
        o_ref[:, cols] = (o * inv).astype(o_ref.dtype)


def _attn_c(proj, mem, wkv, tq):
    b, s, _ = proj.shape
    return pl.pallas_call(
        _attn_c_kernel,
        out_shape=jax.ShapeDtypeStruct((b, s, BRANCH_W), BF16),
        grid=(b, s // tq),
        in_specs=[
            pl.BlockSpec((None, tq, BRANCH_W), lambda bb, i: (bb, i, COL_QC // BRANCH_W)),
            pl.BlockSpec((None, N_MEM, D_MODEL), lambda bb, i: (bb, 0, 0)),
            pl.BlockSpec((D_MODEL, 2 * BRANCH_W), lambda bb, i: (0, 0)),
        ],
        out_specs=pl.BlockSpec((None, tq, BRANCH_W), lambda bb, i: (bb, i, 0)),
        scratch_shapes=[pltpu.VMEM((N_MEM, BRANCH_W), BF16), pltpu.VMEM((N_MEM, BRANCH_W), BF16)],
        compiler_params=pltpu.CompilerParams(
            dimension_semantics=("parallel", "arbitrary"), vmem_limit_bytes=VMEM_LIMIT),
        name="attn_c",
    )(proj, mem, wkv)


def _merge_kernel(x_ref, oa_ref, ob_ref, oc_ref, wg_ref, bg_ref, wpa_ref, wpb_ref, wpc_ref,
                  wo_ref, g_ref, b_ref, o_ref, *, alpha):
    x = x_ref[...]
    xb = x.astype(BF16)
    merged = None
    for n, (br_ref, wp_ref) in enumerate(((oa_ref, wpa_ref), (ob_ref, wpb_ref), (oc_ref, wpc_ref))):
        cols = slice(n * D_MODEL, (n + 1) * D_MODEL)
        gate = jax.nn.sigmoid(
            jnp.dot(xb, wg_ref[:, cols], preferred_element_type=F32) + bg_ref[:, cols])
        term = gate * jnp.dot(br_ref[...], wp_ref[...], preferred_element_type=F32)
        merged = term if merged is None else merged + term
    y = jnp.dot(merged.astype(BF16), wo_ref[...], preferred_element_type=F32)
    o_ref[...] = _layer_norm(alpha * x + y, g_ref[...], b_ref[...])


def _merge(x2d, oa, ob, oc, wg, bg, wpa, wpb, wpc, wo, g, b, alpha, tm):
    t, d = x2d.shape
    const = lambda i: (0, 0)
    row = lambda i: (i, 0)
    return pl.pallas_call(
        functools.partial(_merge_kernel, alpha=alpha),
        out_shape=jax.ShapeDtypeStruct((t, d), F32),
        grid=(t // tm,),
        in_specs=[
            pl.BlockSpec((tm, d), row),
            pl.BlockSpec((tm, BRANCH_W), row),
            pl.BlockSpec((tm, BRANCH_W), row),
            pl.BlockSpec((tm, BRANCH_W), row),
            pl.BlockSpec((d, 3 * d), const),
            pl.BlockSpec((1, 3 * d), const),
            pl.BlockSpec((BRANCH_W, d), const),
            pl.BlockSpec((BRANCH_W, d), const),
            pl.BlockSpec((BRANCH_W, d), const),
            pl.BlockSpec((d, d), const),
            pl.BlockSpec((1, d), const),
            pl.BlockSpec((1, d), const),
        ],
        out_specs=pl.BlockSpec((tm, d), row),
        compiler_params=pltpu.CompilerParams(
            dimension_semantics=("parallel",), vmem_limit_bytes=VMEM_LIMIT),
        name="merge",
    )(x2d, oa, ob, oc, wg, bg, wpa, wpb, wpc, wo, g, b)


def _sort_network(n):
    pairs = []

    def merge(lo, hi, r):
        step = r * 2
        if step < hi - lo:
            merge(lo, hi, step)
            merge(lo + r, hi, step)
            pairs.extend((k, k + r) for k in range(lo + r, hi - r, step))
        else:
            pairs.append((lo, lo + r))

    def sort(lo, hi):
        if hi - lo >= 1:
            mid = lo + (hi - lo) // 2
            sort(lo, mid)
            sort(mid + 1, hi)
            merge(lo, hi, 1)

    sort(0, n - 1)
    return pairs


_SORT16 = _sort_network(PEER_TOPK)


def _top16_desc(slabs):
    v = list(slabs)
    for a, b in _SORT16:
        hi, lo = jnp.maximum(v[a], v[b]), jnp.minimum(v[a], v[b])
        v[a], v[b] = hi, lo
    for shift in (4, 2, 1):
        v = [jnp.maximum(v[k], pltpu.roll(v[PEER_TOPK - 1 - k], shift, 0))
             for k in range(PEER_TOPK)]
        step = PEER_TOPK // 2
        while step >= 1:
            for k in range(PEER_TOPK):
                if k & step == 0:
                    hi, lo = jnp.maximum(v[k], v[k + step]), jnp.minimum(v[k], v[k + step])
                    v[k], v[k + step] = hi, lo
            step //= 2
    return v


def _peer_route(s0, s1):
    n = s0.shape[1]
    top_a = _top16_desc([s0[k * SUBLANES:(k + 1) * SUBLANES] for k in range(N_KEYS // SUBLANES)])
    top_b = _top16_desc([s1[k * SUBLANES:(k + 1) * SUBLANES] for k in range(N_KEYS // SUBLANES)])
    a0, b0 = top_a[0], top_b[0]
    ea = [jnp.exp(t - a0) for t in top_a]
    eb = [jnp.exp(t - b0) for t in top_b]
    sub = lax.broadcasted_iota(jnp.int32, (SUBLANES, n), 0)
    first4 = sub < 4

    def candidates(ea_list):
        col = ea_list[0]
        for s in range(1, 4):
            col = jnp.where(sub == s, ea_list[s], col)
        for s in range(4, 8):
            col = jnp.where(sub == s, eb[s - 4], col)
        out = []
        for v in range(PEER_TOPK):
            c = col * jnp.where(first4, eb[v], ea_list[v])
            if v < 4:
                c = jnp.where(first4, c, -1.0)
            out.append(c)
        return out

    cand = candidates(ea)
    best = _top16_desc(cand)
    z = best[0]
    for t in best[1:]:
        z = z + t
    inv_z = (2.0 ** -0.5) / z
    theta = best[PEER_TOPK - 1]
    ean = [t * inv_z for t in ea]
    cand_n = candidates(ean)
    thr = None
    for c, cn in zip(cand, cand_n):
        t = jnp.where(c >= theta, cn, jnp.inf)
        thr = t if thr is None else jnp.minimum(thr, t)
    for shift in (4, 2, 1):
        thr = jnp.minimum(thr, pltpu.roll(thr, shift, 0))
    thr = thr[0:1]
    e1n = jnp.exp(s0 - a0[0:1]) * inv_z[0:1]
    e2 = jnp.exp(s1 - b0[0:1])
    psi = jnp.full(s0.shape, float(PEER_TOPK), F32)
    for r in range(PEER_TOPK):
        psi = jnp.where(e1n * eb[r][0:1] >= thr, float(PEER_TOPK - 1 - r), psi)
    code2 = jnp.zeros(s1.shape, F32)
    for r in reversed(range(PEER_TOPK)):
        code2 = jnp.where(s1 >= top_b[r][0:1], float(PEER_TOPK - r), code2)
    return e1n, e2, psi, code2


def _peer_kernel(x1_ref, wpq_ref, keys_ref, u_ref, vt_ref, g_ref, b_ref, o_ref,
                 xb_ref, q_ref, e1_ref, psi_ref, e2_ref, code_ref, acc_ref, wa_ref, *,
                 alpha, ec, lane_chunk):
    j = pl.program_id(1)
    tm = x1_ref.shape[0]
    n1 = ec // N_KEYS
    assert n1 % SUBLANES == 0

    @pl.when(j == 0)
    def _():
        xb = x1_ref[...].astype(BF16)
        xb_ref[...] = xb
        acc_ref[...] = jnp.zeros(acc_ref.shape, F32)
        for c in range(0, q_ref.shape[1], 512):
            q_ref[:, c:c + 512] = jnp.dot(xb, wpq_ref[:, c:c + 512],
                                          preferred_element_type=F32).astype(BF16)
        for h in range(PEER_HEADS):
            s = []
            for half in range(2):
                r = 2 * h + half
                s.append(lax.dot_general(keys_ref[r], q_ref[:, r * N_KEYS:(r + 1) * N_KEYS], _NT,
                                         preferred_element_type=F32))
            for lt in range(0, tm, 2 * LANES):
                cols = slice(lt, lt + 2 * LANES)
                e1n, e2, psi, code2 = _peer_route(s[0][:, cols], s[1][:, cols])
                e1_ref[h, :, cols] = e1n
                psi_ref[h, :, cols] = psi
                e2_ref[h, :, cols] = e2.astype(BF16)
                code_ref[h, :, cols] = code2.astype(BF16)

    def packed_row(tile, ii):
        r16 = jnp.broadcast_to(tile[ii:ii + 1, :], (2 * SUBLANES, tile.shape[1])).astype(BF16)
        return jnp.concatenate([r16] * (N_KEYS // (2 * SUBLANES)), axis=0)

    ht = lax.dot_general(u_ref[...], xb_ref[...], _NT, preferred_element_type=F32)
    base = pl.multiple_of(j * n1, SUBLANES)
    for ii in range(n1):
        rows = slice(ii * N_KEYS, (ii + 1) * N_KEYS)
        for lc in range(0, tm, lane_chunk):
            cols = slice(lc, lc + lane_chunk)
            w = None
            grp = pl.multiple_of(base + (ii // SUBLANES) * SUBLANES, SUBLANES)
            for h in range(PEER_HEADS):
                e1b = packed_row(e1_ref[h, pl.ds(grp, SUBLANES), cols], ii % SUBLANES)
                psib = packed_row(psi_ref[h, pl.ds(grp, SUBLANES), cols], ii % SUBLANES)
                p = e1b * e2_ref[h, :, cols]
                t = jnp.where(code_ref[h, :, cols] > psib, p, jnp.zeros_like(p))
                w = t if w is None else w + t
            hblk = ht[rows, cols]
            act = hblk * (1.0 + lax.erf(hblk))
            wa_ref[rows, cols] = w * act.astype(BF16)
    acc_ref[...] += jnp.dot(vt_ref[...], wa_ref[...], preferred_element_type=F32)

    @pl.when(j == pl.num_programs(1) - 1)
    def _():
        z = alpha * x1_ref[...] + acc_ref[...].T
        o_ref[...] = _layer_norm(z, g_ref[...], b_ref[...])


def _peer(x1, wpq, keys, u, vt, g, b, alpha, tm, ec, lane_chunk):
    t, d = x1.shape
    kern = functools.partial(_peer_kernel, alpha=alpha, ec=ec, lane_chunk=lane_chunk)
    n_chunks = N_EXPERTS // ec
    return pl.pallas_call(
        kern,
        out_shape=jax.ShapeDtypeStruct((t, d), F32),
        grid=(t // tm, n_chunks),
        in_specs=[
            pl.BlockSpec((tm, d), lambda i, j: (i, 0)),
            pl.BlockSpec(wpq.shape, lambda i, j: (0, 0)),
            pl.BlockSpec(keys.shape, lambda i, j: (0, 0, 0)),
            pl.BlockSpec((ec, d), lambda i, j: (j, 0)),
            pl.BlockSpec((d, ec), lambda i, j: (0, j)),
            pl.BlockSpec((1, d), lambda i, j: (0, 0)),
            pl.BlockSpec((1, d), lambda i, j: (0, 0)),
        ],
        out_specs=pl.BlockSpec((tm, d), lambda i, j: (i, 0)),
        scratch_shapes=[
            pltpu.VMEM((tm, d), BF16),
            pltpu.VMEM((tm, wpq.shape[1]), BF16),
            pltpu.VMEM((PEER_HEADS, N_KEYS, tm), F32),
            pltpu.VMEM((PEER_HEADS, N_KEYS, tm), F32),
            pltpu.VMEM((PEER_HEADS, N_KEYS, tm), BF16),
            pltpu.VMEM((PEER_HEADS, N_KEYS, tm), BF16),
            pltpu.VMEM((d, tm), F32),
            pltpu.VMEM((ec, tm), BF16),
        ],
        compiler_params=pltpu.CompilerParams(
            dimension_semantics=("parallel", "arbitrary"), vmem_limit_bytes=VMEM_LIMIT),
        name="peer",
    )(x1, wpq, keys, u, vt, g, b)


def _prep_weights(depth, l, w_in, w_mem_kv, lam_q1, lam_k1, lam_q2, lam_k2, subln_g, sink, w_gate,
                  b_gate, w_pa, w_pb, w_pc, w_o, ln1_g, ln1_b, w_pq, sub_keys, peer_u, peer_v,
                  ln2_g, ln2_b):
    w = w_in[l]
    qb_perm = np.concatenate([np.arange(64) + 64 * hq for g in range(4) for hq in (g, g + 4)])
    qa = w[:, 0:512] * (0.125 * math.log2(math.e))
    ka, va = w[:, 512:1024], w[:, 1024:1536]
    qb = w[:, 1536:2048][:, qb_perm] * 0.125
    kb, vb, qc = w[:, 2048:2176], w[:, 2176:2304], w[:, 2304:2816]
    w_proj = jnp.concatenate([qa, ka, va, qb, qc, kb, vb], axis=1).astype(BF16)
    lamp = jnp.stack([lam_q1[l], lam_k1[l], lam_q2[l], lam_k2[l]]).astype(F32)
    row = lambda a: a.astype(F32).reshape(1, -1)
    return dict(
        w_proj=w_proj,
        w_mem_kv=w_mem_kv[l].astype(BF16),
        lamp=lamp,
        subln_g=row(subln_g[l]),
        sink=sink[l].astype(F32),
        w_gate=w_gate[l].astype(BF16),
        b_gate=row(b_gate[l]),
        w_pa=w_pa[l].astype(BF16),
        w_pb=w_pb[l][qb_perm].astype(BF16),
        w_pc=w_pc[l].astype(BF16),
        w_o=w_o[l].astype(BF16),
        ln1_g=row(ln1_g[l]), ln1_b=row(ln1_b[l]),
        w_pq=w_pq[l].astype(BF16),
        keys=sub_keys[l].reshape(2 * PEER_HEADS, N_KEYS, N_KEYS).astype(BF16),
        peer_u=(peer_u[l] * (2.0 ** -0.5)).astype(BF16),
        peer_vt=peer_v[l].T.astype(BF16),
        ln2_g=row(ln2_g[l]), ln2_b=row(ln2_b[l]),
        lam_init=0.8 - 0.6 * math.exp(-0.3 * l),
        alpha=(2.0 * depth) ** 0.25,
    )


def _tile(n, pref):
    t = min(n, pref)
    assert n % t == 0, (n, t)
    return t


def _encoder_layer(x, mem, p):
    b, s, d = x.shape
    t = b * s
    x2d = x.reshape(t, d)
    proj = _proj(x2d, p["w_proj"], _tile(t, 512)).reshape(b, s, PROJ_COLS)
    slopes_a = jnp.asarray(2.0 ** (-8.0 * np.arange(1, DA_HEADS + 1) / DA_HEADS), F32)
    slope_tab = jnp.asarray(_slope_pieces(2.0 ** (-8.0 * np.arange(1, DA_HEADS + 1) / DA_HEADS)))
    slopes_b = jnp.asarray(2.0 ** (-8.0 * np.arange(1, WA_HEADS + 1) / WA_HEADS), F32)
    oa = _attn_a(proj, slopes_a, slope_tab, p["lamp"], p["subln_g"], p["lam_init"], 256, 4, 1024)
    ob = _attn_b(proj, slopes_b, p["sink"], _tile(s, 256))
    oc = _attn_c(proj, mem, p["w_mem_kv"], _tile(s, 512))
    x1 = _merge(x2d, oa.reshape(t, BRANCH_W), ob.reshape(t, BRANCH_W), oc.reshape(t, BRANCH_W),
                p["w_gate"], p["b_gate"], p["w_pa"], p["w_pb"], p["w_pc"], p["w_o"],
                p["ln1_g"], p["ln1_b"], p["alpha"], _tile(t, 512))
    y = _peer(x1, p["w_pq"], p["keys"], p["peer_u"], p["peer_vt"], p["ln2_g"], p["ln2_b"],
              p["alpha"], _tile(t, 512), 2048, 256)
    return y.reshape(b, s, d)


def kernel(x_prompt, x_sample, mem_prompt, mem_sample, w_in, w_mem_kv, lam_q1, lam_k1, lam_q2,
           lam_k2, subln_g, sink, w_gate, b_gate, w_pa, w_pb, w_pc, w_o, ln1_g, ln1_b, w_pq,
           sub_keys, peer_u, peer_v, ln2_g, ln2_b):
    depth = w_in.shape[0]
    y_prompt, y_sample = x_prompt, x_sample
    for l in range(depth):
        p = _prep_weights(depth, l, w_in, w_mem_kv, lam_q1, lam_k1, lam_q2, lam_k2, subln_g, sink,
                          w_gate, b_gate, w_pa, w_pb, w_pc, w_o, ln1_g, ln1_b, w_pq, sub_keys,
                          peer_u, peer_v, ln2_g, ln2_b)
        y_prompt = _encoder_layer(y_prompt, mem_prompt, p)
        y_sample = _encoder_layer(y_sample, mem_sample, p)
    return (y_prompt, y_sample)
```

```python
import functools
import math

import jax
import jax.numpy as jnp
import numpy as np
from jax import lax
from jax.experimental import pallas as pl
from jax.experimental.pallas import tpu as pltpu

F32 = jnp.float32
BF16 = jnp.bfloat16

D_MODEL = 1024
N_MEM = 256
BLOCK = 128
DA_HEADS = 4
WA_HEADS = 8
MEM_HEADS = 4
MEM_DH = 128
BRANCH_W = 512
PEER_HEADS = 8
N_KEYS = 128
N_EXPERTS = N_KEYS * N_KEYS
PEER_TOPK = 16
LN_EPS = 1e-5
NEG = -1e30
LANES = 128
SUBLANES = 8

COL_QA, COL_KA, COL_VA, COL_QB, COL_QC, COL_KB, COL_VB = 0, 512, 1024, 1536, 2048, 2560, 2688
PROJ_COLS = 2816

VMEM_LIMIT = 56 * 1024 * 1024

_NT = (((1,), (1,)), ((), ()))


def _layer_norm(z, g, b):
    mu = jnp.mean(z, axis=-1, keepdims=True)
    zc = z - mu
    var = jnp.mean(zc * zc, axis=-1, keepdims=True)
    return zc * lax.rsqrt(var + LN_EPS) * g + b


def _proj_kernel(x_ref, w_ref, o_ref, *, n_chunk):
    xb = x_ref[...].astype(BF16)
    for c in range(0, o_ref.shape[-1], n_chunk):
        o_ref[:, c:c + n_chunk] = jnp.dot(
            xb, w_ref[:, c:c + n_chunk], preferred_element_type=F32).astype(BF16)


def _proj(x2d, w, tm):
    t, d = x2d.shape
    n = w.shape[1]
    return pl.pallas_call(
        functools.partial(_proj_kernel, n_chunk=256),
        out_shape=jax.ShapeDtypeStruct((t, n), BF16),
        grid=(t // tm,),
        in_specs=[pl.BlockSpec((tm, d), lambda i: (i, 0)),
                  pl.BlockSpec((d, n), lambda i: (0, 0))],
        out_specs=pl.BlockSpec((tm, n), lambda i: (i, 0)),
        compiler_params=pltpu.CompilerParams(
            dimension_semantics=("parallel",), vmem_limit_bytes=VMEM_LIMIT),
        name="proj",
    )(x2d, w)


def _attn_a_kernel(lo_ref, hi_ref, slopes_ref, lamp_ref, g_ref, q_ref, k_ref, v_ref, o_ref,
                   gm_ref, m_ref, l_ref, acc_ref, *, tq, n_streams, tk, lam_init):
    bb = pl.program_id(0)
    h = pl.program_id(1)
    i = pl.program_id(2)
    tg = n_streams * tq
    mid = tg // tk
    flat = (bb * pl.num_programs(1) + h) * pl.num_programs(2) + i
    lo = lo_ref[flat]
    hi = hi_ref[flat]
    slope2 = slopes_ref[4 * h]
    pieces = [slopes_ref[4 * h + 1 + n] for n in range(3)]
    reps = tk // LANES
    assert tq <= 256 and tk % 256 == 0

    gm_ref[...] = (lax.broadcasted_iota(jnp.int32, (tq, tk), 0)
                   - lax.broadcasted_iota(jnp.int32, (tq, tk), 1)).astype(F32) * slope2
    m_ref[...] = jnp.full(m_ref.shape, -jnp.inf, F32)
    l_ref[...] = jnp.zeros(l_ref.shape, F32)
    acc_ref[...] = jnp.zeros(acc_ref.shape, F32)

    lane = lax.broadcasted_iota(jnp.int32, (tq, LANES), 1)
    row = lax.broadcasted_iota(jnp.int32, (tq, LANES), 0).astype(F32)
    qf = jnp.where(lane < 3, row, 0.0)
    klane = lax.broadcasted_iota(jnp.int32, (tk, LANES), 1)
    col = lax.broadcasted_iota(jnp.int32, (tk, LANES), 0)
    col_lo = (col & 255).astype(F32)
    kf = jnp.where((klane >= 3) & (klane < 6), col_lo,
                   jnp.where((klane >= 6) & (klane < 9), col.astype(F32) - col_lo, 0.0))
    for n in range(3):
        qf = jnp.where((lane == 3 + n) | (lane == 6 + n), -pieces[n], qf)
        kf = jnp.where(klane == n, pieces[n], kf)
    qf = jnp.concatenate([qf, qf], axis=0)
    kf = kf.astype(BF16)
    ones_col = jnp.where(klane == 0, 1.0, 0.0).astype(BF16)

    q2, q_right, q_left = [], [], []
    for st in range(n_streams):
        q = q_ref[st * tq:(st + 1) * tq, :]
        zero = jnp.zeros_like(q)
        q2.append(jnp.concatenate([jnp.where(lane < 64, q, zero),
                                   jnp.where(lane >= 64, q, zero)], axis=0))
        q_right.append(jnp.concatenate([q2[st], qf.astype(BF16)], axis=1))
        q_left.append(jnp.concatenate([q2[st], (-qf).astype(BF16)], axis=1))

    def step(st, lhs, rhs, v_aug, bias, shift):
        x = lax.dot_general(lhs, rhs, _NT, preferred_element_type=F32)
        if bias is not None:
            x = x + jnp.concatenate([bias, bias], axis=0)
        m_prev = m_ref[st]
        m_next = jnp.maximum(m_prev, jnp.max(x, axis=1, keepdims=True) - shift)
        alpha = jnp.exp2(m_prev - m_next)
        sub = m_next + shift
        p = jnp.exp2(x - jnp.concatenate([sub] * reps, axis=1))
        pv = jnp.dot(p.astype(BF16), v_aug, preferred_element_type=F32)
        l_ref[st] = alpha * l_ref[st] + pv[:, LANES:]
        acc_ref[st] = alpha * acc_ref[st] + pv[:, :LANES]
        m_ref[st] = m_next

    def tiles(j):
        ks = pl.multiple_of(j * tk, tk)
        kt = k_ref[pl.ds(ks, tk), :]
        v_aug = jnp.concatenate([v_ref[pl.ds(ks, tk), :], ones_col], axis=1)
        return kt, jnp.concatenate([kt, kf], axis=1), v_aug

    def left(j, carry):
        _, k_aug, v_aug = tiles(j)
        base = lax.convert_element_type(i * tg - j * tk, F32)
        for st in range(n_streams):
            step(st, q_left[st], k_aug, v_aug, None, slope2 * (base + st * tq))
        return carry

    def right(j, carry):
        _, k_aug, v_aug = tiles(j)
        base = lax.convert_element_type(j * tk - i * tg, F32)
        for st in range(n_streams):
            step(st, q_right[st], k_aug, v_aug, None, slope2 * (base - st * tq))
        return carry

    lax.fori_loop(lo, i * mid, left, 0)
    for mj in range(mid):
        kt, k_aug, v_aug = tiles(i * mid + mj)
        for st in range(n_streams):
            off = st * tq - mj * tk
            if off - (tk - 1) >= 0:
                step(st, q_left[st], k_aug, v_aug, None, slope2 * off)
            elif off + (tq - 1) <= 0:
                step(st, q_right[st], k_aug, v_aug, None, slope2 * (-off))
            else:
                step(st, q2[st], kt, v_aug, -jnp.abs(gm_ref[...] + slope2 * off), 0.0)
    lax.fori_loop((i + 1) * mid, hi, right, 0)

    lamp = lamp_ref[...]
    lam = (jnp.exp(jnp.sum(lamp[0:1] * lamp[1:2], axis=1, keepdims=True))
           - jnp.exp(jnp.sum(lamp[2:3] * lamp[3:4], axis=1, keepdims=True)) + lam_init)
    for st in range(n_streams):
        o = acc_ref[st] / jnp.sum(l_ref[st], axis=1, keepdims=True)
        o = o[:tq] - lam * o[tq:]
        ms = jnp.mean(o * o, axis=-1, keepdims=True)
        y = o * lax.rsqrt(ms + LN_EPS) * g_ref[...] * (1.0 - lam_init)
        o_ref[st * tq:(st + 1) * tq, :] = y.astype(o_ref.dtype)


UNDERFLOW_LOG2 = 151.0


def _attn_a_bounds(proj, slopes, tg, tk):
    b, s, _ = proj.shape
    ni, nk, mid = s // tg, s // tk, tg // tk
    q = proj[:, :, COL_QA:COL_QA + 512].astype(F32)
    k = proj[:, :, COL_KA:COL_KA + 512].astype(F32)
    group = jnp.asarray(np.repeat(np.eye(2 * DA_HEADS, dtype=np.float32), 64, axis=0))

    def group_sums(a):
        return jnp.einsum("bsc,cg->bsg", a, group, precision=lax.Precision.HIGHEST,
                          preferred_element_type=F32)

    qn = jnp.sqrt(group_sums(q * q)).reshape(b, ni, tg, DA_HEADS, 2)
    own = group_sums(q * k).reshape(b, ni, tg, DA_HEADS, 2)
    kmax = jnp.max(jnp.sqrt(group_sums(k * k)), axis=1).reshape(b, DA_HEADS, 2)
    upper = jnp.max(qn, axis=2) * kmax[:, None] * 1.001
    slack = jnp.max(upper - jnp.min(own, axis=2), axis=-1) + UNDERFLOW_LOG2
    reach = slack / (slopes * math.log2(math.e))
    reach = jnp.minimum(reach, 4.0 * s)
    i0 = (jnp.arange(ni, dtype=F32) * tg)[None, :, None]
    lo = jnp.ceil((i0 + 1.0 - reach) / tk - 1.0)
    lo = jnp.clip(lo, 0, jnp.arange(ni, dtype=F32)[None, :, None] * mid)
    hi = jnp.floor((reach + i0 + tg - 1.0) / tk) + 1.0
    hi = jnp.clip(hi, (jnp.arange(ni, dtype=F32)[None, :, None] + 1.0) * mid, nk)
    flat = lambda a: a.astype(jnp.int32).transpose(0, 2, 1).reshape(-1)
    return flat(lo), flat(hi)


def _slope_pieces(slopes):
    rows = []
    for s in slopes:
        s2 = np.float32(np.float32(s) * np.float32(math.log2(math.e)))
        hi = np.float32(np.asarray(s2, dtype=jnp.bfloat16))
        mid = np.float32(np.asarray(np.float32(s2 - hi), dtype=jnp.bfloat16))
        lo = np.float32(np.asarray(np.float32(s2 - hi - mid), dtype=jnp.bfloat16))
        assert np.float32(np.float32(hi + mid) + lo) == s2
        rows += [s2, hi, mid, lo]
    return np.asarray(rows, np.float32)


def _attn_a(proj, slopes, slope_tab, lamp, subln_g, lam_init, tq, n_streams, tk):
    b, s, _ = proj.shape
    tg = tq * n_streams
    assert tg % tk == 0 and s % tg == 0
    lo, hi = _attn_a_bounds(proj, slopes, tg, tk)
    kern = functools.partial(_attn_a_kernel, tq=tq, n_streams=n_streams, tk=tk, lam_init=lam_init)
    grid_spec = pltpu.PrefetchScalarGridSpec(
        num_scalar_prefetch=2,
        grid=(b, DA_HEADS, s // tg),
        in_specs=[
            pl.BlockSpec(memory_space=pltpu.SMEM),
            pl.BlockSpec((4, 64), lambda bb, h, i, lo, hi: (0, 0)),
            pl.BlockSpec((1, LANES), lambda bb, h, i, lo, hi: (0, 0)),
            pl.BlockSpec((None, tg, LANES), lambda bb, h, i, lo, hi: (bb, i, COL_QA // LANES + h)),
            pl.BlockSpec((None, s, LANES), lambda bb, h, i, lo, hi: (bb, 0, COL_KA // LANES + h)),
            pl.BlockSpec((None, s, LANES), lambda bb, h, i, lo, hi: (bb, 0, COL_VA // LANES + h)),
        ],
        out_specs=pl.BlockSpec((None, tg, LANES), lambda bb, h, i, lo, hi: (bb, i, h)),
        scratch_shapes=[pltpu.VMEM((tq, tk), F32),
                        pltpu.VMEM((n_streams, 2 * tq, LANES), F32),
                        pltpu.VMEM((n_streams, 2 * tq, LANES), F32),
                        pltpu.VMEM((n_streams, 2 * tq, LANES), F32)],
    )
    return pl.pallas_call(
        kern,
        out_shape=jax.ShapeDtypeStruct((b, s, BRANCH_W), BF16),
        grid_spec=grid_spec,
        compiler_params=pltpu.CompilerParams(
            dimension_semantics=("parallel", "parallel", "arbitrary"),
            vmem_limit_bytes=VMEM_LIMIT),
        name="attn_a",
    )(lo, hi, slope_tab, lamp, subln_g, proj, proj, proj)


def _attn_b_kernel(slopes_ref, sink_ref, q_ref, kp_ref, kc_ref, kn_ref, vp_ref, vc_ref, vn_ref,
                   o_ref, *, tq, seq):
    i = pl.program_id(1)
    kfull = jnp.concatenate([kp_ref[...], kc_ref[...], kn_ref[...]], axis=0)
    vfull = jnp.concatenate([vp_ref[...], vc_ref[...], vn_ref[...]], axis=0)
    band = 3 * BLOCK
    r = lax.broadcasted_iota(jnp.int32, (BLOCK, band), 0)
    c = lax.broadcasted_iota(jnp.int32, (BLOCK, band), 1)
    rel_i = jnp.abs(r + BLOCK - c)
    rel = rel_i.astype(F32)
    lane = lax.broadcasted_iota(jnp.int32, (BLOCK, LANES), 1)
    lo_half = lane < 64
    log2e = math.log2(math.e)
    heads = [(n // 2) + 4 * (n % 2) for n in range(8)]
    biases = [(slopes_ref[hq] * log2e) * rel for hq in heads]

    for sub in range(tq // BLOCK):
        q_start = i * tq + sub * BLOCK
        valid = ((rel_i <= BLOCK) & (c >= BLOCK - q_start) & (c < seq + BLOCK - q_start))
        kband = kfull[sub * BLOCK: sub * BLOCK + band]
        vband = vfull[sub * BLOCK: sub * BLOCK + band]
        qblk = q_ref[sub * BLOCK:(sub + 1) * BLOCK, :]
        parts = []
        for g in range(4):
            qg = qblk[:, g * LANES:(g + 1) * LANES]
            zero = jnp.zeros_like(qg)
            parts.append(jnp.where(lo_half, qg, zero))
            parts.append(jnp.where(lo_half, zero, qg))
        qs = jnp.concatenate(parts, axis=0)
        s_all = lax.dot_general(qs, kband, _NT, preferred_element_type=F32)
        ps, invs = [], []
        for n, hq in enumerate(heads):
            s = s_all[n * BLOCK:(n + 1) * BLOCK]
            s = jnp.where(valid, s - biases[n], NEG)
            sk = sink_ref[hq] * log2e
            m = jnp.maximum(jnp.max(s, axis=1, keepdims=True), sk)
            e = jnp.exp2(s - m)
            den = jnp.sum(e, axis=1, keepdims=True) + jnp.exp2(sk - m)
            ps.append(e.astype(BF16))
            invs.append(1.0 / den)
        p_all = jnp.concatenate(ps, axis=0)
        o_all = jnp.dot(p_all, vband, preferred_element_type=F32)
        for g in range(4):
            o_lo = o_all[(2 * g) * BLOCK:(2 * g + 1) * BLOCK] * invs[2 * g]
            o_hi = o_all[(2 * g + 1) * BLOCK:(2 * g + 2) * BLOCK] * invs[2 * g + 1]
            o_ref[sub * BLOCK:(sub + 1) * BLOCK, g * LANES:(g + 1) * LANES] = jnp.where(
                lo_half, o_lo, o_hi).astype(o_ref.dtype)


def _attn_b(proj, slopes, sink, tq):
    b, s, _ = proj.shape
    nb = s // BLOCK
    r = tq // BLOCK
    kcol, vcol = COL_KB // LANES, COL_VB // LANES

    def prev_map(col):
        return lambda bb, i: (bb, jnp.maximum(i * r - 1, 0), col)

    def cur_map(col):
        return lambda bb, i: (bb, i, col)

    def next_map(col):
        return lambda bb, i: (bb, jnp.minimum(i * r + r, nb - 1), col)

    return pl.pallas_call(
        functools.partial(_attn_b_kernel, tq=tq, seq=s),
        out_shape=jax.ShapeDtypeStruct((b, s, BRANCH_W), BF16),
        grid=(b, s // tq),
        in_specs=[
            pl.BlockSpec(memory_space=pltpu.SMEM),
            pl.BlockSpec(memory_space=pltpu.SMEM),
            pl.BlockSpec((None, tq, BRANCH_W), lambda bb, i: (bb, i, COL_QB // BRANCH_W)),
            pl.BlockSpec((None, BLOCK, LANES), prev_map(kcol)),
            pl.BlockSpec((None, tq, LANES), cur_map(kcol)),
            pl.BlockSpec((None, BLOCK, LANES), next_map(kcol)),
            pl.BlockSpec((None, BLOCK, LANES), prev_map(vcol)),
            pl.BlockSpec((None, tq, LANES), cur_map(vcol)),
            pl.BlockSpec((None, BLOCK, LANES), next_map(vcol)),
        ],
        out_specs=pl.BlockSpec((None, tq, BRANCH_W), lambda bb, i: (bb, i, 0)),
        compiler_params=pltpu.CompilerParams(
            dimension_semantics=("parallel", "parallel"), vmem_limit_bytes=VMEM_LIMIT),
        name="attn_b",
    )(slopes, sink, proj, proj, proj, proj, proj, proj, proj)


def _attn_c_kernel(q_ref, mem_ref, wkv_ref, o_ref, mk_ref, mv_ref):
    @pl.when(pl.program_id(1) == 0)
    def _():
        kv = jnp.dot(mem_ref[...].astype(BF16), wkv_ref[...], preferred_element_type=F32)
        mk_ref[...] = kv[:, :BRANCH_W].astype(BF16)
        mv_ref[...] = kv[:, BRANCH_W:].astype(BF16)

    scale = MEM_DH ** -0.5
    for h in range(MEM_HEADS):
        cols = slice(h * MEM_DH, (h + 1) * MEM_DH)
        s = lax.dot_general(q_ref[:, cols], mk_ref[:, cols], _NT,
                            preferred_element_type=F32) * scale
        m = jnp.max(s, axis=1, keepdims=True)
        e = jnp.exp(s - m)
        inv = 1.0 / jnp.sum(e, axis=1, keepdims=True)
        o = jnp.dot(e.astype(BF16), mv_ref[:, cols], preferred_element_type=F32)
        o_ref[:, cols] = (o * inv).astype(o_ref.dtype)


def _attn_c(proj, mem, wkv, tq):
    b, s, _ = proj.shape
    return pl.pallas_call(
        _attn_c_kernel,
        out_shape=jax.ShapeDtypeStruct((b, s, BRANCH_W), BF16),
        grid=(b, s // tq),
        in_specs=[
            pl.BlockSpec((None, tq, BRANCH_W), lambda bb, i: (bb, i, COL_QC // BRANCH_W)),
            pl.BlockSpec((None, N_MEM, D_MODEL), lambda bb, i: (bb, 0, 0)),
            pl.BlockSpec((D_MODEL, 2 * BRANCH_W), lambda bb, i: (0, 0)),
        ],
        out_specs=pl.BlockSpec((None, tq, BRANCH_W), lambda bb, i: (bb, i, 0)),
        scratch_shapes=[pltpu.VMEM((N_MEM, BRANCH_W), BF16), pltpu.VMEM((N_MEM, BRANCH_W), BF16)],
        compiler_params=pltpu.CompilerParams(
            dimension_semantics=("parallel", "arbitrary"), vmem_limit_bytes=VMEM_LIMIT),
        name="attn_c",
    )(proj, mem, wkv)


def _merge_kernel(x_ref, oa_ref, ob_ref, oc_ref, wg_ref, bg_ref, wpa_ref, wpb_ref, wpc_ref,
                  wo_ref, g_ref, b_ref, o_ref, *, alpha):
    x = x_ref[...]
    xb = x.astype(BF16)
    merged = None
    for n, (br_ref, wp_ref) in enumerate(((oa_ref, wpa_ref), (ob_ref, wpb_ref), (oc_ref, wpc_ref))):
        cols = slice(n * D_MODEL, (n + 1) * D_MODEL)
        gate = jax.nn.sigmoid(
            jnp.dot(xb, wg_ref[:, cols], preferred_element_type=F32) + bg_ref[:, cols])
        term = gate * jnp.dot(br_ref[...], wp_ref[...], preferred_element_type=F32)
        merged = term if merged is None else merged + term
    y = jnp.dot(merged.astype(BF16), wo_ref[...], preferred_element_type=F32)
    o_ref[...] = _layer_norm(alpha * x + y, g_ref[...], b_ref[...])


def _merge(x2d, oa, ob, oc, wg, bg, wpa, wpb, wpc, wo, g, b, alpha, tm):
    t, d = x2d.shape
    const = lambda i: (0, 0)
    row = lambda i: (i, 0)
    return pl.pallas_call(
        functools.partial(_merge_kernel, alpha=alpha),
        out_shape=jax.ShapeDtypeStruct((t, d), F32),
        grid=(t // tm,),
        in_specs=[
            pl.BlockSpec((tm, d), row),
            pl.BlockSpec((tm, BRANCH_W), row),
            pl.BlockSpec((tm, BRANCH_W), row),
            pl.BlockSpec((tm, BRANCH_W), row),
            pl.BlockSpec((d, 3 * d), const),
            pl.BlockSpec((1, 3 * d), const),
            pl.BlockSpec((BRANCH_W, d), const),
            pl.BlockSpec((BRANCH_W, d), const),
            pl.BlockSpec((BRANCH_W, d), const),
            pl.BlockSpec((d, d), const),
            pl.BlockSpec((1, d), const),
            pl.BlockSpec((1, d), const),
        ],
        out_specs=pl.BlockSpec((tm, d), row),
        compiler_params=pltpu.CompilerParams(
            dimension_semantics=("parallel",), vmem_limit_bytes=VMEM_LIMIT),
        name="merge",
    )(x2d, oa, ob, oc, wg, bg, wpa, wpb, wpc, wo, g, b)


def _sort_network(n):
    pairs = []

    def merge(lo, hi, r):
        step = r * 2
        if step < hi - lo:
            merge(lo, hi, step)
            merge(lo + r, hi, step)
            pairs.extend((k, k + r) for k in range(lo + r, hi - r, step))
        else:
            pairs.append((lo, lo + r))

    def sort(lo, hi):
        if hi - lo >= 1:
            mid = lo + (hi - lo) // 2
            sort(lo, mid)
            sort(mid + 1, hi)
            merge(lo, hi, 1)

    sort(0, n - 1)
    return pairs


_SORT16 = _sort_network(PEER_TOPK)


def _top16_desc(slabs, presorted=False):
    v = list(slabs)
    for a, b in ([] if presorted else _SORT16):
        hi, lo = jnp.maximum(v[a], v[b]), jnp.minimum(v[a], v[b])
        v[a], v[b] = hi, lo
    for shift in (4, 2, 1):
        v = [jnp.maximum(v[k], pltpu.roll(v[PEER_TOPK - 1 - k], shift, 0))
             for k in range(PEER_TOPK)]
        step = PEER_TOPK // 2
        while step >= 1:
            for k in range(PEER_TOPK):
                if k & step == 0:
                    hi, lo = jnp.maximum(v[k], v[k + step]), jnp.minimum(v[k], v[k + step])
                    v[k], v[k + step] = hi, lo
            step //= 2
    return v


def _peer_route(s0, s1):
    n = s0.shape[1]
    top_a = _top16_desc([s0[k * SUBLANES:(k + 1) * SUBLANES] for k in range(N_KEYS // SUBLANES)])
    top_b = _top16_desc([s1[k * SUBLANES:(k + 1) * SUBLANES] for k in range(N_KEYS // SUBLANES)])
    a0, b0 = top_a[0], top_b[0]
    ea = [jnp.exp(t - a0) for t in top_a]
    eb = [jnp.exp(t - b0) for t in top_b]
    sub = lax.broadcasted_iota(jnp.int32, (SUBLANES, n), 0)
    first4 = sub < 4

    def candidates(ea_list):
        col = ea_list[0]
        for s in range(1, 4):
            col = jnp.where(sub == s, ea_list[s], col)
        for s in range(4, 8):
            col = jnp.where(sub == s, eb[s - 4], col)
        out = []
        for v in range(PEER_TOPK):
            c = col * jnp.where(first4, eb[v], ea_list[min(v + 4, PEER_TOPK - 1)])
            if v >= PEER_TOPK - 4:
                c = jnp.where(first4, c, -1.0)
            out.append(c)
        return out

    cand = candidates(ea)
    best = _top16_desc(cand, presorted=True)
    z = best[0]
    for t in best[1:]:
        z = z + t
    inv_z = (2.0 ** -0.5) / z
    theta = best[PEER_TOPK - 1]
    ean = [t * inv_z for t in ea]
    cand_n = candidates(ean)
    thr = None
    for c, cn in zip(cand, cand_n):
        t = jnp.where(c >= theta, cn, jnp.inf)
        thr = t if thr is None else jnp.minimum(thr, t)
    for shift in (4, 2, 1):
        thr = jnp.minimum(thr, pltpu.roll(thr, shift, 0))
    thr = thr[0:1]
    e1n = jnp.exp(s0 - a0[0:1]) * inv_z[0:1]
    e2 = jnp.exp(s1 - b0[0:1])
    psi = jnp.full(s0.shape, float(PEER_TOPK), F32)
    for r in range(PEER_TOPK):
        psi = jnp.where(e1n * eb[r][0:1] >= thr, float(PEER_TOPK - 1 - r), psi)
    code2 = jnp.zeros(s1.shape, F32)
    for r in reversed(range(PEER_TOPK)):
        code2 = jnp.where(s1 >= top_b[r][0:1], float(PEER_TOPK - r), code2)
    return e1n, e2, psi, code2


def _peer_kernel(x1_ref, wpq_ref, keys_ref, u_ref, vt_ref, g_ref, b_ref, o_ref,
                 xb_ref, q_ref, e1_ref, psi_ref, e2_ref, code_ref, acc_ref, wa_ref, *,
                 alpha, ec, lane_chunk):
    j = pl.program_id(1)
    tm = x1_ref.shape[0]
    n1 = ec // N_KEYS
    assert n1 % SUBLANES == 0

    @pl.when(j == 0)
    def _():
        xb = x1_ref[...].astype(BF16)
        xb_ref[...] = xb
        acc_ref[...] = jnp.zeros(acc_ref.shape, F32)
        for c in range(0, q_ref.shape[1], 512):
            q_ref[:, c:c + 512] = jnp.dot(xb, wpq_ref[:, c:c + 512],
                                          preferred_element_type=F32).astype(BF16)
        for h in range(PEER_HEADS):
            s = []
            for half in range(2):
                r = 2 * h + half
                s.append(lax.dot_general(keys_ref[r], q_ref[:, r * N_KEYS:(r + 1) * N_KEYS], _NT,
                                         preferred_element_type=F32))
            for lt in range(0, tm, 2 * LANES):
                cols = slice(lt, lt + 2 * LANES)
                e1n, e2, psi, code2 = _peer_route(s[0][:, cols], s[1][:, cols])
                e1_ref[h, :, cols] = e1n
                psi_ref[h, :, cols] = psi
                e2_ref[h, :, cols] = e2.astype(BF16)
                code_ref[h, :, cols] = code2.astype(BF16)

    def packed_row(tile, ii):
        r16 = jnp.broadcast_to(tile[ii:ii + 1, :], (2 * SUBLANES, tile.shape[1])).astype(BF16)
        return jnp.concatenate([r16] * (N_KEYS // (2 * SUBLANES)), axis=0)

    ht = lax.dot_general(u_ref[...], xb_ref[...], _NT, preferred_element_type=F32)
    base = pl.multiple_of(j * n1, SUBLANES)
    for ii in range(n1):
        rows = slice(ii * N_KEYS, (ii + 1) * N_KEYS)
        for lc in range(0, tm, lane_chunk):
            cols = slice(lc, lc + lane_chunk)
            w = None
            grp = pl.multiple_of(base + (ii // SUBLANES) * SUBLANES, SUBLANES)
            for h in range(PEER_HEADS):
                e1b = packed_row(e1_ref[h, pl.ds(grp, SUBLANES), cols], ii % SUBLANES)
                psib = packed_row(psi_ref[h, pl.ds(grp, SUBLANES), cols], ii % SUBLANES)
                p = e1b * e2_ref[h, :, cols]
                t = jnp.where(code_ref[h, :, cols] > psib, p, jnp.zeros_like(p))
                w = t if w is None else w + t
            hblk = ht[rows, cols]
            act = hblk * (1.0 + lax.erf(hblk))
            wa_ref[rows, cols] = w * act.astype(BF16)
    acc_ref[...] += jnp.dot(vt_ref[...], wa_ref[...], preferred_element_type=F32)

    @pl.when(j == pl.num_programs(1) - 1)
    def _():
        z = alpha * x1_ref[...] + acc_ref[...].T
        o_ref[...] = _layer_norm(z, g_ref[...], b_ref[...])


def _peer(x1, wpq, keys, u, vt, g, b, alpha, tm, ec, lane_chunk):
    t, d = x1.shape
    kern = functools.partial(_peer_kernel, alpha=alpha, ec=ec, lane_chunk=lane_chunk)
    n_chunks = N_EXPERTS // ec
    return pl.pallas_call(
        kern,
        out_shape=jax.ShapeDtypeStruct((t, d), F32),
        grid=(t // tm, n_chunks),
        in_specs=[
            pl.BlockSpec((tm, d), lambda i, j: (i, 0)),
            pl.BlockSpec(wpq.shape, lambda i, j: (0, 0)),
            pl.BlockSpec(keys.shape, lambda i, j: (0, 0, 0)),
            pl.BlockSpec((ec, d), lambda i, j: (j, 0)),
            pl.BlockSpec((d, ec), lambda i, j: (0, j)),
            pl.BlockSpec((1, d), lambda i, j: (0, 0)),
            pl.BlockSpec((1, d), lambda i, j: (0, 0)),
        ],
        out_specs=pl.BlockSpec((tm, d), lambda i, j: (i, 0)),
        scratch_shapes=[
            pltpu.VMEM((tm, d), BF16),
            pltpu.VMEM((tm, wpq.shape[1]), BF16),
            pltpu.VMEM((PEER_HEADS, N_KEYS, tm), F32),
            pltpu.VMEM((PEER_HEADS, N_KEYS, tm), F32),
            pltpu.VMEM((PEER_HEADS, N_KEYS, tm), BF16),
            pltpu.VMEM((PEER_HEADS, N_KEYS, tm), BF16),
            pltpu.VMEM((d, tm), F32),
            pltpu.VMEM((ec, tm), BF16),
        ],
        compiler_params=pltpu.CompilerParams(
            dimension_semantics=("parallel", "arbitrary"), vmem_limit_bytes=VMEM_LIMIT),
        name="peer",
    )(x1, wpq, keys, u, vt, g, b)


def _prep_weights(depth, l, w_in, w_mem_kv, lam_q1, lam_k1, lam_q2, lam_k2, subln_g, sink, w_gate,
                  b_gate, w_pa, w_pb, w_pc, w_o, ln1_g, ln1_b, w_pq, sub_keys, peer_u, peer_v,
                  ln2_g, ln2_b):
    w = w_in[l]
    qb_perm = np.concatenate([np.arange(64) + 64 * hq for g in range(4) for hq in (g, g + 4)])
    qa = w[:, 0:512] * (0.125 * math.log2(math.e))
    ka, va = w[:, 512:1024], w[:, 1024:1536]
    qb = w[:, 1536:2048][:, qb_perm] * (0.125 * math.log2(math.e))
    kb, vb, qc = w[:, 2048:2176], w[:, 2176:2304], w[:, 2304:2816]
    w_proj = jnp.concatenate([qa, ka, va, qb, qc, kb, vb], axis=1).astype(BF16)
    lamp = jnp.stack([lam_q1[l], lam_k1[l], lam_q2[l], lam_k2[l]]).astype(F32)
    row = lambda a: a.astype(F32).reshape(1, -1)
    return dict(
        w_proj=w_proj,
        w_mem_kv=w_mem_kv[l].astype(BF16),
        lamp=lamp,
        subln_g=row(subln_g[l]),
        sink=sink[l].astype(F32),
        w_gate=w_gate[l].astype(BF16),
        b_gate=row(b_gate[l]),
        w_pa=w_pa[l].astype(BF16),
        w_pb=w_pb[l][qb_perm].astype(BF16),
        w_pc=w_pc[l].astype(BF16),
        w_o=w_o[l].astype(BF16),
        ln1_g=row(ln1_g[l]), ln1_b=row(ln1_b[l]),
        w_pq=w_pq[l].astype(BF16),
        keys=sub_keys[l].reshape(2 * PEER_HEADS, N_KEYS, N_KEYS).astype(BF16),
        peer_u=(peer_u[l] * (2.0 ** -0.5)).astype(BF16),
        peer_vt=peer_v[l].T.astype(BF16),
        ln2_g=row(ln2_g[l]), ln2_b=row(ln2_b[l]),
        lam_init=0.8 - 0.6 * math.exp(-0.3 * l),
        alpha=(2.0 * depth) ** 0.25,
    )


def _tile(n, pref):
    t = min(n, pref)
    assert n % t == 0, (n, t)
    return t


def _encoder_layer(x, mem, p):
    b, s, d = x.shape
    t = b * s
    x2d = x.reshape(t, d)
    proj = _proj(x2d, p["w_proj"], _tile(t, 512)).reshape(b, s, PROJ_COLS)
    slopes_a = jnp.asarray(2.0 ** (-8.0 * np.arange(1, DA_HEADS + 1) / DA_HEADS), F32)
    slope_tab = jnp.asarray(_slope_pieces(2.0 ** (-8.0 * np.arange(1, DA_HEADS + 1) / DA_HEADS)))
    slopes_b = jnp.asarray(2.0 ** (-8.0 * np.arange(1, WA_HEADS + 1) / WA_HEADS), F32)
    oa = _attn_a(proj, slopes_a, slope_tab, p["lamp"], p["subln_g"], p["lam_init"], 256, 4,
                 1024 if s >= 8192 else 512)
    ob = _attn_b(proj, slopes_b, p["sink"], _tile(s, 256))
    oc = _attn_c(proj, mem, p["w_mem_kv"], _tile(s, 512))
    x1 = _merge(x2d, oa.reshape(t, BRANCH_W), ob.reshape(t, BRANCH_W), oc.reshape(t, BRANCH_W),
                p["w_gate"], p["b_gate"], p["w_pa"], p["w_pb"], p["w_pc"], p["w_o"],
                p["ln1_g"], p["ln1_b"], p["alpha"], _tile(t, 512))
    y = _peer(x1, p["w_pq"], p["keys"], p["peer_u"], p["peer_vt"], p["ln2_g"], p["ln2_b"],
              p["alpha"], _tile(t, 512), 2048, 256)
    return y.reshape(b, s, d)


def kernel(x_prompt, x_sample, mem_prompt, mem_sample, w_in, w_mem_kv, lam_q1, lam_k1, lam_q2,
           lam_k2, subln_g, sink, w_gate, b_gate, w_pa, w_pb, w_pc, w_o, ln1_g, ln1_b, w_pq,
           sub_keys, peer_u, peer_v, ln2_g, ln2_b):
    depth = w_in.shape[0]
    y_prompt, y_sample = x_prompt, x_sample
    for l in range(depth):
        p = _prep_weights(depth, l, w_in, w_mem_kv, lam_q1, lam_k1, lam_q2, lam_k2, subln_g, sink,
                          w_gate, b_gate, w_pa, w_pb, w_pc, w_o, ln1_g, ln1_b, w_pq, sub_keys,
                          peer_u, peer_v, ln2_g, ln2_b)
        y_prompt = _encoder_layer(y_prompt, mem_prompt, p)
        y_sample = _encoder_layer(y_sample, mem_sample, p)
    return (y_prompt, y_sample)
```

```python
import functools
import math

import jax
import jax.numpy as jnp
import numpy as np
from jax import lax
from jax.experimental import pallas as pl
from jax.experimental.pallas import tpu as pltpu

F32 = jnp.float32
BF16 = jnp.bfloat16

D_MODEL = 1024
N_MEM = 256
BLOCK = 128
DA_HEADS = 4
WA_HEADS = 8
MEM_HEADS = 4
MEM_DH = 128
BRANCH_W = 512
PEER_HEADS = 8
N_KEYS = 128
N_EXPERTS = N_KEYS * N_KEYS
PEER_TOPK = 16
LN_EPS = 1e-5
NEG = -1e30
LANES = 128
SUBLANES = 8

COL_QA, COL_KA, COL_VA, COL_QB, COL_QC, COL_KB, COL_VB = 0, 512, 1024, 1536, 2048, 2560, 2688
PROJ_COLS = 2816

VMEM_LIMIT = 56 * 1024 * 1024

_NT = (((1,), (1,)), ((), ()))


def _layer_norm(z, g, b):
    mu = jnp.mean(z, axis=-1, keepdims=True)
    zc = z - mu
    var = jnp.mean(zc * zc, axis=-1, keepdims=True)
    return zc * lax.rsqrt(var + LN_EPS) * g + b


def _proj_kernel(x_ref, w_ref, grp_ref, o_ref, st_ref, *, n_chunk):
    xb = x_ref[...].astype(BF16)
    rounded = []
    for c in range(0, o_ref.shape[-1], n_chunk):
        y = jnp.dot(xb, w_ref[:, c:c + n_chunk], preferred_element_type=F32).astype(BF16)
        o_ref[:, c:c + n_chunk] = y
        if c < COL_VA:
            rounded.append(y.astype(F32))
    half = len(rounded) // 2
    q = jnp.concatenate(rounded[:half], axis=1)
    k = jnp.concatenate(rounded[half:], axis=1)
    prod = jnp.concatenate([q * q, k * k, q * k], axis=1).astype(BF16)
    st_ref[...] = jnp.dot(prod, grp_ref[...], preferred_element_type=F32)


def _proj(x2d, w, tm):
    t, d = x2d.shape
    n = w.shape[1]
    width = COL_KA - COL_QA
    grp = np.zeros((3 * width, LANES), np.float32)
    for part in range(3):
        grp[part * width + np.arange(width), part * (width // 64) + np.arange(width) // 64] = 1.0
    grp = jnp.asarray(grp, BF16)
    return pl.pallas_call(
        functools.partial(_proj_kernel, n_chunk=256),
        out_shape=(jax.ShapeDtypeStruct((t, n), BF16), jax.ShapeDtypeStruct((t, LANES), F32)),
        grid=(t // tm,),
        in_specs=[pl.BlockSpec((tm, d), lambda i: (i, 0)),
                  pl.BlockSpec((d, n), lambda i: (0, 0)),
                  pl.BlockSpec(grp.shape, lambda i: (0, 0))],
        out_specs=(pl.BlockSpec((tm, n), lambda i: (i, 0)),
                   pl.BlockSpec((tm, LANES), lambda i: (i, 0))),
        compiler_params=pltpu.CompilerParams(
            dimension_semantics=("parallel",), vmem_limit_bytes=VMEM_LIMIT),
        name="proj",
    )(x2d, w, grp)


def _attn_a_kernel(lo_ref, hi_ref, slopes_ref, lamp_ref, g_ref, q_ref, k_ref, v_ref, o_ref,
                   gm_ref, m_ref, l_ref, acc_ref, *, tq, n_streams, tk, lam_init):
    bb = pl.program_id(0)
    h = pl.program_id(1)
    i = pl.program_id(2)
    tg = n_streams * tq
    mid = tg // tk
    flat = (bb * pl.num_programs(1) + h) * pl.num_programs(2) + i
    lo = lo_ref[flat]
    hi = hi_ref[flat]
    slope2 = slopes_ref[4 * h]
    pieces = [slopes_ref[4 * h + 1 + n] for n in range(3)]
    reps = tk // LANES
    assert tq <= 256 and tk % 256 == 0

    gm_ref[...] = (lax.broadcasted_iota(jnp.int32, (tq, tk), 0)
                   - lax.broadcasted_iota(jnp.int32, (tq, tk), 1)).astype(F32) * slope2
    m_ref[...] = jnp.full(m_ref.shape, -jnp.inf, F32)
    l_ref[...] = jnp.zeros(l_ref.shape, F32)
    acc_ref[...] = jnp.zeros(acc_ref.shape, F32)

    lane = lax.broadcasted_iota(jnp.int32, (tq, LANES), 1)
    row = lax.broadcasted_iota(jnp.int32, (tq, LANES), 0).astype(F32)
    qf = jnp.where(lane < 3, row, 0.0)
    klane = lax.broadcasted_iota(jnp.int32, (tk, LANES), 1)
    col = lax.broadcasted_iota(jnp.int32, (tk, LANES), 0)
    col_lo = (col & 255).astype(F32)
    kf = jnp.where((klane >= 3) & (klane < 6), col_lo,
                   jnp.where((klane >= 6) & (klane < 9), col.astype(F32) - col_lo, 0.0))
    for n in range(3):
        qf = jnp.where((lane == 3 + n) | (lane == 6 + n), -pieces[n], qf)
        kf = jnp.where(klane == n, pieces[n], kf)
    qf = jnp.concatenate([qf, qf], axis=0)
    kf = kf.astype(BF16)
    ones_col = jnp.where(klane == 0, 1.0, 0.0).astype(BF16)

    q2, q_right, q_left = [], [], []
    for st in range(n_streams):
        q = q_ref[st * tq:(st + 1) * tq, :]
        zero = jnp.zeros_like(q)
        q2.append(jnp.concatenate([jnp.where(lane < 64, q, zero),
                                   jnp.where(lane >= 64, q, zero)], axis=0))
        q_right.append(jnp.concatenate([q2[st], qf.astype(BF16)], axis=1))
        q_left.append(jnp.concatenate([q2[st], (-qf).astype(BF16)], axis=1))

    def step(st, lhs, rhs, v_aug, bias, shift):
        x = lax.dot_general(lhs, rhs, _NT, preferred_element_type=F32)
        if bias is not None:
            x = x + jnp.concatenate([bias, bias], axis=0)
        m_prev = m_ref[st]
        m_next = jnp.maximum(m_prev, jnp.max(x, axis=1, keepdims=True) - shift)
        alpha = jnp.exp2(m_prev - m_next)
        sub = m_next + shift
        p = jnp.exp2(x - jnp.concatenate([sub] * reps, axis=1))
        pv = jnp.dot(p.astype(BF16), v_aug, preferred_element_type=F32)
        l_ref[st] = alpha * l_ref[st] + pv[:, LANES:]
        acc_ref[st] = alpha * acc_ref[st] + pv[:, :LANES]
        m_ref[st] = m_next

    def tiles(j):
        ks = pl.multiple_of(j * tk, tk)
        kt = k_ref[pl.ds(ks, tk), :]
        v_aug = jnp.concatenate([v_ref[pl.ds(ks, tk), :], ones_col], axis=1)
        return kt, jnp.concatenate([kt, kf], axis=1), v_aug

    def left(j, carry):
        _, k_aug, v_aug = tiles(j)
        base = lax.convert_element_type(i * tg - j * tk, F32)
        for st in range(n_streams):
            step(st, q_left[st], k_aug, v_aug, None, slope2 * (base + st * tq))
        return carry

    def right(j, carry):
        _, k_aug, v_aug = tiles(j)
        base = lax.convert_element_type(j * tk - i * tg, F32)
        for st in range(n_streams):
            step(st, q_right[st], k_aug, v_aug, None, slope2 * (base - st * tq))
        return carry

    lax.fori_loop(lo, i * mid, left, 0)
    for mj in range(mid):
        kt, k_aug, v_aug = tiles(i * mid + mj)
        for st in range(n_streams):
            off = st * tq - mj * tk
            if off - (tk - 1) >= 0:
                step(st, q_left[st], k_aug, v_aug, None, slope2 * off)
            elif off + (tq - 1) <= 0:
                step(st, q_right[st], k_aug, v_aug, None, slope2 * (-off))
            else:
                step(st, q2[st], kt, v_aug, -jnp.abs(gm_ref[...] + slope2 * off), 0.0)
    lax.fori_loop((i + 1) * mid, hi, right, 0)

    lamp = lamp_ref[...]
    lam = (jnp.exp(jnp.sum(lamp[0:1] * lamp[1:2], axis=1, keepdims=True))
           - jnp.exp(jnp.sum(lamp[2:3] * lamp[3:4], axis=1, keepdims=True)) + lam_init)
    for st in range(n_streams):
        o = acc_ref[st] / jnp.sum(l_ref[st], axis=1, keepdims=True)
        o = o[:tq] - lam * o[tq:]
        ms = jnp.mean(o * o, axis=-1, keepdims=True)
        y = o * lax.rsqrt(ms + LN_EPS) * g_ref[...] * (1.0 - lam_init)
        o_ref[st * tq:(st + 1) * tq, :] = y.astype(o_ref.dtype)


UNDERFLOW_LOG2 = 151.0
NORM_SLACK = 1.004
OWN_SLACK = 2.0 ** -8


def _attn_a_bounds(stats, slopes, tg, tk):
    b, s, _ = stats.shape
    ni, nk, mid = s // tg, s // tk, tg // tk
    ng = 2 * DA_HEADS
    qn = jnp.sqrt(stats[:, :, 0:ng]).reshape(b, ni, tg, DA_HEADS, 2)
    kmax = jnp.max(jnp.sqrt(stats[:, :, ng:2 * ng]), axis=1).reshape(b, DA_HEADS, 2)
    own = stats[:, :, 2 * ng:3 * ng].reshape(b, ni, tg, DA_HEADS, 2)
    upper = jnp.max(qn, axis=2) * kmax[:, None] * NORM_SLACK
    own_low = jnp.min(own, axis=2) - OWN_SLACK * upper
    slack = jnp.max(upper - own_low, axis=-1) + UNDERFLOW_LOG2
    reach = slack / (slopes * math.log2(math.e))
    reach = jnp.minimum(reach, 4.0 * s)
    i0 = (jnp.arange(ni, dtype=F32) * tg)[None, :, None]
    lo = jnp.ceil((i0 + 1.0 - reach) / tk - 1.0)
    lo = jnp.clip(lo, 0, jnp.arange(ni, dtype=F32)[None, :, None] * mid)
    hi = jnp.floor((reach + i0 + tg - 1.0) / tk) + 1.0
    hi = jnp.clip(hi, (jnp.arange(ni, dtype=F32)[None, :, None] + 1.0) * mid, nk)
    flat = lambda a: a.astype(jnp.int32).transpose(0, 2, 1).reshape(-1)
    return flat(lo), flat(hi)


def _slope_pieces(slopes):
    rows = []
    for s in slopes:
        s2 = np.float32(np.float32(s) * np.float32(math.log2(math.e)))
        hi = np.float32(np.asarray(s2, dtype=jnp.bfloat16))
        mid = np.float32(np.asarray(np.float32(s2 - hi), dtype=jnp.bfloat16))
        lo = np.float32(np.asarray(np.float32(s2 - hi - mid), dtype=jnp.bfloat16))
        assert np.float32(np.float32(hi + mid) + lo) == s2
        rows += [s2, hi, mid, lo]
    return np.asarray(rows, np.float32)


def _attn_a(proj, stats, slopes, slope_tab, lamp, subln_g, lam_init, tq, n_streams, tk):
    b, s, _ = proj.shape
    tg = tq * n_streams
    assert tg % tk == 0 and s % tg == 0
    lo, hi = _attn_a_bounds(stats, slopes, tg, tk)
    kern = functools.partial(_attn_a_kernel, tq=tq, n_streams=n_streams, tk=tk, lam_init=lam_init)
    grid_spec = pltpu.PrefetchScalarGridSpec(
        num_scalar_prefetch=2,
        grid=(b, DA_HEADS, s // tg),
        in_specs=[
            pl.BlockSpec(memory_space=pltpu.SMEM),
            pl.BlockSpec((4, 64), lambda bb, h, i, lo, hi: (0, 0)),
            pl.BlockSpec((1, LANES), lambda bb, h, i, lo, hi: (0, 0)),
            pl.BlockSpec((None, tg, LANES), lambda bb, h, i, lo, hi: (bb, i, COL_QA // LANES + h)),
            pl.BlockSpec((None, s, LANES), lambda bb, h, i, lo, hi: (bb, 0, COL_KA // LANES + h)),
            pl.BlockSpec((None, s, LANES), lambda bb, h, i, lo, hi: (bb, 0, COL_VA // LANES + h)),
        ],
        out_specs=pl.BlockSpec((None, tg, LANES), lambda bb, h, i, lo, hi: (bb, i, h)),
        scratch_shapes=[pltpu.VMEM((tq, tk), F32),
                        pltpu.VMEM((n_streams, 2 * tq, LANES), F32),
                        pltpu.VMEM((n_streams, 2 * tq, LANES), F32),
                        pltpu.VMEM((n_streams, 2 * tq, LANES), F32)],
    )
    return pl.pallas_call(
        kern,
        out_shape=jax.ShapeDtypeStruct((b, s, BRANCH_W), BF16),
        grid_spec=grid_spec,
        compiler_params=pltpu.CompilerParams(
            dimension_semantics=("parallel", "parallel", "arbitrary"),
            vmem_limit_bytes=VMEM_LIMIT),
        name="attn_a",
    )(lo, hi, slope_tab, lamp, subln_g, proj, proj, proj)


def _attn_b_kernel(slopes_ref, sink_ref, q_ref, kp_ref, kc_ref, kn_ref, vp_ref, vc_ref, vn_ref,
                   o_ref, *, tq, seq):
    i = pl.program_id(1)
    kfull = jnp.concatenate([kp_ref[...], kc_ref[...], kn_ref[...]], axis=0)
    vfull = jnp.concatenate([vp_ref[...], vc_ref[...], vn_ref[...]], axis=0)
    band = 3 * BLOCK
    r = lax.broadcasted_iota(jnp.int32, (BLOCK, band), 0)
    c = lax.broadcasted_iota(jnp.int32, (BLOCK, band), 1)
    rel_i = jnp.abs(r + BLOCK - c)
    rel = rel_i.astype(F32)
    lane = lax.broadcasted_iota(jnp.int32, (BLOCK, LANES), 1)
    lo_half = lane < 64
    log2e = math.log2(math.e)
    heads = [(n // 2) + 4 * (n % 2) for n in range(8)]
    biases = [(slopes_ref[hq] * log2e) * rel for hq in heads]

    for sub in range(tq // BLOCK):
        q_start = i * tq + sub * BLOCK
        valid = ((rel_i <= BLOCK) & (c >= BLOCK - q_start) & (c < seq + BLOCK - q_start))
        kband = kfull[sub * BLOCK: sub * BLOCK + band]
        vband = vfull[sub * BLOCK: sub * BLOCK + band]
        qblk = q_ref[sub * BLOCK:(sub + 1) * BLOCK, :]
        parts = []
        for g in range(4):
            qg = qblk[:, g * LANES:(g + 1) * LANES]
            zero = jnp.zeros_like(qg)
            parts.append(jnp.where(lo_half, qg, zero))
            parts.append(jnp.where(lo_half, zero, qg))
        qs = jnp.concatenate(parts, axis=0)
        s_all = lax.dot_general(qs, kband, _NT, preferred_element_type=F32)
        ps, invs = [], []
        for n, hq in enumerate(heads):
            s = s_all[n * BLOCK:(n + 1) * BLOCK]
            s = jnp.where(valid, s - biases[n], NEG)
            sk = sink_ref[hq] * log2e
            m = jnp.maximum(jnp.max(s, axis=1, keepdims=True), sk)
            e = jnp.exp2(s - m)
            den = jnp.sum(e, axis=1, keepdims=True) + jnp.exp2(sk - m)
            ps.append(e.astype(BF16))
            invs.append(1.0 / den)
        p_all = jnp.concatenate(ps, axis=0)
        o_all = jnp.dot(p_all, vband, preferred_element_type=F32)
        for g in range(4):
            o_lo = o_all[(2 * g) * BLOCK:(2 * g + 1) * BLOCK] * invs[2 * g]
            o_hi = o_all[(2 * g + 1) * BLOCK:(2 * g + 2) * BLOCK] * invs[2 * g + 1]
            o_ref[sub * BLOCK:(sub + 1) * BLOCK, g * LANES:(g + 1) * LANES] = jnp.where(
                lo_half, o_lo, o_hi).astype(o_ref.dtype)


def _attn_b(proj, slopes, sink, tq):
    b, s, _ = proj.shape
    nb = s // BLOCK
    r = tq // BLOCK
    kcol, vcol = COL_KB // LANES, COL_VB // LANES

    def prev_map(col):
        return lambda bb, i: (bb, jnp.maximum(i * r - 1, 0), col)

    def cur_map(col):
        return lambda bb, i: (bb, i, col)

    def next_map(col):
        return lambda bb, i: (bb, jnp.minimum(i * r + r, nb - 1), col)

    return pl.pallas_call(
        functools.partial(_attn_b_kernel, tq=tq, seq=s),
        out_shape=jax.ShapeDtypeStruct((b, s, BRANCH_W), BF16),
        grid=(b, s // tq),
        in_specs=[
            pl.BlockSpec(memory_space=pltpu.SMEM),
            pl.BlockSpec(memory_space=pltpu.SMEM),
            pl.BlockSpec((None, tq, BRANCH_W), lambda bb, i: (bb, i, COL_QB // BRANCH_W)),
            pl.BlockSpec((None, BLOCK, LANES), prev_map(kcol)),
            pl.BlockSpec((None, tq, LANES), cur_map(kcol)),
            pl.BlockSpec((None, BLOCK, LANES), next_map(kcol)),
            pl.BlockSpec((None, BLOCK, LANES), prev_map(vcol)),
            pl.BlockSpec((None, tq, LANES), cur_map(vcol)),
            pl.BlockSpec((None, BLOCK, LANES), next_map(vcol)),
        ],
        out_specs=pl.BlockSpec((None, tq, BRANCH_W), lambda bb, i: (bb, i, 0)),
        compiler_params=pltpu.CompilerParams(
            dimension_semantics=("parallel", "parallel"), vmem_limit_bytes=VMEM_LIMIT),
        name="attn_b",
    )(slopes, sink, proj, proj, proj, proj, proj, proj, proj)


def _attn_c_kernel(q_ref, mem_ref, wkv_ref, o_ref, mk_ref, mv_ref):
    @pl.when(pl.program_id(1) == 0)
    def _():
        kv = jnp.dot(mem_ref[...].astype(BF16), wkv_ref[...], preferred_element_type=F32)
        mk_ref[...] = kv[:, :BRANCH_W].astype(BF16)
        mv_ref[...] = kv[:, BRANCH_W:].astype(BF16)

    scale = MEM_DH ** -0.5
    for h in range(MEM_HEADS):
        cols = slice(h * MEM_DH, (h + 1) * MEM_DH)
        s = lax.dot_general(q_ref[:, cols], mk_ref[:, cols], _NT,
                            preferred_element_type=F32) * scale
        m = jnp.max(s, axis=1, keepdims=True)
        e = jnp.exp(s - m)
        inv = 1.0 / jnp.sum(e, axis=1, keepdims=True)
        o = jnp.dot(e.astype(BF16), mv_ref[:, cols], preferred_element_type=F32)
        o_ref[:, cols] = (o * inv).astype(o_ref.dtype)


def _attn_c(proj, mem, wkv, tq):
    b, s, _ = proj.shape
    return pl.pallas_call(
        _attn_c_kernel,
        out_shape=jax.ShapeDtypeStruct((b, s, BRANCH_W), BF16),
        grid=(b, s // tq),
        in_specs=[
            pl.BlockSpec((None, tq, BRANCH_W), lambda bb, i: (bb, i, COL_QC // BRANCH_W)),
            pl.BlockSpec((None, N_MEM, D_MODEL), lambda bb, i: (bb, 0, 0)),
            pl.BlockSpec((D_MODEL, 2 * BRANCH_W), lambda bb, i: (0, 0)),
        ],
        out_specs=pl.BlockSpec((None, tq, BRANCH_W), lambda bb, i: (bb, i, 0)),
        scratch_shapes=[pltpu.VMEM((N_MEM, BRANCH_W), BF16), pltpu.VMEM((N_MEM, BRANCH_W), BF16)],
        compiler_params=pltpu.CompilerParams(
            dimension_semantics=("parallel", "arbitrary"), vmem_limit_bytes=VMEM_LIMIT),
        name="attn_c",
    )(proj, mem, wkv)


def _merge_kernel(x_ref, oa_ref, ob_ref, oc_ref, wg_ref, bg_ref, wpa_ref, wpb_ref, wpc_ref,
                  wo_ref, g_ref, b_ref, o_ref, *, alpha):
    x = x_ref[...]
    xb = x.astype(BF16)
    merged = None
    for n, (br_ref, wp_ref) in enumerate(((oa_ref, wpa_ref), (ob_ref, wpb_ref), (oc_ref, wpc_ref))):
        cols = slice(n * D_MODEL, (n + 1) * D_MODEL)
        gate = jax.nn.sigmoid(
            jnp.dot(xb, wg_ref[:, cols], preferred_element_type=F32) + bg_ref[:, cols])
        term = gate * jnp.dot(br_ref[...], wp_ref[...], preferred_element_type=F32)
        merged = term if merged is None else merged + term
    y = jnp.dot(merged.astype(BF16), wo_ref[...], preferred_element_type=F32)
    o_ref[...] = _layer_norm(alpha * x + y, g_ref[...], b_ref[...])


def _merge(x2d, oa, ob, oc, wg, bg, wpa, wpb, wpc, wo, g, b, alpha, tm):
    t, d = x2d.shape
    const = lambda i: (0, 0)
    row = lambda i: (i, 0)
    return pl.pallas_call(
        functools.partial(_merge_kernel, alpha=alpha),
        out_shape=jax.ShapeDtypeStruct((t, d), F32),
        grid=(t // tm,),
        in_specs=[
            pl.BlockSpec((tm, d), row),
            pl.BlockSpec((tm, BRANCH_W), row),
            pl.BlockSpec((tm, BRANCH_W), row),
            pl.BlockSpec((tm, BRANCH_W), row),
            pl.BlockSpec((d, 3 * d), const),
            pl.BlockSpec((1, 3 * d), const),
            pl.BlockSpec((BRANCH_W, d), const),
            pl.BlockSpec((BRANCH_W, d), const),
            pl.BlockSpec((BRANCH_W, d), const),
            pl.BlockSpec((d, d), const),
            pl.BlockSpec((1, d), const),
            pl.BlockSpec((1, d), const),
        ],
        out_specs=pl.BlockSpec((tm, d), row),
        compiler_params=pltpu.CompilerParams(
            dimension_semantics=("parallel",), vmem_limit_bytes=VMEM_LIMIT),
        name="merge",
    )(x2d, oa, ob, oc, wg, bg, wpa, wpb, wpc, wo, g, b)


def _sort_network(n):
    pairs = []

    def merge(lo, hi, r):
        step = r * 2
        if step < hi - lo:
            merge(lo, hi, step)
            merge(lo + r, hi, step)
            pairs.extend((k, k + r) for k in range(lo + r, hi - r, step))
        else:
            pairs.append((lo, lo + r))

    def sort(lo, hi):
        if hi - lo >= 1:
            mid = lo + (hi - lo) // 2
            sort(lo, mid)
            sort(mid + 1, hi)
            merge(lo, hi, 1)

    sort(0, n - 1)
    return pairs


_SORT16 = _sort_network(PEER_TOPK)


def _top16_desc(slabs, presorted=False):
    v = list(slabs)
    for a, b in ([] if presorted else _SORT16):
        hi, lo = jnp.maximum(v[a], v[b]), jnp.minimum(v[a], v[b])
        v[a], v[b] = hi, lo
    for shift in (4, 2, 1):
        v = [jnp.maximum(v[k], pltpu.roll(v[PEER_TOPK - 1 - k], shift, 0))
             for k in range(PEER_TOPK)]
        step = PEER_TOPK // 2
        while step >= 1:
            for k in range(PEER_TOPK):
                if k & step == 0:
                    hi, lo = jnp.maximum(v[k], v[k + step]), jnp.minimum(v[k], v[k + step])
                    v[k], v[k + step] = hi, lo
            step //= 2
    return v


def _peer_route(s0, s1):
    n = s0.shape[1]
    top_a = _top16_desc([s0[k * SUBLANES:(k + 1) * SUBLANES] for k in range(N_KEYS // SUBLANES)])
    top_b = _top16_desc([s1[k * SUBLANES:(k + 1) * SUBLANES] for k in range(N_KEYS // SUBLANES)])
    a0, b0 = top_a[0], top_b[0]
    ea = [jnp.exp(t - a0) for t in top_a]
    eb = [jnp.exp(t - b0) for t in top_b]
    sub = lax.broadcasted_iota(jnp.int32, (SUBLANES, n), 0)
    first4 = sub < 4

    def candidates(ea_list):
        col = ea_list[0]
        for s in range(1, 4):
            col = jnp.where(sub == s, ea_list[s], col)
        for s in range(4, 8):
            col = jnp.where(sub == s, eb[s - 4], col)
        out = []
        for v in range(PEER_TOPK):
            c = col * jnp.where(first4, eb[v], ea_list[min(v + 4, PEER_TOPK - 1)])
            if v >= PEER_TOPK - 4:
                c = jnp.where(first4, c, -1.0)
            out.append(c)
        return out

    cand = candidates(ea)
    best = _top16_desc(cand, presorted=True)
    z = best[0]
    for t in best[1:]:
        z = z + t
    inv_z = (2.0 ** -0.5) / z
    theta = best[PEER_TOPK - 1]
    ean = [t * inv_z for t in ea]
    cand_n = candidates(ean)
    thr = None
    for c, cn in zip(cand, cand_n):
        t = jnp.where(c >= theta, cn, jnp.inf)
        thr = t if thr is None else jnp.minimum(thr, t)
    for shift in (4, 2, 1):
        thr = jnp.minimum(thr, pltpu.roll(thr, shift, 0))
    thr = thr[0:1]
    e1n = jnp.exp(s0 - a0[0:1]) * inv_z[0:1]
    e2 = jnp.exp(s1 - b0[0:1])
    psi = jnp.full(s0.shape, float(PEER_TOPK), F32)
    for r in range(PEER_TOPK):
        psi = jnp.where(e1n * eb[r][0:1] >= thr, float(PEER_TOPK - 1 - r), psi)
    code2 = jnp.zeros(s1.shape, F32)
    for r in reversed(range(PEER_TOPK)):
        code2 = jnp.where(s1 >= top_b[r][0:1], float(PEER_TOPK - r), code2)
    return e1n, e2, psi, code2


def _peer_kernel(x1_ref, wpq_ref, keys_ref, u_ref, vt_ref, g_ref, b_ref, o_ref,
                 xb_ref, q_ref, e1_ref, psi_ref, e2_ref, code_ref, acc_ref, wa_ref, *,
                 alpha, ec, lane_chunk):
    j = pl.program_id(1)
    tm = x1_ref.shape[0]
    n1 = ec // N_KEYS
    assert n1 % SUBLANES == 0

    @pl.when(j == 0)
    def _():
        xb = x1_ref[...].astype(BF16)
        xb_ref[...] = xb
        acc_ref[...] = jnp.zeros(acc_ref.shape, F32)
        for c in range(0, q_ref.shape[1], 512):
            q_ref[:, c:c + 512] = jnp.dot(xb, wpq_ref[:, c:c + 512],
                                          preferred_element_type=F32).astype(BF16)
        for h in range(PEER_HEADS):
            s = []
            for half in range(2):
                r = 2 * h + half
                s.append(lax.dot_general(keys_ref[r], q_ref[:, r * N_KEYS:(r + 1) * N_KEYS], _NT,
                                         preferred_element_type=F32))
            for lt in range(0, tm, 2 * LANES):
                cols = slice(lt, lt + 2 * LANES)
                e1n, e2, psi, code2 = _peer_route(s[0][:, cols], s[1][:, cols])
                e1_ref[h, :, cols] = e1n
                psi_ref[h, :, cols] = psi
                e2_ref[h, :, cols] = e2.astype(BF16)
                code_ref[h, :, cols] = code2.astype(BF16)

    def packed_row(tile, ii):
        r16 = jnp.broadcast_to(tile[ii:ii + 1, :], (2 * SUBLANES, tile.shape[1])).astype(BF16)
        return jnp.concatenate([r16] * (N_KEYS // (2 * SUBLANES)), axis=0)

    ht = lax.dot_general(u_ref[...], xb_ref[...], _NT, preferred_element_type=F32)
    base = pl.multiple_of(j * n1, SUBLANES)
    for ii in range(n1):
        rows = slice(ii * N_KEYS, (ii + 1) * N_KEYS)
        for lc in range(0, tm, lane_chunk):
            cols = slice(lc, lc + lane_chunk)
            w = None
            grp = pl.multiple_of(base + (ii // SUBLANES) * SUBLANES, SUBLANES)
            for h in range(PEER_HEADS):
                e1b = packed_row(e1_ref[h, pl.ds(grp, SUBLANES), cols], ii % SUBLANES)
                psib = packed_row(psi_ref[h, pl.ds(grp, SUBLANES), cols], ii % SUBLANES)
                p = e1b * e2_ref[h, :, cols]
                t = jnp.where(code_ref[h, :, cols] > psib, p, jnp.zeros_like(p))
                w = t if w is None else w + t
            hblk = ht[rows, cols]
            act = hblk * (1.0 + lax.erf(hblk))
            wa_ref[rows, cols] = w * act.astype(BF16)
    acc_ref[...] += jnp.dot(vt_ref[...], wa_ref[...], preferred_element_type=F32)

    @pl.when(j == pl.num_programs(1) - 1)
    def _():
        z = alpha * x1_ref[...] + acc_ref[...].T
        o_ref[...] = _layer_norm(z, g_ref[...], b_ref[...])


def _peer(x1, wpq, keys, u, vt, g, b, alpha, tm, ec, lane_chunk):
    t, d = x1.shape
    kern = functools.partial(_peer_kernel, alpha=alpha, ec=ec, lane_chunk=lane_chunk)
    n_chunks = N_EXPERTS // ec
    return pl.pallas_call(
        kern,
        out_shape=jax.ShapeDtypeStruct((t, d), F32),
        grid=(t // tm, n_chunks),
        in_specs=[
            pl.BlockSpec((tm, d), lambda i, j: (i, 0)),
            pl.BlockSpec(wpq.shape, lambda i, j: (0, 0)),
            pl.BlockSpec(keys.shape, lambda i, j: (0, 0, 0)),
            pl.BlockSpec((ec, d), lambda i, j: (j, 0)),
            pl.BlockSpec((d, ec), lambda i, j: (0, j)),
            pl.BlockSpec((1, d), lambda i, j: (0, 0)),
            pl.BlockSpec((1, d), lambda i, j: (0, 0)),
        ],
        out_specs=pl.BlockSpec((tm, d), lambda i, j: (i, 0)),
        scratch_shapes=[
            pltpu.VMEM((tm, d), BF16),
            pltpu.VMEM((tm, wpq.shape[1]), BF16),
            pltpu.VMEM((PEER_HEADS, N_KEYS, tm), F32),
            pltpu.VMEM((PEER_HEADS, N_KEYS, tm), F32),
            pltpu.VMEM((PEER_HEADS, N_KEYS, tm), BF16),
            pltpu.VMEM((PEER_HEADS, N_KEYS, tm), BF16),
            pltpu.VMEM((d, tm), F32),
            pltpu.VMEM((ec, tm), BF16),
        ],
        compiler_params=pltpu.CompilerParams(
            dimension_semantics=("parallel", "arbitrary"), vmem_limit_bytes=VMEM_LIMIT),
        name="peer",
    )(x1, wpq, keys, u, vt, g, b)


def _prep_weights(depth, l, w_in, w_mem_kv, lam_q1, lam_k1, lam_q2, lam_k2, subln_g, sink, w_gate,
                  b_gate, w_pa, w_pb, w_pc, w_o, ln1_g, ln1_b, w_pq, sub_keys, peer_u, peer_v,
                  ln2_g, ln2_b):
    w = w_in[l]
    qb_perm = np.concatenate([np.arange(64) + 64 * hq for g in range(4) for hq in (g, g + 4)])
    qa = w[:, 0:512] * (0.125 * math.log2(math.e))
    ka, va = w[:, 512:1024], w[:, 1024:1536]
    qb = w[:, 1536:2048][:, qb_perm] * (0.125 * math.log2(math.e))
    kb, vb, qc = w[:, 2048:2176], w[:, 2176:2304], w[:, 2304:2816]
    w_proj = jnp.concatenate([qa, ka, va, qb, qc, kb, vb], axis=1).astype(BF16)
    lamp = jnp.stack([lam_q1[l], lam_k1[l], lam_q2[l], lam_k2[l]]).astype(F32)
    row = lambda a: a.astype(F32).reshape(1, -1)
    return dict(
        w_proj=w_proj,
        w_mem_kv=w_mem_kv[l].astype(BF16),
        lamp=lamp,
        subln_g=row(subln_g[l]),
        sink=sink[l].astype(F32),
        w_gate=w_gate[l].astype(BF16),
        b_gate=row(b_gate[l]),
        w_pa=w_pa[l].astype(BF16),
        w_pb=w_pb[l][qb_perm].astype(BF16),
        w_pc=w_pc[l].astype(BF16),
        w_o=w_o[l].astype(BF16),
        ln1_g=row(ln1_g[l]), ln1_b=row(ln1_b[l]),
        w_pq=w_pq[l].astype(BF16),
        keys=sub_keys[l].reshape(2 * PEER_HEADS, N_KEYS, N_KEYS).astype(BF16),
        peer_u=(peer_u[l] * (2.0 ** -0.5)).astype(BF16),
        peer_vt=peer_v[l].T.astype(BF16),
        ln2_g=row(ln2_g[l]), ln2_b=row(ln2_b[l]),
        lam_init=0.8 - 0.6 * math.exp(-0.3 * l),
        alpha=(2.0 * depth) ** 0.25,
    )


def _tile(n, pref):
    t = min(n, pref)
    assert n % t == 0, (n, t)
    return t


def _encoder_layer(x, mem, p):
    b, s, d = x.shape
    t = b * s
    x2d = x.reshape(t, d)
    proj, stats = _proj(x2d, p["w_proj"], _tile(t, 512))
    proj, stats = proj.reshape(b, s, PROJ_COLS), stats.reshape(b, s, LANES)
    slopes_a = jnp.asarray(2.0 ** (-8.0 * np.arange(1, DA_HEADS + 1) / DA_HEADS), F32)
    slope_tab = jnp.asarray(_slope_pieces(2.0 ** (-8.0 * np.arange(1, DA_HEADS + 1) / DA_HEADS)))
    slopes_b = jnp.asarray(2.0 ** (-8.0 * np.arange(1, WA_HEADS + 1) / WA_HEADS), F32)
    oa = _attn_a(proj, stats, slopes_a, slope_tab, p["lamp"], p["subln_g"], p["lam_init"], 256, 4,
                 1024 if s >= 8192 else 512)
    ob = _attn_b(proj, slopes_b, p["sink"], _tile(s, 256))
    oc = _attn_c(proj, mem, p["w_mem_kv"], _tile(s, 512))
    x1 = _merge(x2d, oa.reshape(t, BRANCH_W), ob.reshape(t, BRANCH_W), oc.reshape(t, BRANCH_W),
                p["w_gate"], p["b_gate"], p["w_pa"], p["w_pb"], p["w_pc"], p["w_o"],
                p["ln1_g"], p["ln1_b"], p["alpha"], _tile(t, 512))
    y = _peer(x1, p["w_pq"], p["keys"], p["peer_u"], p["peer_vt"], p["ln2_g"], p["ln2_b"],
              p["alpha"], _tile(t, 512), 2048, 256)
    return y.reshape(b, s, d)


def kernel(x_prompt, x_sample, mem_prompt, mem_sample, w_in, w_mem_kv, lam_q1, lam_k1, lam_q2,
           lam_k2, subln_g, sink, w_gate, b_gate, w_pa, w_pb, w_pc, w_o, ln1_g, ln1_b, w_pq,
           sub_keys, peer_u, peer_v, ln2_g, ln2_b):
    depth = w_in.shape[0]
    y_prompt, y_sample = x_prompt, x_sample
    for l in range(depth):
        p = _prep_weights(depth, l, w_in, w_mem_kv, lam_q1, lam_k1, lam_q2, lam_k2, subln_g, sink,
                          w_gate, b_gate, w_pa, w_pb, w_pc, w_o, ln1_g, ln1_b, w_pq, sub_keys,
                          peer_u, peer_v, ln2_g, ln2_b)
        y_prompt = _encoder_layer(y_prompt, mem_prompt, p)
        y_sample = _encoder_layer(y_sample, mem_sample, p)
    return (y_prompt, y_sample)
```

```python
import functools
import math

import jax
import jax.numpy as jnp
import numpy as np
from jax import lax
from jax.experimental import pallas as pl
from jax.experimental.pallas import tpu as pltpu

F32 = jnp.float32
BF16 = jnp.bfloat16

D_MODEL = 1024
N_MEM = 256
BLOCK = 128
DA_HEADS = 4
WA_HEADS = 8
MEM_HEADS = 4
MEM_DH = 128
BRANCH_W = 512
PEER_HEADS = 8
N_KEYS = 128
N_EXPERTS = N_KEYS * N_KEYS
PEER_TOPK = 16
LN_EPS = 1e-5
NEG = -1e30
LANES = 128
SUBLANES = 8

COL_QA, COL_KA, COL_VA, COL_QB, COL_QC, COL_KB, COL_VB = 0, 512, 1024, 1536, 2048, 2560, 2688
PROJ_COLS = 2816

VMEM_LIMIT = 56 * 1024 * 1024

_NT = (((1,), (1,)), ((), ()))


def _layer_norm(z, g, b):
    mu = jnp.mean(z, axis=-1, keepdims=True)
    zc = z - mu
    var = jnp.mean(zc * zc, axis=-1, keepdims=True)
    return zc * lax.rsqrt(var + LN_EPS) * g + b


def _proj_kernel(x_ref, w_ref, grp_ref, o_ref, st_ref, *, n_chunk):
    xb = x_ref[...].astype(BF16)
    rounded = []
    for c in range(0, o_ref.shape[-1], n_chunk):
        y = jnp.dot(xb, w_ref[:, c:c + n_chunk], preferred_element_type=F32).astype(BF16)
        o_ref[:, c:c + n_chunk] = y
        if c < COL_VA:
            rounded.append(y.astype(F32))
    half = len(rounded) // 2
    q = jnp.concatenate(rounded[:half], axis=1)
    k = jnp.concatenate(rounded[half:], axis=1)
    prod = jnp.concatenate([q * q, k * k, q * k], axis=1).astype(BF16)
    st_ref[...] = jnp.dot(prod, grp_ref[...], preferred_element_type=F32)


def _proj(x2d, w, tm):
    t, d = x2d.shape
    n = w.shape[1]
    width = COL_KA - COL_QA
    grp = np.zeros((3 * width, LANES), np.float32)
    for part in range(3):
        grp[part * width + np.arange(width), part * (width // 64) + np.arange(width) // 64] = 1.0
    grp = jnp.asarray(grp, BF16)
    return pl.pallas_call(
        functools.partial(_proj_kernel, n_chunk=256),
        out_shape=(jax.ShapeDtypeStruct((t, n), BF16), jax.ShapeDtypeStruct((t, LANES), F32)),
        grid=(t // tm,),
        in_specs=[pl.BlockSpec((tm, d), lambda i: (i, 0)),
                  pl.BlockSpec((d, n), lambda i: (0, 0)),
                  pl.BlockSpec(grp.shape, lambda i: (0, 0))],
        out_specs=(pl.BlockSpec((tm, n), lambda i: (i, 0)),
                   pl.BlockSpec((tm, LANES), lambda i: (i, 0))),
        compiler_params=pltpu.CompilerParams(
            dimension_semantics=("parallel",), vmem_limit_bytes=VMEM_LIMIT),
        name="proj",
    )(x2d, w, grp)


def _attn_a_kernel(lo_ref, hi_ref, slopes_ref, lamp_ref, g_ref, q_ref, k_ref, v_ref, o_ref,
                   gm_ref, m_ref, l_ref, acc_ref, *, tq, n_streams, tk, lam_init):
    bb = pl.program_id(0)
    h = pl.program_id(1)
    i = pl.program_id(2)
    tg = n_streams * tq
    mid = tg // tk
    flat = (bb * pl.num_programs(1) + h) * pl.num_programs(2) + i
    lo = lo_ref[flat]
    hi = hi_ref[flat]
    slope2 = slopes_ref[4 * h]
    pieces = [slopes_ref[4 * h + 1 + n] for n in range(3)]
    reps = tk // LANES
    assert tq <= 256 and tk % 256 == 0

    gm_ref[...] = (lax.broadcasted_iota(jnp.int32, (tq, tk), 0)
                   - lax.broadcasted_iota(jnp.int32, (tq, tk), 1)).astype(F32) * slope2
    m_ref[...] = jnp.full(m_ref.shape, -jnp.inf, F32)
    l_ref[...] = jnp.zeros(l_ref.shape, F32)
    acc_ref[...] = jnp.zeros(acc_ref.shape, F32)

    lane = lax.broadcasted_iota(jnp.int32, (tq, LANES), 1)
    row = lax.broadcasted_iota(jnp.int32, (tq, LANES), 0).astype(F32)
    qf = jnp.where(lane < 3, row, 0.0)
    klane = lax.broadcasted_iota(jnp.int32, (tk, LANES), 1)
    col = lax.broadcasted_iota(jnp.int32, (tk, LANES), 0)
    col_lo = (col & 255).astype(F32)
    kf = jnp.where((klane >= 3) & (klane < 6), col_lo,
                   jnp.where((klane >= 6) & (klane < 9), col.astype(F32) - col_lo, 0.0))
    for n in range(3):
        qf = jnp.where((lane == 3 + n) | (lane == 6 + n), -pieces[n], qf)
        kf = jnp.where(klane == n, pieces[n], kf)
    qf = jnp.concatenate([qf, qf], axis=0)
    kf = kf.astype(BF16)
    ones_col = jnp.where(klane == 0, 1.0, 0.0).astype(BF16)

    q2, q_right, q_left = [], [], []
    for st in range(n_streams):
        q = q_ref[st * tq:(st + 1) * tq, :]
        zero = jnp.zeros_like(q)
        q2.append(jnp.concatenate([jnp.where(lane < 64, q, zero),
                                   jnp.where(lane >= 64, q, zero)], axis=0))
        q_right.append(jnp.concatenate([q2[st], qf.astype(BF16)], axis=1))
        q_left.append(jnp.concatenate([q2[st], (-qf).astype(BF16)], axis=1))

    def step(st, lhs, rhs, v_aug, bias, shift):
        x = lax.dot_general(lhs, rhs, _NT, preferred_element_type=F32)
        if bias is not None:
            x = x + jnp.concatenate([bias, bias], axis=0)
        m_prev = m_ref[st]
        m_next = jnp.maximum(m_prev, jnp.max(x, axis=1, keepdims=True) - shift)
        alpha = jnp.exp2(m_prev - m_next)
        sub = m_next + shift
        p = jnp.exp2(x - jnp.concatenate([sub] * reps, axis=1))
        pv = jnp.dot(p.astype(BF16), v_aug, preferred_element_type=F32)
        l_ref[st] = alpha * l_ref[st] + pv[:, LANES:]
        acc_ref[st] = alpha * acc_ref[st] + pv[:, :LANES]
        m_ref[st] = m_next

    def tiles(j):
        ks = pl.multiple_of(j * tk, tk)
        kt = k_ref[pl.ds(ks, tk), :]
        v_aug = jnp.concatenate([v_ref[pl.ds(ks, tk), :], ones_col], axis=1)
        return kt, jnp.concatenate([kt, kf], axis=1), v_aug

    def left(j, carry):
        _, k_aug, v_aug = tiles(j)
        base = lax.convert_element_type(i * tg - j * tk, F32)
        for st in range(n_streams):
            step(st, q_left[st], k_aug, v_aug, None, slope2 * (base + st * tq))
        return carry

    def right(j, carry):
        _, k_aug, v_aug = tiles(j)
        base = lax.convert_element_type(j * tk - i * tg, F32)
        for st in range(n_streams):
            step(st, q_right[st], k_aug, v_aug, None, slope2 * (base - st * tq))
        return carry

    lax.fori_loop(lo, i * mid, left, 0)
    for mj in range(mid):
        kt, k_aug, v_aug = tiles(i * mid + mj)
        for st in range(n_streams):
            off = st * tq - mj * tk
            if off - (tk - 1) >= 0:
                step(st, q_left[st], k_aug, v_aug, None, slope2 * off)
            elif off + (tq - 1) <= 0:
                step(st, q_right[st], k_aug, v_aug, None, slope2 * (-off))
            else:
                step(st, q2[st], kt, v_aug, -jnp.abs(gm_ref[...] + slope2 * off), 0.0)
    lax.fori_loop((i + 1) * mid, hi, right, 0)

    lamp = lamp_ref[...]
    lam = (jnp.exp(jnp.sum(lamp[0:1] * lamp[1:2], axis=1, keepdims=True))
           - jnp.exp(jnp.sum(lamp[2:3] * lamp[3:4], axis=1, keepdims=True)) + lam_init)
    for st in range(n_streams):
        o = acc_ref[st] / jnp.sum(l_ref[st], axis=1, keepdims=True)
        o = o[:tq] - lam * o[tq:]
        ms = jnp.mean(o * o, axis=-1, keepdims=True)
        y = o * lax.rsqrt(ms + LN_EPS) * g_ref[...] * (1.0 - lam_init)
        o_ref[st * tq:(st + 1) * tq, :] = y.astype(o_ref.dtype)


UNDERFLOW_LOG2 = 151.0
NORM_SLACK = 1.004
OWN_SLACK = 2.0 ** -8


def _attn_a_bounds(stats, slopes, tg, tk):
    b, s, _ = stats.shape
    ni, nk, mid = s // tg, s // tk, tg // tk
    ng = 2 * DA_HEADS
    qn = jnp.sqrt(stats[:, :, 0:ng]).reshape(b, ni, tg, DA_HEADS, 2)
    kmax = jnp.max(jnp.sqrt(stats[:, :, ng:2 * ng]), axis=1).reshape(b, DA_HEADS, 2)
    own = stats[:, :, 2 * ng:3 * ng].reshape(b, ni, tg, DA_HEADS, 2)
    upper = jnp.max(qn, axis=2) * kmax[:, None] * NORM_SLACK
    own_low = jnp.min(own, axis=2) - OWN_SLACK * upper
    slack = jnp.max(upper - own_low, axis=-1) + UNDERFLOW_LOG2
    reach = slack / (slopes * math.log2(math.e))
    reach = jnp.minimum(reach, 4.0 * s)
    i0 = (jnp.arange(ni, dtype=F32) * tg)[None, :, None]
    lo = jnp.ceil((i0 + 1.0 - reach) / tk - 1.0)
    lo = jnp.clip(lo, 0, jnp.arange(ni, dtype=F32)[None, :, None] * mid)
    hi = jnp.floor((reach + i0 + tg - 1.0) / tk) + 1.0
    hi = jnp.clip(hi, (jnp.arange(ni, dtype=F32)[None, :, None] + 1.0) * mid, nk)
    flat = lambda a: a.astype(jnp.int32).transpose(0, 2, 1).reshape(-1)
    return flat(lo), flat(hi)


def _slope_pieces(slopes):
    rows = []
    for s in slopes:
        s2 = np.float32(np.float32(s) * np.float32(math.log2(math.e)))
        hi = np.float32(np.asarray(s2, dtype=jnp.bfloat16))
        mid = np.float32(np.asarray(np.float32(s2 - hi), dtype=jnp.bfloat16))
        lo = np.float32(np.asarray(np.float32(s2 - hi - mid), dtype=jnp.bfloat16))
        assert np.float32(np.float32(hi + mid) + lo) == s2
        rows += [s2, hi, mid, lo]
    return np.asarray(rows, np.float32)


def _attn_a(proj, stats, slopes, slope_tab, lamp, subln_g, lam_init, tq, n_streams, tk):
    b, s, _ = proj.shape
    tg = tq * n_streams
    assert tg % tk == 0 and s % tg == 0
    lo, hi = _attn_a_bounds(stats, slopes, tg, tk)
    kern = functools.partial(_attn_a_kernel, tq=tq, n_streams=n_streams, tk=tk, lam_init=lam_init)
    grid_spec = pltpu.PrefetchScalarGridSpec(
        num_scalar_prefetch=2,
        grid=(b, DA_HEADS, s // tg),
        in_specs=[
            pl.BlockSpec(memory_space=pltpu.SMEM),
            pl.BlockSpec((4, 64), lambda bb, h, i, lo, hi: (0, 0)),
            pl.BlockSpec((1, LANES), lambda bb, h, i, lo, hi: (0, 0)),
            pl.BlockSpec((None, tg, LANES), lambda bb, h, i, lo, hi: (bb, i, COL_QA // LANES + h)),
            pl.BlockSpec((None, s, LANES), lambda bb, h, i, lo, hi: (bb, 0, COL_KA // LANES + h)),
            pl.BlockSpec((None, s, LANES), lambda bb, h, i, lo, hi: (bb, 0, COL_VA // LANES + h)),
        ],
        out_specs=pl.BlockSpec((None, tg, LANES), lambda bb, h, i, lo, hi: (bb, i, h)),
        scratch_shapes=[pltpu.VMEM((tq, tk), F32),
                        pltpu.VMEM((n_streams, 2 * tq, LANES), F32),
                        pltpu.VMEM((n_streams, 2 * tq, LANES), F32),
                        pltpu.VMEM((n_streams, 2 * tq, LANES), F32)],
    )
    return pl.pallas_call(
        kern,
        out_shape=jax.ShapeDtypeStruct((b, s, BRANCH_W), BF16),
        grid_spec=grid_spec,
        compiler_params=pltpu.CompilerParams(
            dimension_semantics=("parallel", "parallel", "arbitrary"),
            vmem_limit_bytes=VMEM_LIMIT),
        name="attn_a",
    )(lo, hi, slope_tab, lamp, subln_g, proj, proj, proj)


def _attn_b_kernel(slopes_ref, sink_ref, q_ref, kp_ref, kc_ref, kn_ref, vp_ref, vc_ref, vn_ref,
                   o_ref, *, tq, seq):
    i = pl.program_id(1)
    kfull = jnp.concatenate([kp_ref[...], kc_ref[...], kn_ref[...]], axis=0)
    vfull = jnp.concatenate([vp_ref[...], vc_ref[...], vn_ref[...]], axis=0)
    band = 3 * BLOCK
    r = lax.broadcasted_iota(jnp.int32, (BLOCK, band), 0)
    c = lax.broadcasted_iota(jnp.int32, (BLOCK, band), 1)
    rel_i = jnp.abs(r + BLOCK - c)
    rel = rel_i.astype(F32)
    lane = lax.broadcasted_iota(jnp.int32, (BLOCK, LANES), 1)
    lo_half = lane < 64
    log2e = math.log2(math.e)
    heads = [(n // 2) + 4 * (n % 2) for n in range(8)]
    biases = [(slopes_ref[hq] * log2e) * rel for hq in heads]

    for sub in range(tq // BLOCK):
        q_start = i * tq + sub * BLOCK
        valid = ((rel_i <= BLOCK) & (c >= BLOCK - q_start) & (c < seq + BLOCK - q_start))
        kband = kfull[sub * BLOCK: sub * BLOCK + band]
        vband = vfull[sub * BLOCK: sub * BLOCK + band]
        qblk = q_ref[sub * BLOCK:(sub + 1) * BLOCK, :]
        parts = []
        for g in range(4):
            qg = qblk[:, g * LANES:(g + 1) * LANES]
            zero = jnp.zeros_like(qg)
            parts.append(jnp.where(lo_half, qg, zero))
            parts.append(jnp.where(lo_half, zero, qg))
        qs = jnp.concatenate(parts, axis=0)
        s_all = lax.dot_general(qs, kband, _NT, preferred_element_type=F32)
        ps, invs = [], []
        for n, hq in enumerate(heads):
            s = s_all[n * BLOCK:(n + 1) * BLOCK]
            s = jnp.where(valid, s - biases[n], NEG)
            sk = sink_ref[hq] * log2e
            m = jnp.maximum(jnp.max(s, axis=1, keepdims=True), sk)
            e = jnp.exp2(s - m)
            den = jnp.sum(e, axis=1, keepdims=True) + jnp.exp2(sk - m)
            ps.append(e.astype(BF16))
            invs.append(1.0 / den)
        p_all = jnp.concatenate(ps, axis=0)
        o_all = jnp.dot(p_all, vband, preferred_element_type=F32)
        for g in range(4):
            o_lo = o_all[(2 * g) * BLOCK:(2 * g + 1) * BLOCK] * invs[2 * g]
            o_hi = o_all[(2 * g + 1) * BLOCK:(2 * g + 2) * BLOCK] * invs[2 * g + 1]
            o_ref[sub * BLOCK:(sub + 1) * BLOCK, g * LANES:(g + 1) * LANES] = jnp.where(
                lo_half, o_lo, o_hi).astype(o_ref.dtype)


def _attn_b(proj, slopes, sink, tq):
    b, s, _ = proj.shape
    nb = s // BLOCK
    r = tq // BLOCK
    kcol, vcol = COL_KB // LANES, COL_VB // LANES

    def prev_map(col):
        return lambda bb, i: (bb, jnp.maximum(i * r - 1, 0), col)

    def cur_map(col):
        return lambda bb, i: (bb, i, col)

    def next_map(col):
        return lambda bb, i: (bb, jnp.minimum(i * r + r, nb - 1), col)

    return pl.pallas_call(
        functools.partial(_attn_b_kernel, tq=tq, seq=s),
        out_shape=jax.ShapeDtypeStruct((b, s, BRANCH_W), BF16),
        grid=(b, s // tq),
        in_specs=[
            pl.BlockSpec(memory_space=pltpu.SMEM),
            pl.BlockSpec(memory_space=pltpu.SMEM),
            pl.BlockSpec((None, tq, BRANCH_W), lambda bb, i: (bb, i, COL_QB // BRANCH_W)),
            pl.BlockSpec((None, BLOCK, LANES), prev_map(kcol)),
            pl.BlockSpec((None, tq, LANES), cur_map(kcol)),
            pl.BlockSpec((None, BLOCK, LANES), next_map(kcol)),
            pl.BlockSpec((None, BLOCK, LANES), prev_map(vcol)),
            pl.BlockSpec((None, tq, LANES), cur_map(vcol)),
            pl.BlockSpec((None, BLOCK, LANES), next_map(vcol)),
        ],
        out_specs=pl.BlockSpec((None, tq, BRANCH_W), lambda bb, i: (bb, i, 0)),
        compiler_params=pltpu.CompilerParams(
            dimension_semantics=("parallel", "parallel"), vmem_limit_bytes=VMEM_LIMIT),
        name="attn_b",
    )(slopes, sink, proj, proj, proj, proj, proj, proj, proj)


def _attn_c_kernel(q_ref, mem_ref, wkv_ref, o_ref, mk_ref, mv_ref):
    @pl.when(pl.program_id(1) == 0)
    def _():
        kv = jnp.dot(mem_ref[...].astype(BF16), wkv_ref[...], preferred_element_type=F32)
        mk_ref[...] = kv[:, :BRANCH_W].astype(BF16)
        mv_ref[...] = kv[:, BRANCH_W:].astype(BF16)

    scale = MEM_DH ** -0.5
    for h in range(MEM_HEADS):
        cols = slice(h * MEM_DH, (h + 1) * MEM_DH)
        s = lax.dot_general(q_ref[:, cols], mk_ref[:, cols], _NT,
                            preferred_element_type=F32) * scale
        m = jnp.max(s, axis=1, keepdims=True)
        e = jnp.exp(s - m)
        inv = 1.0 / jnp.sum(e, axis=1, keepdims=True)
        o = jnp.dot(e.astype(BF16), mv_ref[:, cols], preferred_element_type=F32)
        o_ref[:, cols] = (o * inv).astype(o_ref.dtype)


def _attn_c(proj, mem, wkv, tq):
    b, s, _ = proj.shape
    return pl.pallas_call(
        _attn_c_kernel,
        out_shape=jax.ShapeDtypeStruct((b, s, BRANCH_W), BF16),
        grid=(b, s // tq),
        in_specs=[
            pl.BlockSpec((None, tq, BRANCH_W), lambda bb, i: (bb, i, COL_QC // BRANCH_W)),
            pl.BlockSpec((None, N_MEM, D_MODEL), lambda bb, i: (bb, 0, 0)),
            pl.BlockSpec((D_MODEL, 2 * BRANCH_W), lambda bb, i: (0, 0)),
        ],
        out_specs=pl.BlockSpec((None, tq, BRANCH_W), lambda bb, i: (bb, i, 0)),
        scratch_shapes=[pltpu.VMEM((N_MEM, BRANCH_W), BF16), pltpu.VMEM((N_MEM, BRANCH_W), BF16)],
        compiler_params=pltpu.CompilerParams(
            dimension_semantics=("parallel", "arbitrary"), vmem_limit_bytes=VMEM_LIMIT),
        name="attn_c",
    )(proj, mem, wkv)


def _merge_kernel(x_ref, oa_ref, ob_ref, oc_ref, wg_ref, bg_ref, wpa_ref, wpb_ref, wpc_ref,
                  wo_ref, g_ref, b_ref, o_ref, *, alpha):
    x = x_ref[...]
    xb = x.astype(BF16)
    merged = None
    for n, (br_ref, wp_ref) in enumerate(((oa_ref, wpa_ref), (ob_ref, wpb_ref), (oc_ref, wpc_ref))):
        cols = slice(n * D_MODEL, (n + 1) * D_MODEL)
        gate = jax.nn.sigmoid(
            jnp.dot(xb, wg_ref[:, cols], preferred_element_type=F32) + bg_ref[:, cols])
        term = gate * jnp.dot(br_ref[...], wp_ref[...], preferred_element_type=F32)
        merged = term if merged is None else merged + term
    y = jnp.dot(merged.astype(BF16), wo_ref[...], preferred_element_type=F32)
    o_ref[...] = _layer_norm(alpha * x + y, g_ref[...], b_ref[...])


def _merge(x2d, oa, ob, oc, wg, bg, wpa, wpb, wpc, wo, g, b, alpha, tm):
    t, d = x2d.shape
    const = lambda i: (0, 0)
    row = lambda i: (i, 0)
    return pl.pallas_call(
        functools.partial(_merge_kernel, alpha=alpha),
        out_shape=jax.ShapeDtypeStruct((t, d), F32),
        grid=(t // tm,),
        in_specs=[
            pl.BlockSpec((tm, d), row),
            pl.BlockSpec((tm, BRANCH_W), row),
            pl.BlockSpec((tm, BRANCH_W), row),
            pl.BlockSpec((tm, BRANCH_W), row),
            pl.BlockSpec((d, 3 * d), const),
            pl.BlockSpec((1, 3 * d), const),
            pl.BlockSpec((BRANCH_W, d), const),
            pl.BlockSpec((BRANCH_W, d), const),
            pl.BlockSpec((BRANCH_W, d), const),
            pl.BlockSpec((d, d), const),
            pl.BlockSpec((1, d), const),
            pl.BlockSpec((1, d), const),
        ],
        out_specs=pl.BlockSpec((tm, d), row),
        compiler_params=pltpu.CompilerParams(
            dimension_semantics=("parallel",), vmem_limit_bytes=VMEM_LIMIT),
        name="merge",
    )(x2d, oa, ob, oc, wg, bg, wpa, wpb, wpc, wo, g, b)


def _sort_network(n):
    pairs = []

    def merge(lo, hi, r):
        step = r * 2
        if step < hi - lo:
            merge(lo, hi, step)
            merge(lo + r, hi, step)
            pairs.extend((k, k + r) for k in range(lo + r, hi - r, step))
        else:
            pairs.append((lo, lo + r))

    def sort(lo, hi):
        if hi - lo >= 1:
            mid = lo + (hi - lo) // 2
            sort(lo, mid)
            sort(mid + 1, hi)
            merge(lo, hi, 1)

    sort(0, n - 1)
    return pairs


_SORT16 = _sort_network(PEER_TOPK)


def _top16_desc(slabs, presorted=False):
    v = list(slabs)
    for a, b in ([] if presorted else _SORT16):
        hi, lo = jnp.maximum(v[a], v[b]), jnp.minimum(v[a], v[b])
        v[a], v[b] = hi, lo
    for shift in (4, 2, 1):
        v = [jnp.maximum(v[k], pltpu.roll(v[PEER_TOPK - 1 - k], shift, 0))
             for k in range(PEER_TOPK)]
        step = PEER_TOPK // 2
        while step >= 1:
            for k in range(PEER_TOPK):
                if k & step == 0:
                    hi, lo = jnp.maximum(v[k], v[k + step]), jnp.minimum(v[k], v[k + step])
                    v[k], v[k + step] = hi, lo
            step //= 2
    return v


def _peer_route(s0, s1):
    n = s0.shape[1]
    top_a = _top16_desc([s0[k * SUBLANES:(k + 1) * SUBLANES] for k in range(N_KEYS // SUBLANES)])
    top_b = _top16_desc([s1[k * SUBLANES:(k + 1) * SUBLANES] for k in range(N_KEYS // SUBLANES)])
    a0, b0 = top_a[0], top_b[0]
    ea = [jnp.exp(t - a0) for t in top_a]
    eb = [jnp.exp(t - b0) for t in top_b]
    sub = lax.broadcasted_iota(jnp.int32, (SUBLANES, n), 0)
    first4 = sub < 4

    def candidates(ea_list):
        col = ea_list[0]
        for s in range(1, 4):
            col = jnp.where(sub == s, ea_list[s], col)
        for s in range(4, 8):
            col = jnp.where(sub == s, eb[s - 4], col)
        out = []
        for v in range(PEER_TOPK):
            c = col * jnp.where(first4, eb[v], ea_list[min(v + 4, PEER_TOPK - 1)])
            if v >= PEER_TOPK - 4:
                c = jnp.where(first4, c, -1.0)
            out.append(c)
        return out

    cand = candidates(ea)
    best = _top16_desc(cand, presorted=True)
    z = best[0]
    for t in best[1:]:
        z = z + t
    inv_z = (2.0 ** -0.5) / z
    theta = best[PEER_TOPK - 1]
    ean = [t * inv_z for t in ea]
    cand_n = candidates(ean)
    thr = None
    for c, cn in zip(cand, cand_n):
        t = jnp.where(c >= theta, cn, jnp.inf)
        thr = t if thr is None else jnp.minimum(thr, t)
    for shift in (4, 2, 1):
        thr = jnp.minimum(thr, pltpu.roll(thr, shift, 0))
    thr = thr[0:1]
    e1n = jnp.exp(s0 - a0[0:1]) * inv_z[0:1]
    e2 = jnp.exp(s1 - b0[0:1])
    psi = jnp.full(s0.shape, float(PEER_TOPK), F32)
    for r in range(PEER_TOPK):
        psi = jnp.where(e1n * eb[r][0:1] >= thr, float(PEER_TOPK - 1 - r), psi)
    code2 = jnp.zeros(s1.shape, F32)
    for r in reversed(range(PEER_TOPK)):
        code2 = jnp.where(s1 >= top_b[r][0:1], float(PEER_TOPK - r), code2)
    return e1n, e2, psi, code2


def _peer_kernel(x1_ref, wpq_ref, keys_ref, u_ref, vt_ref, g_ref, b_ref, o_ref,
                 xb_ref, q_ref, e1_ref, psi_ref, e2_ref, code_ref, acc_ref, wa_ref, *,
                 alpha, ec, lane_chunk):
    j = pl.program_id(1)
    tm = x1_ref.shape[0]
    n1 = ec // N_KEYS
    assert n1 % SUBLANES == 0

    @pl.when(j == 0)
    def _():
        xb = x1_ref[...].astype(BF16)
        xb_ref[...] = xb
        acc_ref[...] = jnp.zeros(acc_ref.shape, F32)
        for c in range(0, q_ref.shape[1], 512):
            q_ref[:, c:c + 512] = jnp.dot(xb, wpq_ref[:, c:c + 512],
                                          preferred_element_type=F32).astype(BF16)
        for h in range(PEER_HEADS):
            s = []
            for half in range(2):
                r = 2 * h + half
                s.append(lax.dot_general(keys_ref[r], q_ref[:, r * N_KEYS:(r + 1) * N_KEYS], _NT,
                                         preferred_element_type=F32))
            for lt in range(0, tm, 2 * LANES):
                cols = slice(lt, lt + 2 * LANES)
                e1n, e2, psi, code2 = _peer_route(s[0][:, cols], s[1][:, cols])
                e1_ref[h, :, cols] = e1n
                psi_ref[h, :, cols] = psi
                e2_ref[h, :, cols] = e2.astype(BF16)
                code_ref[h, :, cols] = code2.astype(BF16)

    def packed_row(tile, ii):
        r16 = jnp.broadcast_to(tile[ii:ii + 1, :], (2 * SUBLANES, tile.shape[1])).astype(BF16)
        return jnp.concatenate([r16] * (N_KEYS // (2 * SUBLANES)), axis=0)

    ht = lax.dot_general(u_ref[...], xb_ref[...], _NT, preferred_element_type=F32)
    base = pl.multiple_of(j * n1, SUBLANES)
    for ii in range(n1):
        rows = slice(ii * N_KEYS, (ii + 1) * N_KEYS)
        for lc in range(0, tm, lane_chunk):
            cols = slice(lc, lc + lane_chunk)
            w = None
            grp = pl.multiple_of(base + (ii // SUBLANES) * SUBLANES, SUBLANES)
            for h in range(PEER_HEADS):
                e1b = packed_row(e1_ref[h, pl.ds(grp, SUBLANES), cols], ii % SUBLANES)
                psib = packed_row(psi_ref[h, pl.ds(grp, SUBLANES), cols], ii % SUBLANES)
                p = e1b * e2_ref[h, :, cols]
                t = jnp.where(code_ref[h, :, cols] > psib, p, jnp.zeros_like(p))
                w = t if w is None else w + t
            hblk = ht[rows, cols].astype(BF16)
            wa_ref[rows, cols] = w * (hblk * (1.0 + lax.erf(hblk)))
    acc_ref[...] += jnp.dot(vt_ref[...], wa_ref[...], preferred_element_type=F32)

    @pl.when(j == pl.num_programs(1) - 1)
    def _():
        z = alpha * x1_ref[...] + acc_ref[...].T
        o_ref[...] = _layer_norm(z, g_ref[...], b_ref[...])


def _peer(x1, wpq, keys, u, vt, g, b, alpha, tm, ec, lane_chunk):
    t, d = x1.shape
    kern = functools.partial(_peer_kernel, alpha=alpha, ec=ec, lane_chunk=lane_chunk)
    n_chunks = N_EXPERTS // ec
    return pl.pallas_call(
        kern,
        out_shape=jax.ShapeDtypeStruct((t, d), F32),
        grid=(t // tm, n_chunks),
        in_specs=[
            pl.BlockSpec((tm, d), lambda i, j: (i, 0)),
            pl.BlockSpec(wpq.shape, lambda i, j: (0, 0)),
            pl.BlockSpec(keys.shape, lambda i, j: (0, 0, 0)),
            pl.BlockSpec((ec, d), lambda i, j: (j, 0)),
            pl.BlockSpec((d, ec), lambda i, j: (0, j)),
            pl.BlockSpec((1, d), lambda i, j: (0, 0)),
            pl.BlockSpec((1, d), lambda i, j: (0, 0)),
        ],
        out_specs=pl.BlockSpec((tm, d), lambda i, j: (i, 0)),
        scratch_shapes=[
            pltpu.VMEM((tm, d), BF16),
            pltpu.VMEM((tm, wpq.shape[1]), BF16),
            pltpu.VMEM((PEER_HEADS, N_KEYS, tm), F32),
            pltpu.VMEM((PEER_HEADS, N_KEYS, tm), F32),
            pltpu.VMEM((PEER_HEADS, N_KEYS, tm), BF16),
            pltpu.VMEM((PEER_HEADS, N_KEYS, tm), BF16),
            pltpu.VMEM((d, tm), F32),
            pltpu.VMEM((ec, tm), BF16),
        ],
        compiler_params=pltpu.CompilerParams(
            dimension_semantics=("parallel", "arbitrary"), vmem_limit_bytes=VMEM_LIMIT),
        name="peer",
    )(x1, wpq, keys, u, vt, g, b)


def _prep_weights(depth, l, w_in, w_mem_kv, lam_q1, lam_k1, lam_q2, lam_k2, subln_g, sink, w_gate,
                  b_gate, w_pa, w_pb, w_pc, w_o, ln1_g, ln1_b, w_pq, sub_keys, peer_u, peer_v,
                  ln2_g, ln2_b):
    w = w_in[l]
    qb_perm = np.concatenate([np.arange(64) + 64 * hq for g in range(4) for hq in (g, g + 4)])
    qa = w[:, 0:512] * (0.125 * math.log2(math.e))
    ka, va = w[:, 512:1024], w[:, 1024:1536]
    qb = w[:, 1536:2048][:, qb_perm] * (0.125 * math.log2(math.e))
    kb, vb, qc = w[:, 2048:2176], w[:, 2176:2304], w[:, 2304:2816]
    w_proj = jnp.concatenate([qa, ka, va, qb, qc, kb, vb], axis=1).astype(BF16)
    lamp = jnp.stack([lam_q1[l], lam_k1[l], lam_q2[l], lam_k2[l]]).astype(F32)
    row = lambda a: a.astype(F32).reshape(1, -1)
    return dict(
        w_proj=w_proj,
        w_mem_kv=w_mem_kv[l].astype(BF16),
        lamp=lamp,
        subln_g=row(subln_g[l]),
        sink=sink[l].astype(F32),
        w_gate=w_gate[l].astype(BF16),
        b_gate=row(b_gate[l]),
        w_pa=w_pa[l].astype(BF16),
        w_pb=w_pb[l][qb_perm].astype(BF16),
        w_pc=w_pc[l].astype(BF16),
        w_o=w_o[l].astype(BF16),
        ln1_g=row(ln1_g[l]), ln1_b=row(ln1_b[l]),
        w_pq=w_pq[l].astype(BF16),
        keys=sub_keys[l].reshape(2 * PEER_HEADS, N_KEYS, N_KEYS).astype(BF16),
        peer_u=(peer_u[l] * (2.0 ** -0.5)).astype(BF16),
        peer_vt=peer_v[l].T.astype(BF16),
        ln2_g=row(ln2_g[l]), ln2_b=row(ln2_b[l]),
        lam_init=0.8 - 0.6 * math.exp(-0.3 * l),
        alpha=(2.0 * depth) ** 0.25,
    )


def _tile(n, pref):
    t = min(n, pref)
    assert n % t == 0, (n, t)
    return t


def _encoder_layer(x, mem, p):
    b, s, d = x.shape
    t = b * s
    x2d = x.reshape(t, d)
    proj, stats = _proj(x2d, p["w_proj"], _tile(t, 512))
    proj, stats = proj.reshape(b, s, PROJ_COLS), stats.reshape(b, s, LANES)
    slopes_a = jnp.asarray(2.0 ** (-8.0 * np.arange(1, DA_HEADS + 1) / DA_HEADS), F32)
    slope_tab = jnp.asarray(_slope_pieces(2.0 ** (-8.0 * np.arange(1, DA_HEADS + 1) / DA_HEADS)))
    slopes_b = jnp.asarray(2.0 ** (-8.0 * np.arange(1, WA_HEADS + 1) / WA_HEADS), F32)
    oa = _attn_a(proj, stats, slopes_a, slope_tab, p["lamp"], p["subln_g"], p["lam_init"], 256, 4,
                 1024 if s >= 8192 else 512)
    ob = _attn_b(proj, slopes_b, p["sink"], _tile(s, 256))
    oc = _attn_c(proj, mem, p["w_mem_kv"], _tile(s, 512))
    x1 = _merge(x2d, oa.reshape(t, BRANCH_W), ob.reshape(t, BRANCH_W), oc.reshape(t, BRANCH_W),
                p["w_gate"], p["b_gate"], p["w_pa"], p["w_pb"], p["w_pc"], p["w_o"],
                p["ln1_g"], p["ln1_b"], p["alpha"], _tile(t, 512))
    y = _peer(x1, p["w_pq"], p["keys"], p["peer_u"], p["peer_vt"], p["ln2_g"], p["ln2_b"],
              p["alpha"], _tile(t, 512), 2048, 256)
    return y.reshape(b, s, d)


def kernel(x_prompt, x_sample, mem_prompt, mem_sample, w_in, w_mem_kv, lam_q1, lam_k1, lam_q2,
           lam_k2, subln_g, sink, w_gate, b_gate, w_pa, w_pb, w_pc, w_o, ln1_g, ln1_b, w_pq,
           sub_keys, peer_u, peer_v, ln2_g, ln2_b):
    depth = w_in.shape[0]
    y_prompt, y_sample = x_prompt, x_sample
    for l in range(depth):
        p = _prep_weights(depth, l, w_in, w_mem_kv, lam_q1, lam_k1, lam_q2, lam_k2, subln_g, sink,
                          w_gate, b_gate, w_pa, w_pb, w_pc, w_o, ln1_g, ln1_b, w_pq, sub_keys,
                          peer_u, peer_v, ln2_g, ln2_b)
        y_prompt = _encoder_layer(y_prompt, mem_prompt, p)
        y_sample = _encoder_layer(y_sample, mem_sample, p)
    return (y_prompt, y_sample)
```

```python
import functools
import math

import jax
import jax.numpy as jnp
import numpy as np
from jax import lax
from jax.experimental import pallas as pl
from jax.experimental.pallas import tpu as pltpu

F32 = jnp.float32
BF16 = jnp.bfloat16

D_MODEL = 1024
N_MEM = 256
BLOCK = 128
DA_HEADS = 4
WA_HEADS = 8
MEM_HEADS = 4
MEM_DH = 128
BRANCH_W = 512
PEER_HEADS = 8
N_KEYS = 128
N_EXPERTS = N_KEYS * N_KEYS
PEER_TOPK = 16
LN_EPS = 1e-5
NEG = -1e30
LANES = 128
SUBLANES = 8

COL_QA, COL_KA, COL_VA, COL_QB, COL_QC, COL_KB, COL_VB = 0, 512, 1024, 1536, 2048, 2560, 2688
PROJ_COLS = 2816

VMEM_LIMIT = 56 * 1024 * 1024

_NT = (((1,), (1,)), ((), ()))


def _layer_norm(z, g, b):
    mu = jnp.mean(z, axis=-1, keepdims=True)
    zc = z - mu
    var = jnp.mean(zc * zc, axis=-1, keepdims=True)
    return zc * lax.rsqrt(var + LN_EPS) * g + b


def _proj_kernel(x_ref, w_ref, grp_ref, o_ref, st_ref, *, n_chunk):
    xb = x_ref[...].astype(BF16)
    rounded = []
    for c in range(0, o_ref.shape[-1], n_chunk):
        y = jnp.dot(xb, w_ref[:, c:c + n_chunk], preferred_element_type=F32).astype(BF16)
        o_ref[:, c:c + n_chunk] = y
        if c < COL_VA:
            rounded.append(y.astype(F32))
    half = len(rounded) // 2
    q = jnp.concatenate(rounded[:half], axis=1)
    k = jnp.concatenate(rounded[half:], axis=1)
    prod = jnp.concatenate([q * q, k * k, q * k], axis=1).astype(BF16)
    st_ref[...] = jnp.dot(prod, grp_ref[...], preferred_element_type=F32)


def _proj(x2d, w, tm):
    t, d = x2d.shape
    n = w.shape[1]
    width = COL_KA - COL_QA
    grp = np.zeros((3 * width, LANES), np.float32)
    for part in range(3):
        grp[part * width + np.arange(width), part * (width // 64) + np.arange(width) // 64] = 1.0
    grp = jnp.asarray(grp, BF16)
    return pl.pallas_call(
        functools.partial(_proj_kernel, n_chunk=256),
        out_shape=(jax.ShapeDtypeStruct((t, n), BF16), jax.ShapeDtypeStruct((t, LANES), F32)),
        grid=(t // tm,),
        in_specs=[pl.BlockSpec((tm, d), lambda i: (i, 0)),
                  pl.BlockSpec((d, n), lambda i: (0, 0)),
                  pl.BlockSpec(grp.shape, lambda i: (0, 0))],
        out_specs=(pl.BlockSpec((tm, n), lambda i: (i, 0)),
                   pl.BlockSpec((tm, LANES), lambda i: (i, 0))),
        compiler_params=pltpu.CompilerParams(
            dimension_semantics=("parallel",), vmem_limit_bytes=VMEM_LIMIT),
        name="proj",
    )(x2d, w, grp)


def _attn_a_kernel(lo_ref, hi_ref, slopes_ref, lamp_ref, g_ref, q_ref, k_ref, v_ref, o_ref,
                   gm_ref, m_ref, l_ref, acc_ref, *, tq, n_streams, tk, lam_init):
    bb = pl.program_id(0)
    h = pl.program_id(1)
    i = pl.program_id(2)
    tg = n_streams * tq
    mid = tg // tk
    flat = (bb * pl.num_programs(1) + h) * pl.num_programs(2) + i
    lo = lo_ref[flat]
    hi = hi_ref[flat]
    slope2 = slopes_ref[4 * h]
    pieces = [slopes_ref[4 * h + 1 + n] for n in range(3)]
    reps = tk // LANES
    assert tq <= 256 and tk % 256 == 0

    gm_ref[...] = (lax.broadcasted_iota(jnp.int32, (tq, tk), 0)
                   - lax.broadcasted_iota(jnp.int32, (tq, tk), 1)).astype(F32) * slope2
    m_ref[...] = jnp.full(m_ref.shape, -jnp.inf, F32)
    l_ref[...] = jnp.zeros(l_ref.shape, F32)
    acc_ref[...] = jnp.zeros(acc_ref.shape, F32)

    lane = lax.broadcasted_iota(jnp.int32, (tq, LANES), 1)
    row = lax.broadcasted_iota(jnp.int32, (tq, LANES), 0).astype(F32)
    qf = jnp.where(lane < 3, row, 0.0)
    klane = lax.broadcasted_iota(jnp.int32, (tk, LANES), 1)
    col = lax.broadcasted_iota(jnp.int32, (tk, LANES), 0)
    col_lo = (col & 255).astype(F32)
    kf = jnp.where((klane >= 3) & (klane < 6), col_lo,
                   jnp.where((klane >= 6) & (klane < 9), col.astype(F32) - col_lo, 0.0))
    for n in range(3):
        qf = jnp.where((lane == 3 + n) | (lane == 6 + n), -pieces[n], qf)
        kf = jnp.where(klane == n, pieces[n], kf)
    qf = jnp.concatenate([qf, qf], axis=0)
    kf = kf.astype(BF16)
    ones_col = jnp.where(klane == 0, 1.0, 0.0).astype(BF16)

    q2, q_right, q_left = [], [], []
    for st in range(n_streams):
        q = q_ref[st * tq:(st + 1) * tq, :]
        zero = jnp.zeros_like(q)
        q2.append(jnp.concatenate([jnp.where(lane < 64, q, zero),
                                   jnp.where(lane >= 64, q, zero)], axis=0))
        q_right.append(jnp.concatenate([q2[st], qf.astype(BF16)], axis=1))
        q_left.append(jnp.concatenate([q2[st], (-qf).astype(BF16)], axis=1))

    def step(st, lhs, rhs, v_aug, bias, shift):
        x = lax.dot_general(lhs, rhs, _NT, preferred_element_type=F32)
        if bias is not None:
            x = x + jnp.concatenate([bias, bias], axis=0)
        m_prev = m_ref[st]
        m_next = jnp.maximum(m_prev, jnp.max(x, axis=1, keepdims=True) - shift)
        alpha = jnp.exp2(m_prev - m_next)
        sub = m_next + shift
        p = jnp.exp2(x - jnp.concatenate([sub] * reps, axis=1))
        pv = jnp.dot(p.astype(BF16), v_aug, preferred_element_type=F32)
        l_ref[st] = alpha * l_ref[st] + pv[:, LANES:]
        acc_ref[st] = alpha * acc_ref[st] + pv[:, :LANES]
        m_ref[st] = m_next

    def tiles(j):
        ks = pl.multiple_of(j * tk, tk)
        kt = k_ref[pl.ds(ks, tk), :]
        v_aug = jnp.concatenate([v_ref[pl.ds(ks, tk), :], ones_col], axis=1)
        return kt, jnp.concatenate([kt, kf], axis=1), v_aug

    def left(j, carry):
        _, k_aug, v_aug = tiles(j)
        base = lax.convert_element_type(i * tg - j * tk, F32)
        for st in range(n_streams):
            step(st, q_left[st], k_aug, v_aug, None, slope2 * (base + st * tq))
        return carry

    def right(j, carry):
        _, k_aug, v_aug = tiles(j)
        base = lax.convert_element_type(j * tk - i * tg, F32)
        for st in range(n_streams):
            step(st, q_right[st], k_aug, v_aug, None, slope2 * (base - st * tq))
        return carry

    lax.fori_loop(lo, i * mid, left, 0)
    for mj in range(mid):
        kt, k_aug, v_aug = tiles(i * mid + mj)
        for st in range(n_streams):
            off = st * tq - mj * tk
            if off - (tk - 1) >= 0:
                step(st, q_left[st], k_aug, v_aug, None, slope2 * off)
            elif off + (tq - 1) <= 0:
                step(st, q_right[st], k_aug, v_aug, None, slope2 * (-off))
            else:
                step(st, q2[st], kt, v_aug, -jnp.abs(gm_ref[...] + slope2 * off), 0.0)
    lax.fori_loop((i + 1) * mid, hi, right, 0)

    lamp = lamp_ref[...]
    lam = (jnp.exp(jnp.sum(lamp[0:1] * lamp[1:2], axis=1, keepdims=True))
           - jnp.exp(jnp.sum(lamp[2:3] * lamp[3:4], axis=1, keepdims=True)) + lam_init)
    for st in range(n_streams):
        o = acc_ref[st] / jnp.sum(l_ref[st], axis=1, keepdims=True)
        o = o[:tq] - lam * o[tq:]
        ms = jnp.mean(o * o, axis=-1, keepdims=True)
        y = o * lax.rsqrt(ms + LN_EPS) * g_ref[...] * (1.0 - lam_init)
        o_ref[st * tq:(st + 1) * tq, :] = y.astype(o_ref.dtype)


UNDERFLOW_LOG2 = 151.0
NORM_SLACK = 1.004
OWN_SLACK = 2.0 ** -8


def _attn_a_bounds(stats, slopes, tg, tk):
    b, s, _ = stats.shape
    ni, nk, mid = s // tg, s // tk, tg // tk
    ng = 2 * DA_HEADS
    qn = jnp.sqrt(stats[:, :, 0:ng]).reshape(b, ni, tg, DA_HEADS, 2)
    kmax = jnp.max(jnp.sqrt(stats[:, :, ng:2 * ng]), axis=1).reshape(b, DA_HEADS, 2)
    own = stats[:, :, 2 * ng:3 * ng].reshape(b, ni, tg, DA_HEADS, 2)
    upper = jnp.max(qn, axis=2) * kmax[:, None] * NORM_SLACK
    own_low = jnp.min(own, axis=2) - OWN_SLACK * upper
    slack = jnp.max(upper - own_low, axis=-1) + UNDERFLOW_LOG2
    reach = slack / (slopes * math.log2(math.e))
    reach = jnp.minimum(reach, 4.0 * s)
    i0 = (jnp.arange(ni, dtype=F32) * tg)[None, :, None]
    lo = jnp.ceil((i0 + 1.0 - reach) / tk - 1.0)
    lo = jnp.clip(lo, 0, jnp.arange(ni, dtype=F32)[None, :, None] * mid)
    hi = jnp.floor((reach + i0 + tg - 1.0) / tk) + 1.0
    hi = jnp.clip(hi, (jnp.arange(ni, dtype=F32)[None, :, None] + 1.0) * mid, nk)
    flat = lambda a: a.astype(jnp.int32).transpose(0, 2, 1).reshape(-1)
    return flat(lo), flat(hi)


def _slope_pieces(slopes):
    rows = []
    for s in slopes:
        s2 = np.float32(np.float32(s) * np.float32(math.log2(math.e)))
        hi = np.float32(np.asarray(s2, dtype=jnp.bfloat16))
        mid = np.float32(np.asarray(np.float32(s2 - hi), dtype=jnp.bfloat16))
        lo = np.float32(np.asarray(np.float32(s2 - hi - mid), dtype=jnp.bfloat16))
        assert np.float32(np.float32(hi + mid) + lo) == s2
        rows += [s2, hi, mid, lo]
    return np.asarray(rows, np.float32)


def _attn_a(proj, stats, slopes, slope_tab, lamp, subln_g, lam_init, tq, n_streams, tk):
    b, s, _ = proj.shape
    tg = tq * n_streams
    assert tg % tk == 0 and s % tg == 0
    lo, hi = _attn_a_bounds(stats, slopes, tg, tk)
    kern = functools.partial(_attn_a_kernel, tq=tq, n_streams=n_streams, tk=tk, lam_init=lam_init)
    grid_spec = pltpu.PrefetchScalarGridSpec(
        num_scalar_prefetch=2,
        grid=(b, DA_HEADS, s // tg),
        in_specs=[
            pl.BlockSpec(memory_space=pltpu.SMEM),
            pl.BlockSpec((4, 64), lambda bb, h, i, lo, hi: (0, 0)),
            pl.BlockSpec((1, LANES), lambda bb, h, i, lo, hi: (0, 0)),
            pl.BlockSpec((None, tg, LANES), lambda bb, h, i, lo, hi: (bb, i, COL_QA // LANES + h)),
            pl.BlockSpec((None, s, LANES), lambda bb, h, i, lo, hi: (bb, 0, COL_KA // LANES + h)),
            pl.BlockSpec((None, s, LANES), lambda bb, h, i, lo, hi: (bb, 0, COL_VA // LANES + h)),
        ],
        out_specs=pl.BlockSpec((None, tg, LANES), lambda bb, h, i, lo, hi: (bb, i, h)),
        scratch_shapes=[pltpu.VMEM((tq, tk), F32),
                        pltpu.VMEM((n_streams, 2 * tq, LANES), F32),
                        pltpu.VMEM((n_streams, 2 * tq, LANES), F32),
                        pltpu.VMEM((n_streams, 2 * tq, LANES), F32)],
    )
    return pl.pallas_call(
        kern,
        out_shape=jax.ShapeDtypeStruct((b, s, BRANCH_W), BF16),
        grid_spec=grid_spec,
        compiler_params=pltpu.CompilerParams(
            dimension_semantics=("parallel", "parallel", "arbitrary"),
            vmem_limit_bytes=VMEM_LIMIT),
        name="attn_a",
    )(lo, hi, slope_tab, lamp, subln_g, proj, proj, proj)


def _attn_b_kernel(slopes_ref, sink_ref, q_ref, kp_ref, kc_ref, kn_ref, vp_ref, vc_ref, vn_ref,
                   o_ref, *, tq, seq):
    i = pl.program_id(1)
    kfull = jnp.concatenate([kp_ref[...], kc_ref[...], kn_ref[...]], axis=0)
    vfull = jnp.concatenate([vp_ref[...], vc_ref[...], vn_ref[...]], axis=0)
    band = 3 * BLOCK
    r = lax.broadcasted_iota(jnp.int32, (BLOCK, band), 0)
    c = lax.broadcasted_iota(jnp.int32, (BLOCK, band), 1)
    rel_i = jnp.abs(r + BLOCK - c)
    rel = rel_i.astype(F32)
    lane = lax.broadcasted_iota(jnp.int32, (BLOCK, LANES), 1)
    lo_half = lane < 64
    log2e = math.log2(math.e)
    heads = [(n // 2) + 4 * (n % 2) for n in range(8)]
    biases = [(slopes_ref[hq] * log2e) * rel for hq in heads]

    for sub in range(tq // BLOCK):
        q_start = i * tq + sub * BLOCK
        valid = ((rel_i <= BLOCK) & (c >= BLOCK - q_start) & (c < seq + BLOCK - q_start))
        kband = kfull[sub * BLOCK: sub * BLOCK + band]
        vband = vfull[sub * BLOCK: sub * BLOCK + band]
        qblk = q_ref[sub * BLOCK:(sub + 1) * BLOCK, :]
        parts = []
        for g in range(4):
            qg = qblk[:, g * LANES:(g + 1) * LANES]
            zero = jnp.zeros_like(qg)
            parts.append(jnp.where(lo_half, qg, zero))
            parts.append(jnp.where(lo_half, zero, qg))
        qs = jnp.concatenate(parts, axis=0)
        s_all = lax.dot_general(qs, kband, _NT, preferred_element_type=F32)
        ps, invs = [], []
        for n, hq in enumerate(heads):
            s = s_all[n * BLOCK:(n + 1) * BLOCK]
            s = jnp.where(valid, s - biases[n], NEG)
            sk = sink_ref[hq] * log2e
            m = jnp.maximum(jnp.max(s, axis=1, keepdims=True), sk)
            e = jnp.exp2(s - m)
            den = jnp.sum(e, axis=1, keepdims=True) + jnp.exp2(sk - m)
            ps.append(e.astype(BF16))
            invs.append(1.0 / den)
        p_all = jnp.concatenate(ps, axis=0)
        o_all = jnp.dot(p_all, vband, preferred_element_type=F32)
        for g in range(4):
            o_lo = o_all[(2 * g) * BLOCK:(2 * g + 1) * BLOCK] * invs[2 * g]
            o_hi = o_all[(2 * g + 1) * BLOCK:(2 * g + 2) * BLOCK] * invs[2 * g + 1]
            o_ref[sub * BLOCK:(sub + 1) * BLOCK, g * LANES:(g + 1) * LANES] = jnp.where(
                lo_half, o_lo, o_hi).astype(o_ref.dtype)


def _attn_b(proj, slopes, sink, tq):
    b, s, _ = proj.shape
    nb = s // BLOCK
    r = tq // BLOCK
    kcol, vcol = COL_KB // LANES, COL_VB // LANES

    def prev_map(col):
        return lambda bb, i: (bb, jnp.maximum(i * r - 1, 0), col)

    def cur_map(col):
        return lambda bb, i: (bb, i, col)

    def next_map(col):
        return lambda bb, i: (bb, jnp.minimum(i * r + r, nb - 1), col)

    return pl.pallas_call(
        functools.partial(_attn_b_kernel, tq=tq, seq=s),
        out_shape=jax.ShapeDtypeStruct((b, s, BRANCH_W), BF16),
        grid=(b, s // tq),
        in_specs=[
            pl.BlockSpec(memory_space=pltpu.SMEM),
            pl.BlockSpec(memory_space=pltpu.SMEM),
            pl.BlockSpec((None, tq, BRANCH_W), lambda bb, i: (bb, i, COL_QB // BRANCH_W)),
            pl.BlockSpec((None, BLOCK, LANES), prev_map(kcol)),
            pl.BlockSpec((None, tq, LANES), cur_map(kcol)),
            pl.BlockSpec((None, BLOCK, LANES), next_map(kcol)),
            pl.BlockSpec((None, BLOCK, LANES), prev_map(vcol)),
            pl.BlockSpec((None, tq, LANES), cur_map(vcol)),
            pl.BlockSpec((None, BLOCK, LANES), next_map(vcol)),
        ],
        out_specs=pl.BlockSpec((None, tq, BRANCH_W), lambda bb, i: (bb, i, 0)),
        compiler_params=pltpu.CompilerParams(
            dimension_semantics=("parallel", "parallel"), vmem_limit_bytes=VMEM_LIMIT),
        name="attn_b",
    )(slopes, sink, proj, proj, proj, proj, proj, proj, proj)


def _memory_attention(q_ref, mk_ref, mv_ref):
    scale = MEM_DH ** -0.5
    outs = []
    for h in range(MEM_HEADS):
        cols = slice(h * MEM_DH, (h + 1) * MEM_DH)
        s = lax.dot_general(q_ref[:, cols], mk_ref[:, cols], _NT,
                            preferred_element_type=F32) * scale
        m = jnp.max(s, axis=1, keepdims=True)
        e = jnp.exp(s - m)
        inv = 1.0 / jnp.sum(e, axis=1, keepdims=True)
        o = jnp.dot(e.astype(BF16), mv_ref[:, cols], preferred_element_type=F32)
        outs.append((o * inv).astype(BF16))
    return jnp.concatenate(outs, axis=1)


def _merge_kernel(x_ref, oa_ref, ob_ref, qc_ref, mem_ref, wkv_ref, wg_ref, bg_ref, wpa_ref, wpb_ref,
                  wpc_ref, wo_ref, g_ref, b_ref, o_ref, mk_ref, mv_ref, *, alpha, tiles_per_seq):
    @pl.when(pl.program_id(0) % tiles_per_seq == 0)
    def _():
        kv = jnp.dot(mem_ref[...].astype(BF16), wkv_ref[...], preferred_element_type=F32)
        mk_ref[...] = kv[:, :BRANCH_W].astype(BF16)
        mv_ref[...] = kv[:, BRANCH_W:].astype(BF16)

    x = x_ref[...]
    xb = x.astype(BF16)
    branches = (oa_ref[...], ob_ref[...], _memory_attention(qc_ref, mk_ref, mv_ref))
    merged = None
    for n, (br, wp_ref) in enumerate(zip(branches, (wpa_ref, wpb_ref, wpc_ref))):
        cols = slice(n * D_MODEL, (n + 1) * D_MODEL)
        gate = jax.nn.sigmoid(
            jnp.dot(xb, wg_ref[:, cols], preferred_element_type=F32) + bg_ref[:, cols])
        term = gate * jnp.dot(br, wp_ref[...], preferred_element_type=F32)
        merged = term if merged is None else merged + term
    y = jnp.dot(merged.astype(BF16), wo_ref[...], preferred_element_type=F32)
    o_ref[...] = _layer_norm(alpha * x + y, g_ref[...], b_ref[...])


def _merge(x2d, oa, ob, proj2d, mem, wkv, wg, bg, wpa, wpb, wpc, wo, g, b, alpha, tm):
    t, d = x2d.shape
    tiles_per_seq = t // mem.shape[0] // tm
    assert tiles_per_seq * tm * mem.shape[0] == t
    const = lambda i: (0, 0)
    row = lambda i: (i, 0)
    return pl.pallas_call(
        functools.partial(_merge_kernel, alpha=alpha, tiles_per_seq=tiles_per_seq),
        out_shape=jax.ShapeDtypeStruct((t, d), F32),
        grid=(t // tm,),
        in_specs=[
            pl.BlockSpec((tm, d), row),
            pl.BlockSpec((tm, BRANCH_W), row),
            pl.BlockSpec((tm, BRANCH_W), row),
            pl.BlockSpec((tm, BRANCH_W), lambda i: (i, COL_QC // BRANCH_W)),
            pl.BlockSpec((None, N_MEM, D_MODEL), lambda i: (i // tiles_per_seq, 0, 0)),
            pl.BlockSpec((D_MODEL, 2 * BRANCH_W), const),
            pl.BlockSpec((d, 3 * d), const),
            pl.BlockSpec((1, 3 * d), const),
            pl.BlockSpec((BRANCH_W, d), const),
            pl.BlockSpec((BRANCH_W, d), const),
            pl.BlockSpec((BRANCH_W, d), const),
            pl.BlockSpec((d, d), const),
            pl.BlockSpec((1, d), const),
            pl.BlockSpec((1, d), const),
        ],
        out_specs=pl.BlockSpec((tm, d), row),
        scratch_shapes=[pltpu.VMEM((N_MEM, BRANCH_W), BF16), pltpu.VMEM((N_MEM, BRANCH_W), BF16)],
        compiler_params=pltpu.CompilerParams(
            dimension_semantics=("arbitrary",), vmem_limit_bytes=VMEM_LIMIT),
        name="merge",
    )(x2d, oa, ob, proj2d, mem, wkv, wg, bg, wpa, wpb, wpc, wo, g, b)


def _sort_network(n):
    pairs = []

    def merge(lo, hi, r):
        step = r * 2
        if step < hi - lo:
            merge(lo, hi, step)
            merge(lo + r, hi, step)
            pairs.extend((k, k + r) for k in range(lo + r, hi - r, step))
        else:
            pairs.append((lo, lo + r))

    def sort(lo, hi):
        if hi - lo >= 1:
            mid = lo + (hi - lo) // 2
            sort(lo, mid)
            sort(mid + 1, hi)
            merge(lo, hi, 1)

    sort(0, n - 1)
    return pairs


_SORT16 = _sort_network(PEER_TOPK)


def _top16_desc(slabs, presorted=False):
    v = list(slabs)
    for a, b in ([] if presorted else _SORT16):
        hi, lo = jnp.maximum(v[a], v[b]), jnp.minimum(v[a], v[b])
        v[a], v[b] = hi, lo
    for shift in (4, 2, 1):
        v = [jnp.maximum(v[k], pltpu.roll(v[PEER_TOPK - 1 - k], shift, 0))
             for k in range(PEER_TOPK)]
        step = PEER_TOPK // 2
        while step >= 1:
            for k in range(PEER_TOPK):
                if k & step == 0:
                    hi, lo = jnp.maximum(v[k], v[k + step]), jnp.minimum(v[k], v[k + step])
                    v[k], v[k + step] = hi, lo
            step //= 2
    return v


def _peer_route(s0, s1):
    n = s0.shape[1]
    top_a = _top16_desc([s0[k * SUBLANES:(k + 1) * SUBLANES] for k in range(N_KEYS // SUBLANES)])
    top_b = _top16_desc([s1[k * SUBLANES:(k + 1) * SUBLANES] for k in range(N_KEYS // SUBLANES)])
    a0, b0 = top_a[0], top_b[0]
    ea = [jnp.exp(t - a0) for t in top_a]
    eb = [jnp.exp(t - b0) for t in top_b]
    sub = lax.broadcasted_iota(jnp.int32, (SUBLANES, n), 0)
    first4 = sub < 4

    def candidates(ea_list):
        col = ea_list[0]
        for s in range(1, 4):
            col = jnp.where(sub == s, ea_list[s], col)
        for s in range(4, 8):
            col = jnp.where(sub == s, eb[s - 4], col)
        out = []
        for v in range(PEER_TOPK):
            c = col * jnp.where(first4, eb[v], ea_list[min(v + 4, PEER_TOPK - 1)])
            if v >= PEER_TOPK - 4:
                c = jnp.where(first4, c, -1.0)
            out.append(c)
        return out

    cand = candidates(ea)
    best = _top16_desc(cand, presorted=True)
    z = best[0]
    for t in best[1:]:
        z = z + t
    inv_z = (2.0 ** -0.5) / z
    theta = best[PEER_TOPK - 1]
    ean = [t * inv_z for t in ea]
    cand_n = candidates(ean)
    thr = None
    for c, cn in zip(cand, cand_n):
        t = jnp.where(c >= theta, cn, jnp.inf)
        thr = t if thr is None else jnp.minimum(thr, t)
    for shift in (4, 2, 1):
        thr = jnp.minimum(thr, pltpu.roll(thr, shift, 0))
    thr = thr[0:1]
    e1n = jnp.exp(s0 - a0[0:1]) * inv_z[0:1]
    e2 = jnp.exp(s1 - b0[0:1])
    psi = jnp.full(s0.shape, float(PEER_TOPK), F32)
    for r in range(PEER_TOPK):
        psi = jnp.where(e1n * eb[r][0:1] >= thr, float(PEER_TOPK - 1 - r), psi)
    code2 = jnp.zeros(s1.shape, F32)
    for r in reversed(range(PEER_TOPK)):
        code2 = jnp.where(s1 >= top_b[r][0:1], float(PEER_TOPK - r), code2)
    return e1n, e2, psi, code2


def _peer_kernel(x1_ref, wpq_ref, keys_ref, u_ref, vt_ref, g_ref, b_ref, o_ref,
                 xb_ref, q_ref, e1_ref, psi_ref, e2_ref, code_ref, acc_ref, wa_ref, *,
                 alpha, ec, lane_chunk):
    j = pl.program_id(1)
    tm = x1_ref.shape[0]
    n1 = ec // N_KEYS
    assert n1 % SUBLANES == 0

    @pl.when(j == 0)
    def _():
        xb = x1_ref[...].astype(BF16)
        xb_ref[...] = xb
        acc_ref[...] = jnp.zeros(acc_ref.shape, F32)
        for c in range(0, q_ref.shape[1], 512):
            q_ref[:, c:c + 512] = jnp.dot(xb, wpq_ref[:, c:c + 512],
                                          preferred_element_type=F32).astype(BF16)
        for h in range(PEER_HEADS):
            s = []
            for half in range(2):
                r = 2 * h + half
                s.append(lax.dot_general(keys_ref[r], q_ref[:, r * N_KEYS:(r + 1) * N_KEYS], _NT,
                                         preferred_element_type=F32))
            for lt in range(0, tm, 2 * LANES):
                cols = slice(lt, lt + 2 * LANES)
                e1n, e2, psi, code2 = _peer_route(s[0][:, cols], s[1][:, cols])
                e1_ref[h, :, cols] = e1n
                psi_ref[h, :, cols] = psi
                e2_ref[h, :, cols] = e2.astype(BF16)
                code_ref[h, :, cols] = code2.astype(BF16)

    def packed_row(tile, ii):
        r16 = jnp.broadcast_to(tile[ii:ii + 1, :], (2 * SUBLANES, tile.shape[1])).astype(BF16)
        return jnp.concatenate([r16] * (N_KEYS // (2 * SUBLANES)), axis=0)

    ht = lax.dot_general(u_ref[...], xb_ref[...], _NT, preferred_element_type=F32)
    base = pl.multiple_of(j * n1, SUBLANES)
    for ii in range(n1):
        rows = slice(ii * N_KEYS, (ii + 1) * N_KEYS)
        for lc in range(0, tm, lane_chunk):
            cols = slice(lc, lc + lane_chunk)
            w = None
            grp = pl.multiple_of(base + (ii // SUBLANES) * SUBLANES, SUBLANES)
            for h in range(PEER_HEADS):
                e1b = packed_row(e1_ref[h, pl.ds(grp, SUBLANES), cols], ii % SUBLANES)
                psib = packed_row(psi_ref[h, pl.ds(grp, SUBLANES), cols], ii % SUBLANES)
                p = e1b * e2_ref[h, :, cols]
                t = jnp.where(code_ref[h, :, cols] > psib, p, jnp.zeros_like(p))
                w = t if w is None else w + t
            hblk = ht[rows, cols]
            act = hblk * (1.0 + lax.erf(hblk))
            wa_ref[rows, cols] = w * act.astype(BF16)
    acc_ref[...] += jnp.dot(vt_ref[...], wa_ref[...], preferred_element_type=F32)

    @pl.when(j == pl.num_programs(1) - 1)
    def _():
        z = alpha * x1_ref[...] + acc_ref[...].T
        o_ref[...] = _layer_norm(z, g_ref[...], b_ref[...])


def _peer(x1, wpq, keys, u, vt, g, b, alpha, tm, ec, lane_chunk):
    t, d = x1.shape
    kern = functools.partial(_peer_kernel, alpha=alpha, ec=ec, lane_chunk=lane_chunk)
    n_chunks = N_EXPERTS // ec
    return pl.pallas_call(
        kern,
        out_shape=jax.ShapeDtypeStruct((t, d), F32),
        grid=(t // tm, n_chunks),
        in_specs=[
            pl.BlockSpec((tm, d), lambda i, j: (i, 0)),
            pl.BlockSpec(wpq.shape, lambda i, j: (0, 0)),
            pl.BlockSpec(keys.shape, lambda i, j: (0, 0, 0)),
            pl.BlockSpec((ec, d), lambda i, j: (j, 0)),
            pl.BlockSpec((d, ec), lambda i, j: (0, j)),
            pl.BlockSpec((1, d), lambda i, j: (0, 0)),
            pl.BlockSpec((1, d), lambda i, j: (0, 0)),
        ],
        out_specs=pl.BlockSpec((tm, d), lambda i, j: (i, 0)),
        scratch_shapes=[
            pltpu.VMEM((tm, d), BF16),
            pltpu.VMEM((tm, wpq.shape[1]), BF16),
            pltpu.VMEM((PEER_HEADS, N_KEYS, tm), F32),
            pltpu.VMEM((PEER_HEADS, N_KEYS, tm), F32),
            pltpu.VMEM((PEER_HEADS, N_KEYS, tm), BF16),
            pltpu.VMEM((PEER_HEADS, N_KEYS, tm), BF16),
            pltpu.VMEM((d, tm), F32),
            pltpu.VMEM((ec, tm), BF16),
        ],
        compiler_params=pltpu.CompilerParams(
            dimension_semantics=("parallel", "arbitrary"), vmem_limit_bytes=VMEM_LIMIT),
        name="peer",
    )(x1, wpq, keys, u, vt, g, b)


def _prep_weights(depth, l, w_in, w_mem_kv, lam_q1, lam_k1, lam_q2, lam_k2, subln_g, sink, w_gate,
                  b_gate, w_pa, w_pb, w_pc, w_o, ln1_g, ln1_b, w_pq, sub_keys, peer_u, peer_v,
                  ln2_g, ln2_b):
    w = w_in[l]
    qb_perm = np.concatenate([np.arange(64) + 64 * hq for g in range(4) for hq in (g, g + 4)])
    qa = w[:, 0:512] * (0.125 * math.log2(math.e))
    ka, va = w[:, 512:1024], w[:, 1024:1536]
    qb = w[:, 1536:2048][:, qb_perm] * (0.125 * math.log2(math.e))
    kb, vb, qc = w[:, 2048:2176], w[:, 2176:2304], w[:, 2304:2816]
    w_proj = jnp.concatenate([qa, ka, va, qb, qc, kb, vb], axis=1).astype(BF16)
    lamp = jnp.stack([lam_q1[l], lam_k1[l], lam_q2[l], lam_k2[l]]).astype(F32)
    row = lambda a: a.astype(F32).reshape(1, -1)
    return dict(
        w_proj=w_proj,
        w_mem_kv=w_mem_kv[l].astype(BF16),
        lamp=lamp,
        subln_g=row(subln_g[l]),
        sink=sink[l].astype(F32),
        w_gate=w_gate[l].astype(BF16),
        b_gate=row(b_gate[l]),
        w_pa=w_pa[l].astype(BF16),
        w_pb=w_pb[l][qb_perm].astype(BF16),
        w_pc=w_pc[l].astype(BF16),
        w_o=w_o[l].astype(BF16),
        ln1_g=row(ln1_g[l]), ln1_b=row(ln1_b[l]),
        w_pq=w_pq[l].astype(BF16),
        keys=sub_keys[l].reshape(2 * PEER_HEADS, N_KEYS, N_KEYS).astype(BF16),
        peer_u=(peer_u[l] * (2.0 ** -0.5)).astype(BF16),
        peer_vt=peer_v[l].T.astype(BF16),
        ln2_g=row(ln2_g[l]), ln2_b=row(ln2_b[l]),
        lam_init=0.8 - 0.6 * math.exp(-0.3 * l),
        alpha=(2.0 * depth) ** 0.25,
    )


def _tile(n, pref):
    t = min(n, pref)
    assert n % t == 0, (n, t)
    return t


def _encoder_layer(x, mem, p):
    b, s, d = x.shape
    t = b * s
    x2d = x.reshape(t, d)
    proj, stats = _proj(x2d, p["w_proj"], _tile(t, 512))
    proj, stats = proj.reshape(b, s, PROJ_COLS), stats.reshape(b, s, LANES)
    slopes_a = jnp.asarray(2.0 ** (-8.0 * np.arange(1, DA_HEADS + 1) / DA_HEADS), F32)
    slope_tab = jnp.asarray(_slope_pieces(2.0 ** (-8.0 * np.arange(1, DA_HEADS + 1) / DA_HEADS)))
    slopes_b = jnp.asarray(2.0 ** (-8.0 * np.arange(1, WA_HEADS + 1) / WA_HEADS), F32)
    oa = _attn_a(proj, stats, slopes_a, slope_tab, p["lamp"], p["subln_g"], p["lam_init"], 256, 4,
                 1024 if s >= 8192 else 512)
    ob = _attn_b(proj, slopes_b, p["sink"], _tile(s, 256))
    x1 = _merge(x2d, oa.reshape(t, BRANCH_W), ob.reshape(t, BRANCH_W),
                proj.reshape(t, PROJ_COLS), mem, p["w_mem_kv"], p["w_gate"], p["b_gate"], p["w_pa"], p["w_pb"], p["w_pc"], p["w_o"],
                p["ln1_g"], p["ln1_b"], p["alpha"], _tile(t, 512))
    y = _peer(x1, p["w_pq"], p["keys"], p["peer_u"], p["peer_vt"], p["ln2_g"], p["ln2_b"],
              p["alpha"], _tile(t, 512), 2048, 256)
    return y.reshape(b, s, d)


def kernel(x_prompt, x_sample, mem_prompt, mem_sample, w_in, w_mem_kv, lam_q1, lam_k1, lam_q2,
           lam_k2, subln_g, sink, w_gate, b_gate, w_pa, w_pb, w_pc, w_o, ln1_g, ln1_b, w_pq,
           sub_keys, peer_u, peer_v, ln2_g, ln2_b):
    depth = w_in.shape[0]
    y_prompt, y_sample = x_prompt, x_sample
    for l in range(depth):
        p = _prep_weights(depth, l, w_in, w_mem_kv, lam_q1, lam_k1, lam_q2, lam_k2, subln_g, sink,
                          w_gate, b_gate, w_pa, w_pb, w_pc, w_o, ln1_g, ln1_b, w_pq, sub_keys,
                          peer_u, peer_v, ln2_g, ln2_b)
        y_prompt = _encoder_layer(y_prompt, mem_prompt, p)
        y_sample = _encoder_layer(y_sample, mem_sample, p)
    return (y_prompt, y_sample)
```

```python
import functools
import math

import jax
import jax.numpy as jnp
import numpy as np
from jax import lax
from jax.experimental import pallas as pl
from jax.experimental.pallas import tpu as pltpu

F32 = jnp.float32
BF16 = jnp.bfloat16

D_MODEL = 1024
N_MEM = 256
BLOCK = 128
DA_HEADS = 4
WA_HEADS = 8
MEM_HEADS = 4
MEM_DH = 128
BRANCH_W = 512
PEER_HEADS = 8
N_KEYS = 128
N_EXPERTS = N_KEYS * N_KEYS
PEER_TOPK = 16
LN_EPS = 1e-5
NEG = -1e30
LANES = 128
SUBLANES = 8

COL_QA, COL_KA, COL_VA, COL_QB, COL_QC, COL_KB, COL_VB = 0, 512, 1024, 1536, 2048, 2560, 2688
PROJ_COLS = 2816

VMEM_LIMIT = 56 * 1024 * 1024

_NT = (((1,), (1,)), ((), ()))


def _layer_norm(z, g, b):
    mu = jnp.mean(z, axis=-1, keepdims=True)
    zc = z - mu
    var = jnp.mean(zc * zc, axis=-1, keepdims=True)
    return zc * lax.rsqrt(var + LN_EPS) * g + b


def _proj_kernel(x_ref, w_ref, grp_ref, o_ref, st_ref, *, n_chunk):
    xb = x_ref[...].astype(BF16)
    rounded = []
    for c in range(0, o_ref.shape[-1], n_chunk):
        y = jnp.dot(xb, w_ref[:, c:c + n_chunk], preferred_element_type=F32).astype(BF16)
        o_ref[:, c:c + n_chunk] = y
        if c < COL_VA:
            rounded.append(y.astype(F32))
    half = len(rounded) // 2
    q = jnp.concatenate(rounded[:half], axis=1)
    k = jnp.concatenate(rounded[half:], axis=1)
    prod = jnp.concatenate([q * q, k * k, q * k], axis=1).astype(BF16)
    st_ref[...] = jnp.dot(prod, grp_ref[...], preferred_element_type=F32)


def _proj(x2d, w, tm):
    t, d = x2d.shape
    n = w.shape[1]
    width = COL_KA - COL_QA
    grp = np.zeros((3 * width, LANES), np.float32)
    for part in range(3):
        grp[part * width + np.arange(width), part * (width // 64) + np.arange(width) // 64] = 1.0
    grp = jnp.asarray(grp, BF16)
    return pl.pallas_call(
        functools.partial(_proj_kernel, n_chunk=256),
        out_shape=(jax.ShapeDtypeStruct((t, n), BF16), jax.ShapeDtypeStruct((t, LANES), F32)),
        grid=(t // tm,),
        in_specs=[pl.BlockSpec((tm, d), lambda i: (i, 0)),
                  pl.BlockSpec((d, n), lambda i: (0, 0)),
                  pl.BlockSpec(grp.shape, lambda i: (0, 0))],
        out_specs=(pl.BlockSpec((tm, n), lambda i: (i, 0)),
                   pl.BlockSpec((tm, LANES), lambda i: (i, 0))),
        compiler_params=pltpu.CompilerParams(
            dimension_semantics=("parallel",), vmem_limit_bytes=VMEM_LIMIT),
        name="proj",
    )(x2d, w, grp)


def _attn_a_kernel(lo_ref, hi_ref, slopes_ref, lamp_ref, g_ref, q_ref, k_ref, v_ref, o_ref,
                   gm_ref, m_ref, l_ref, acc_ref, *, tq, n_streams, tk, lam_init):
    bb = pl.program_id(0)
    h = pl.program_id(1)
    i = pl.program_id(2)
    tg = n_streams * tq
    mid = tg // tk
    flat = (bb * pl.num_programs(1) + h) * pl.num_programs(2) + i
    lo = lo_ref[flat]
    hi = hi_ref[flat]
    slope2 = slopes_ref[4 * h]
    pieces = [slopes_ref[4 * h + 1 + n] for n in range(3)]
    reps = tk // LANES
    assert tq <= 256 and tk % 256 == 0

    gm_ref[...] = (lax.broadcasted_iota(jnp.int32, (tq, tk), 0)
                   - lax.broadcasted_iota(jnp.int32, (tq, tk), 1)).astype(F32) * slope2
    m_ref[...] = jnp.full(m_ref.shape, -jnp.inf, F32)
    l_ref[...] = jnp.zeros(l_ref.shape, F32)
    acc_ref[...] = jnp.zeros(acc_ref.shape, F32)

    lane = lax.broadcasted_iota(jnp.int32, (tq, LANES), 1)
    row = lax.broadcasted_iota(jnp.int32, (tq, LANES), 0).astype(F32)
    qf = jnp.where(lane < 3, row, 0.0)
    klane = lax.broadcasted_iota(jnp.int32, (tk, LANES), 1)
    col = lax.broadcasted_iota(jnp.int32, (tk, LANES), 0)
    col_lo = (col & 255).astype(F32)
    kf = jnp.where((klane >= 3) & (klane < 6), col_lo,
                   jnp.where((klane >= 6) & (klane < 9), col.astype(F32) - col_lo, 0.0))
    for n in range(3):
        qf = jnp.where((lane == 3 + n) | (lane == 6 + n), -pieces[n], qf)
        kf = jnp.where(klane == n, pieces[n], kf)
    qf = jnp.concatenate([qf, qf], axis=0)
    kf = kf.astype(BF16)
    ones_col = jnp.where(klane == 0, 1.0, 0.0).astype(BF16)

    q2, q_right, q_left = [], [], []
    for st in range(n_streams):
        q = q_ref[st * tq:(st + 1) * tq, :]
        zero = jnp.zeros_like(q)
        q2.append(jnp.concatenate([jnp.where(lane < 64, q, zero),
                                   jnp.where(lane >= 64, q, zero)], axis=0))
        q_right.append(jnp.concatenate([q2[st], qf.astype(BF16)], axis=1))
        q_left.append(jnp.concatenate([q2[st], (-qf).astype(BF16)], axis=1))

    def step(st, lhs, rhs, v_aug, bias, shift):
        x = lax.dot_general(lhs, rhs, _NT, preferred_element_type=F32)
        if bias is not None:
            x = x + jnp.concatenate([bias, bias], axis=0)
        m_prev = m_ref[st]
        m_next = jnp.maximum(m_prev, jnp.max(x, axis=1, keepdims=True) - shift)
        alpha = jnp.exp2(m_prev - m_next)
        sub = m_next + shift
        p = jnp.exp2(x - jnp.concatenate([sub] * reps, axis=1))
        pv = jnp.dot(p.astype(BF16), v_aug, preferred_element_type=F32)
        l_ref[st] = alpha * l_ref[st] + pv[:, LANES:]
        acc_ref[st] = alpha * acc_ref[st] + pv[:, :LANES]
        m_ref[st] = m_next

    def tiles(j):
        ks = pl.multiple_of(j * tk, tk)
        kt = k_ref[pl.ds(ks, tk), :]
        v_aug = jnp.concatenate([v_ref[pl.ds(ks, tk), :], ones_col], axis=1)
        return kt, jnp.concatenate([kt, kf], axis=1), v_aug

    def left(j, carry):
        _, k_aug, v_aug = tiles(j)
        base = lax.convert_element_type(i * tg - j * tk, F32)
        for st in range(n_streams):
            step(st, q_left[st], k_aug, v_aug, None, slope2 * (base + st * tq))
        return carry

    def right(j, carry):
        _, k_aug, v_aug = tiles(j)
        base = lax.convert_element_type(j * tk - i * tg, F32)
        for st in range(n_streams):
            step(st, q_right[st], k_aug, v_aug, None, slope2 * (base - st * tq))
        return carry

    lax.fori_loop(lo, i * mid, left, 0)
    for mj in range(mid):
        kt, k_aug, v_aug = tiles(i * mid + mj)
        for st in range(n_streams):
            off = st * tq - mj * tk
            if off - (tk - 1) >= 0:
                step(st, q_left[st], k_aug, v_aug, None, slope2 * off)
            elif off + (tq - 1) <= 0:
                step(st, q_right[st], k_aug, v_aug, None, slope2 * (-off))
            else:
                step(st, q2[st], kt, v_aug, -jnp.abs(gm_ref[...] + slope2 * off), 0.0)
    lax.fori_loop((i + 1) * mid, hi, right, 0)

    lamp = lamp_ref[...]
    lam = (jnp.exp(jnp.sum(lamp[0:1] * lamp[1:2], axis=1, keepdims=True))
           - jnp.exp(jnp.sum(lamp[2:3] * lamp[3:4], axis=1, keepdims=True)) + lam_init)
    for st in range(n_streams):
        o = acc_ref[st] / jnp.sum(l_ref[st], axis=1, keepdims=True)
        o = o[:tq] - lam * o[tq:]
        ms = jnp.mean(o * o, axis=-1, keepdims=True)
        y = o * lax.rsqrt(ms + LN_EPS) * g_ref[...] * (1.0 - lam_init)
        o_ref[st * tq:(st + 1) * tq, :] = y.astype(o_ref.dtype)


UNDERFLOW_LOG2 = 151.0
NORM_SLACK = 1.004
OWN_SLACK = 2.0 ** -8


def _attn_a_bounds(stats, slopes, tg, tk):
    b, s, _ = stats.shape
    ni, nk, mid = s // tg, s // tk, tg // tk
    ng = 2 * DA_HEADS
    qn = jnp.sqrt(stats[:, :, 0:ng]).reshape(b, ni, tg, DA_HEADS, 2)
    kmax = jnp.max(jnp.sqrt(stats[:, :, ng:2 * ng]), axis=1).reshape(b, DA_HEADS, 2)
    own = stats[:, :, 2 * ng:3 * ng].reshape(b, ni, tg, DA_HEADS, 2)
    upper = jnp.max(qn, axis=2) * kmax[:, None] * NORM_SLACK
    own_low = jnp.min(own, axis=2) - OWN_SLACK * upper
    slack = jnp.max(upper - own_low, axis=-1) + UNDERFLOW_LOG2
    reach = slack / (slopes * math.log2(math.e))
    reach = jnp.minimum(reach, 4.0 * s)
    i0 = (jnp.arange(ni, dtype=F32) * tg)[None, :, None]
    lo = jnp.ceil((i0 + 1.0 - reach) / tk - 1.0)
    lo = jnp.clip(lo, 0, jnp.arange(ni, dtype=F32)[None, :, None] * mid)
    hi = jnp.floor((reach + i0 + tg - 1.0) / tk) + 1.0
    hi = jnp.clip(hi, (jnp.arange(ni, dtype=F32)[None, :, None] + 1.0) * mid, nk)
    flat = lambda a: a.astype(jnp.int32).transpose(0, 2, 1).reshape(-1)
    return flat(lo), flat(hi)


def _slope_pieces(slopes):
    rows = []
    for s in slopes:
        s2 = np.float32(np.float32(s) * np.float32(math.log2(math.e)))
        hi = np.float32(np.asarray(s2, dtype=jnp.bfloat16))
        mid = np.float32(np.asarray(np.float32(s2 - hi), dtype=jnp.bfloat16))
        lo = np.float32(np.asarray(np.float32(s2 - hi - mid), dtype=jnp.bfloat16))
        assert np.float32(np.float32(hi + mid) + lo) == s2
        rows += [s2, hi, mid, lo]
    return np.asarray(rows, np.float32)


def _attn_a(proj, stats, slopes, slope_tab, lamp, subln_g, lam_init, tq, n_streams, tk):
    b, s, _ = proj.shape
    tg = tq * n_streams
    assert tg % tk == 0 and s % tg == 0
    lo, hi = _attn_a_bounds(stats, slopes, tg, tk)
    kern = functools.partial(_attn_a_kernel, tq=tq, n_streams=n_streams, tk=tk, lam_init=lam_init)
    grid_spec = pltpu.PrefetchScalarGridSpec(
        num_scalar_prefetch=2,
        grid=(b, DA_HEADS, s // tg),
        in_specs=[
            pl.BlockSpec(memory_space=pltpu.SMEM),
            pl.BlockSpec((4, 64), lambda bb, h, i, lo, hi: (0, 0)),
            pl.BlockSpec((1, LANES), lambda bb, h, i, lo, hi: (0, 0)),
            pl.BlockSpec((None, tg, LANES), lambda bb, h, i, lo, hi: (bb, i, COL_QA // LANES + h)),
            pl.BlockSpec((None, s, LANES), lambda bb, h, i, lo, hi: (bb, 0, COL_KA // LANES + h)),
            pl.BlockSpec((None, s, LANES), lambda bb, h, i, lo, hi: (bb, 0, COL_VA // LANES + h)),
        ],
        out_specs=pl.BlockSpec((None, tg, LANES), lambda bb, h, i, lo, hi: (bb, i, h)),
        scratch_shapes=[pltpu.VMEM((tq, tk), F32),
                        pltpu.VMEM((n_streams, 2 * tq, LANES), F32),
                        pltpu.VMEM((n_streams, 2 * tq, LANES), F32),
                        pltpu.VMEM((n_streams, 2 * tq, LANES), F32)],
    )
    return pl.pallas_call(
        kern,
        out_shape=jax.ShapeDtypeStruct((b, s, BRANCH_W), BF16),
        grid_spec=grid_spec,
        compiler_params=pltpu.CompilerParams(
            dimension_semantics=("parallel", "parallel", "arbitrary"),
            vmem_limit_bytes=VMEM_LIMIT),
        name="attn_a",
    )(lo, hi, slope_tab, lamp, subln_g, proj, proj, proj)


def _attn_b_kernel(slopes_ref, sink_ref, q_ref, kp_ref, kc_ref, kn_ref, vp_ref, vc_ref, vn_ref,
                   o_ref, *, tq, seq):
    i = pl.program_id(1)
    kfull = jnp.concatenate([kp_ref[...], kc_ref[...], kn_ref[...]], axis=0)
    vfull = jnp.concatenate([vp_ref[...], vc_ref[...], vn_ref[...]], axis=0)
    band = 3 * BLOCK
    r = lax.broadcasted_iota(jnp.int32, (BLOCK, band), 0)
    c = lax.broadcasted_iota(jnp.int32, (BLOCK, band), 1)
    rel_i = jnp.abs(r + BLOCK - c)
    rel = rel_i.astype(F32)
    lane = lax.broadcasted_iota(jnp.int32, (BLOCK, LANES), 1)
    lo_half = lane < 64
    log2e = math.log2(math.e)
    heads = [(n // 2) + 4 * (n % 2) for n in range(8)]
    biases = [(slopes_ref[hq] * log2e) * rel for hq in heads]

    for sub in range(tq // BLOCK):
        q_start = i * tq + sub * BLOCK
        valid = ((rel_i <= BLOCK) & (c >= BLOCK - q_start) & (c < seq + BLOCK - q_start))
        kband = kfull[sub * BLOCK: sub * BLOCK + band]
        vband = vfull[sub * BLOCK: sub * BLOCK + band]
        qblk = q_ref[sub * BLOCK:(sub + 1) * BLOCK, :]
        parts = []
        for g in range(4):
            qg = qblk[:, g * LANES:(g + 1) * LANES]
            zero = jnp.zeros_like(qg)
            parts.append(jnp.where(lo_half, qg, zero))
            parts.append(jnp.where(lo_half, zero, qg))
        qs = jnp.concatenate(parts, axis=0)
        s_all = lax.dot_general(qs, kband, _NT, preferred_element_type=F32)
        ps, invs = [], []
        for n, hq in enumerate(heads):
            s = s_all[n * BLOCK:(n + 1) * BLOCK]
            s = jnp.where(valid, s - biases[n], NEG)
            sk = sink_ref[hq] * log2e
            m = jnp.maximum(jnp.max(s, axis=1, keepdims=True), sk)
            e = jnp.exp2(s - m)
            den = jnp.sum(e, axis=1, keepdims=True) + jnp.exp2(sk - m)
            ps.append(e.astype(BF16))
            invs.append(1.0 / den)
        p_all = jnp.concatenate(ps, axis=0)
        o_all = jnp.dot(p_all, vband, preferred_element_type=F32)
        for g in range(4):
            o_lo = o_all[(2 * g) * BLOCK:(2 * g + 1) * BLOCK] * invs[2 * g]
            o_hi = o_all[(2 * g + 1) * BLOCK:(2 * g + 2) * BLOCK] * invs[2 * g + 1]
            o_ref[sub * BLOCK:(sub + 1) * BLOCK, g * LANES:(g + 1) * LANES] = jnp.where(
                lo_half, o_lo, o_hi).astype(o_ref.dtype)


def _attn_b(proj, slopes, sink, tq):
    b, s, _ = proj.shape
    nb = s // BLOCK
    r = tq // BLOCK
    kcol, vcol = COL_KB // LANES, COL_VB // LANES

    def prev_map(col):
        return lambda bb, i: (bb, jnp.maximum(i * r - 1, 0), col)

    def cur_map(col):
        return lambda bb, i: (bb, i, col)

    def next_map(col):
        return lambda bb, i: (bb, jnp.minimum(i * r + r, nb - 1), col)

    return pl.pallas_call(
        functools.partial(_attn_b_kernel, tq=tq, seq=s),
        out_shape=jax.ShapeDtypeStruct((b, s, BRANCH_W), BF16),
        grid=(b, s // tq),
        in_specs=[
            pl.BlockSpec(memory_space=pltpu.SMEM),
            pl.BlockSpec(memory_space=pltpu.SMEM),
            pl.BlockSpec((None, tq, BRANCH_W), lambda bb, i: (bb, i, COL_QB // BRANCH_W)),
            pl.BlockSpec((None, BLOCK, LANES), prev_map(kcol)),
            pl.BlockSpec((None, tq, LANES), cur_map(kcol)),
            pl.BlockSpec((None, BLOCK, LANES), next_map(kcol)),
            pl.BlockSpec((None, BLOCK, LANES), prev_map(vcol)),
            pl.BlockSpec((None, tq, LANES), cur_map(vcol)),
            pl.BlockSpec((None, BLOCK, LANES), next_map(vcol)),
        ],
        out_specs=pl.BlockSpec((None, tq, BRANCH_W), lambda bb, i: (bb, i, 0)),
        compiler_params=pltpu.CompilerParams(
            dimension_semantics=("parallel", "parallel"), vmem_limit_bytes=VMEM_LIMIT),
        name="attn_b",
    )(slopes, sink, proj, proj, proj, proj, proj, proj, proj)


def _memory_attention(q_ref, mk_ref, mv_ref):
    scale = MEM_DH ** -0.5
    outs = []
    for h in range(MEM_HEADS):
        cols = slice(h * MEM_DH, (h + 1) * MEM_DH)
        s = lax.dot_general(q_ref[:, cols], mk_ref[:, cols], _NT,
                            preferred_element_type=F32) * scale
        m = jnp.max(s, axis=1, keepdims=True)
        e = jnp.exp(s - m)
        inv = 1.0 / jnp.sum(e, axis=1, keepdims=True)
        o = jnp.dot(e.astype(BF16), mv_ref[:, cols], preferred_element_type=F32)
        outs.append((o * inv).astype(BF16))
    return jnp.concatenate(outs, axis=1)


def _merge_kernel(x_ref, oa_ref, ob_ref, qc_ref, mem_ref, wkv_ref, wg_ref, bg_ref, wpa_ref, wpb_ref,
                  wpc_ref, wo_ref, g_ref, b_ref, o_ref, mk_ref, mv_ref, *, alpha, tiles_per_seq):
    @pl.when(pl.program_id(0) % tiles_per_seq == 0)
    def _():
        kv = jnp.dot(mem_ref[...].astype(BF16), wkv_ref[...], preferred_element_type=F32)
        mk_ref[...] = kv[:, :BRANCH_W].astype(BF16)
        mv_ref[...] = kv[:, BRANCH_W:].astype(BF16)

    x = x_ref[...]
    xb = x.astype(BF16)
    branches = (oa_ref[...], ob_ref[...], _memory_attention(qc_ref, mk_ref, mv_ref))
    merged = None
    for n, (br, wp_ref) in enumerate(zip(branches, (wpa_ref, wpb_ref, wpc_ref))):
        cols = slice(n * D_MODEL, (n + 1) * D_MODEL)
        gate = jax.nn.sigmoid(
            jnp.dot(xb, wg_ref[:, cols], preferred_element_type=F32) + bg_ref[:, cols])
        term = gate * jnp.dot(br, wp_ref[...], preferred_element_type=F32)
        merged = term if merged is None else merged + term
    y = jnp.dot(merged.astype(BF16), wo_ref[...], preferred_element_type=F32)
    o_ref[...] = _layer_norm(alpha * x + y, g_ref[...], b_ref[...])


def _merge(x2d, oa, ob, proj2d, mem, wkv, wg, bg, wpa, wpb, wpc, wo, g, b, alpha, tm):
    t, d = x2d.shape
    tiles_per_seq = t // mem.shape[0] // tm
    assert tiles_per_seq * tm * mem.shape[0] == t
    const = lambda i: (0, 0)
    row = lambda i: (i, 0)
    return pl.pallas_call(
        functools.partial(_merge_kernel, alpha=alpha, tiles_per_seq=tiles_per_seq),
        out_shape=jax.ShapeDtypeStruct((t, d), F32),
        grid=(t // tm,),
        in_specs=[
            pl.BlockSpec((tm, d), row),
            pl.BlockSpec((tm, BRANCH_W), row),
            pl.BlockSpec((tm, BRANCH_W), row),
            pl.BlockSpec((tm, BRANCH_W), lambda i: (i, COL_QC // BRANCH_W)),
            pl.BlockSpec((None, N_MEM, D_MODEL), lambda i: (i // tiles_per_seq, 0, 0)),
            pl.BlockSpec((D_MODEL, 2 * BRANCH_W), const),
            pl.BlockSpec((d, 3 * d), const),
            pl.BlockSpec((1, 3 * d), const),
            pl.BlockSpec((BRANCH_W, d), const),
            pl.BlockSpec((BRANCH_W, d), const),
            pl.BlockSpec((BRANCH_W, d), const),
            pl.BlockSpec((d, d), const),
            pl.BlockSpec((1, d), const),
            pl.BlockSpec((1, d), const),
        ],
        out_specs=pl.BlockSpec((tm, d), row),
        scratch_shapes=[pltpu.VMEM((N_MEM, BRANCH_W), BF16), pltpu.VMEM((N_MEM, BRANCH_W), BF16)],
        compiler_params=pltpu.CompilerParams(
            dimension_semantics=("arbitrary",), vmem_limit_bytes=VMEM_LIMIT),
        name="merge",
    )(x2d, oa, ob, proj2d, mem, wkv, wg, bg, wpa, wpb, wpc, wo, g, b)


def _sort_network(n):
    pairs = []

    def merge(lo, hi, r):
        step = r * 2
        if step < hi - lo:
            merge(lo, hi, step)
            merge(lo + r, hi, step)
            pairs.extend((k, k + r) for k in range(lo + r, hi - r, step))
        else:
            pairs.append((lo, lo + r))

    def sort(lo, hi):
        if hi - lo >= 1:
            mid = lo + (hi - lo) // 2
            sort(lo, mid)
            sort(mid + 1, hi)
            merge(lo, hi, 1)

    sort(0, n - 1)
    return pairs


_SORT16 = _sort_network(PEER_TOPK)


def _top16_desc(slabs, presorted=False):
    v = list(slabs)
    for a, b in ([] if presorted else _SORT16):
        hi, lo = jnp.maximum(v[a], v[b]), jnp.minimum(v[a], v[b])
        v[a], v[b] = hi, lo
    for shift in (4, 2, 1):
        v = [jnp.maximum(v[k], pltpu.roll(v[PEER_TOPK - 1 - k], shift, 0))
             for k in range(PEER_TOPK)]
        step = PEER_TOPK // 2
        while step >= 1:
            for k in range(PEER_TOPK):
                if k & step == 0:
                    hi, lo = jnp.maximum(v[k], v[k + step]), jnp.minimum(v[k], v[k + step])
                    v[k], v[k + step] = hi, lo
            step //= 2
    return v


def _peer_route(s0, s1):
    n = s0.shape[1]
    top_a = _top16_desc([s0[k * SUBLANES:(k + 1) * SUBLANES] for k in range(N_KEYS // SUBLANES)])
    top_b = _top16_desc([s1[k * SUBLANES:(k + 1) * SUBLANES] for k in range(N_KEYS // SUBLANES)])
    a0, b0 = top_a[0], top_b[0]
    ea = [jnp.exp(t - a0) for t in top_a]
    eb = [jnp.exp(t - b0) for t in top_b]
    sub = lax.broadcasted_iota(jnp.int32, (SUBLANES, n), 0)
    first4 = sub < 4

    def candidates(ea_list):
        col = ea_list[0]
        for s in range(1, 4):
            col = jnp.where(sub == s, ea_list[s], col)
        for s in range(4, 8):
            col = jnp.where(sub == s, eb[s - 4], col)
        out = []
        for v in range(PEER_TOPK):
            c = col * jnp.where(first4, eb[v], ea_list[min(v + 4, PEER_TOPK - 1)])
            if v >= PEER_TOPK - 4:
                c = jnp.where(first4, c, -1.0)
            out.append(c)
        return out

    cand = candidates(ea)
    best = _top16_desc(cand, presorted=True)
    z = best[0]
    for t in best[1:]:
        z = z + t
    inv_z = (2.0 ** -0.5) / z
    theta = best[PEER_TOPK - 1]
    ean = [t * inv_z for t in ea]
    cand_n = candidates(ean)
    thr = None
    for c, cn in zip(cand, cand_n):
        t = jnp.where(c >= theta, cn, jnp.inf)
        thr = t if thr is None else jnp.minimum(thr, t)
    for shift in (4, 2, 1):
        thr = jnp.minimum(thr, pltpu.roll(thr, shift, 0))
    thr = thr[0:1]
    e1n = jnp.exp(s0 - a0[0:1]) * inv_z[0:1]
    e2 = jnp.exp(s1 - b0[0:1])
    psi = jnp.full(s0.shape, float(PEER_TOPK), F32)
    for r in range(PEER_TOPK):
        psi = jnp.where(e1n * eb[r][0:1] >= thr, float(PEER_TOPK - 1 - r), psi)
    code2 = jnp.zeros(s1.shape, F32)
    for r in reversed(range(PEER_TOPK)):
        code2 = jnp.where(s1 >= top_b[r][0:1], float(PEER_TOPK - r), code2)
    return e1n, e2, psi, code2


def _peer_kernel(x1_ref, wpq_ref, keys_ref, u_ref, vt_ref, g_ref, b_ref, o_ref,
                 xb_ref, q_ref, e1_ref, psi_ref, e2_ref, code_ref, acc_ref, wa_ref, *,
                 alpha, ec, lane_chunk):
    j = pl.program_id(1)
    tm = x1_ref.shape[0]
    n1 = ec // N_KEYS
    assert n1 % SUBLANES == 0

    @pl.when(j == 0)
    def _():
        xb = x1_ref[...].astype(BF16)
        xb_ref[...] = xb
        acc_ref[...] = jnp.zeros(acc_ref.shape, F32)
        for c in range(0, q_ref.shape[1], 512):
            q_ref[:, c:c + 512] = jnp.dot(xb, wpq_ref[:, c:c + 512],
                                          preferred_element_type=F32).astype(BF16)
        for h in range(PEER_HEADS):
            s = []
            for half in range(2):
                r = 2 * h + half
                s.append(lax.dot_general(keys_ref[r], q_ref[:, r * N_KEYS:(r + 1) * N_KEYS], _NT,
                                         preferred_element_type=F32))
            for lt in range(0, tm, 2 * LANES):
                cols = slice(lt, lt + 2 * LANES)
                e1n, e2, psi, code2 = _peer_route(s[0][:, cols], s[1][:, cols])
                e1_ref[h, :, cols] = e1n
                psi_ref[h, :, cols] = psi
                e2_ref[h, :, cols] = e2.astype(BF16)
                code_ref[h, :, cols] = code2.astype(BF16)

    def packed_row(tile, ii):
        r16 = jnp.broadcast_to(tile[ii:ii + 1, :], (2 * SUBLANES, tile.shape[1])).astype(BF16)
        return jnp.concatenate([r16] * (N_KEYS // (2 * SUBLANES)), axis=0)

    ht = lax.dot_general(u_ref[...], xb_ref[...], _NT, preferred_element_type=F32)
    base = pl.multiple_of(j * n1, SUBLANES)
    for ii in range(n1):
        rows = slice(ii * N_KEYS, (ii + 1) * N_KEYS)
        for lc in range(0, tm, lane_chunk):
            cols = slice(lc, lc + lane_chunk)
            w = None
            grp = pl.multiple_of(base + (ii // SUBLANES) * SUBLANES, SUBLANES)
            for h in range(PEER_HEADS):
                e1b = packed_row(e1_ref[h, pl.ds(grp, SUBLANES), cols], ii % SUBLANES)
                psib = packed_row(psi_ref[h, pl.ds(grp, SUBLANES), cols], ii % SUBLANES)
                p = e1b * e2_ref[h, :, cols]
                t = jnp.where(code_ref[h, :, cols] > psib, p, jnp.zeros_like(p))
                w = t if w is None else w + t
            hblk = ht[rows, cols]
            act = hblk * (1.0 + lax.erf(hblk))
            wa_ref[rows, cols] = w * act.astype(BF16)
    acc_ref[...] += jnp.dot(vt_ref[...], wa_ref[...], preferred_element_type=F32)

    @pl.when(j == pl.num_programs(1) - 1)
    def _():
        z = alpha * x1_ref[...] + acc_ref[...].T
        o_ref[...] = _layer_norm(z, g_ref[...], b_ref[...])


def _peer(x1, wpq, keys, u, vt, g, b, alpha, tm, ec, lane_chunk):
    t, d = x1.shape
    kern = functools.partial(_peer_kernel, alpha=alpha, ec=ec, lane_chunk=lane_chunk)
    n_chunks = N_EXPERTS // ec
    return pl.pallas_call(
        kern,
        out_shape=jax.ShapeDtypeStruct((t, d), F32),
        grid=(t // tm, n_chunks),
        in_specs=[
            pl.BlockSpec((tm, d), lambda i, j: (i, 0)),
            pl.BlockSpec(wpq.shape, lambda i, j: (0, 0)),
            pl.BlockSpec(keys.shape, lambda i, j: (0, 0, 0)),
            pl.BlockSpec((ec, d), lambda i, j: (j, 0)),
            pl.BlockSpec((d, ec), lambda i, j: (0, j)),
            pl.BlockSpec((1, d), lambda i, j: (0, 0)),
            pl.BlockSpec((1, d), lambda i, j: (0, 0)),
        ],
        out_specs=pl.BlockSpec((tm, d), lambda i, j: (i, 0)),
        scratch_shapes=[
            pltpu.VMEM((tm, d), BF16),
            pltpu.VMEM((tm, wpq.shape[1]), BF16),
            pltpu.VMEM((PEER_HEADS, N_KEYS, tm), F32),
            pltpu.VMEM((PEER_HEADS, N_KEYS, tm), F32),
            pltpu.VMEM((PEER_HEADS, N_KEYS, tm), BF16),
            pltpu.VMEM((PEER_HEADS, N_KEYS, tm), BF16),
            pltpu.VMEM((d, tm), F32),
            pltpu.VMEM((ec, tm), BF16),
        ],
        compiler_params=pltpu.CompilerParams(
            dimension_semantics=("parallel", "arbitrary"), vmem_limit_bytes=VMEM_LIMIT),
        name="peer",
    )(x1, wpq, keys, u, vt, g, b)


def _prep_weights(depth, l, w_in, w_mem_kv, lam_q1, lam_k1, lam_q2, lam_k2, subln_g, sink, w_gate,
                  b_gate, w_pa, w_pb, w_pc, w_o, ln1_g, ln1_b, w_pq, sub_keys, peer_u, peer_v,
                  ln2_g, ln2_b):
    w = w_in[l]
    qb_perm = np.concatenate([np.arange(64) + 64 * hq for g in range(4) for hq in (g, g + 4)])
    qa = w[:, 0:512] * (0.125 * math.log2(math.e))
    ka, va = w[:, 512:1024], w[:, 1024:1536]
    qb = w[:, 1536:2048][:, qb_perm] * (0.125 * math.log2(math.e))
    kb, vb, qc = w[:, 2048:2176], w[:, 2176:2304], w[:, 2304:2816]
    w_proj = jnp.concatenate([qa, ka, va, qb, qc, kb, vb], axis=1).astype(BF16)
    lamp = jnp.stack([lam_q1[l], lam_k1[l], lam_q2[l], lam_k2[l]]).astype(F32)
    row = lambda a: a.astype(F32).reshape(1, -1)
    return dict(
        w_proj=w_proj,
        w_mem_kv=w_mem_kv[l].astype(BF16),
        lamp=lamp,
        subln_g=row(subln_g[l]),
        sink=sink[l].astype(F32),
        w_gate=w_gate[l].astype(BF16),
        b_gate=row(b_gate[l]),
        w_pa=w_pa[l].astype(BF16),
        w_pb=w_pb[l][qb_perm].astype(BF16),
        w_pc=w_pc[l].astype(BF16),
        w_o=w_o[l].astype(BF16),
        ln1_g=row(ln1_g[l]), ln1_b=row(ln1_b[l]),
        w_pq=w_pq[l].astype(BF16),
        keys=sub_keys[l].reshape(2 * PEER_HEADS, N_KEYS, N_KEYS).astype(BF16),
        peer_u=(peer_u[l] * (2.0 ** -0.5)).astype(BF16),
        peer_vt=peer_v[l].T.astype(BF16),
        ln2_g=row(ln2_g[l]), ln2_b=row(ln2_b[l]),
        lam_init=0.8 - 0.6 * math.exp(-0.3 * l),
        alpha=(2.0 * depth) ** 0.25,
    )


def _tile(n, pref):
    t = min(n, pref)
    assert n % t == 0, (n, t)
    return t


def _encoder_layer(x, mem, p):
    b, s, d = x.shape
    t = b * s
    x2d = x.reshape(t, d)
    proj, stats = _proj(x2d, p["w_proj"], _tile(t, 512))
    proj, stats = proj.reshape(b, s, PROJ_COLS), stats.reshape(b, s, LANES)
    slopes_a = jnp.asarray(2.0 ** (-8.0 * np.arange(1, DA_HEADS + 1) / DA_HEADS), F32)
    slope_tab = jnp.asarray(_slope_pieces(2.0 ** (-8.0 * np.arange(1, DA_HEADS + 1) / DA_HEADS)))
    slopes_b = jnp.asarray(2.0 ** (-8.0 * np.arange(1, WA_HEADS + 1) / WA_HEADS), F32)
    oa = _attn_a(proj, stats, slopes_a, slope_tab, p["lamp"], p["subln_g"], p["lam_init"], 256, 4,
                 1024 if s >= 8192 else 512)
    ob = _attn_b(proj, slopes_b, p["sink"], _tile(s, 512))
    x1 = _merge(x2d, oa.reshape(t, BRANCH_W), ob.reshape(t, BRANCH_W),
                proj.reshape(t, PROJ_COLS), mem, p["w_mem_kv"], p["w_gate"], p["b_gate"], p["w_pa"], p["w_pb"], p["w_pc"], p["w_o"],
                p["ln1_g"], p["ln1_b"], p["alpha"], _tile(t, 512))
    y = _peer(x1, p["w_pq"], p["keys"], p["peer_u"], p["peer_vt"], p["ln2_g"], p["ln2_b"],
              p["alpha"], _tile(t, 512), 2048, 256)
    return y.reshape(b, s, d)


def kernel(x_prompt, x_sample, mem_prompt, mem_sample, w_in, w_mem_kv, lam_q1, lam_k1, lam_q2,
           lam_k2, subln_g, sink, w_gate, b_gate, w_pa, w_pb, w_pc, w_o, ln1_g, ln1_b, w_pq,
           sub_keys, peer_u, peer_v, ln2_g, ln2_b):
    depth = w_in.shape[0]
    y_prompt, y_sample = x_prompt, x_sample
    for l in range(depth):
        p = _prep_weights(depth, l, w_in, w_mem_kv, lam_q1, lam_k1, lam_q2, lam_k2, subln_g, sink,
                          w_gate, b_gate, w_pa, w_pb, w_pc, w_o, ln1_g, ln1_b, w_pq, sub_keys,
                          peer_u, peer_v, ln2_g, ln2_b)
        y_prompt = _encoder_layer(y_prompt, mem_prompt, p)
        y_sample = _encoder_layer(y_sample, mem_sample, p)
    return (y_prompt, y_sample)
```
